```python
import math
import jax, jax.numpy as jnp
from jax import lax
import numpy as np

D_MODEL = 2048
BATCH = 8
SEQ = 4096
DEPTH = 1

N_Q_HEADS = 32
N_KV_HEADS = 4
GROUP = N_Q_HEADS // N_KV_HEADS
HEAD_DIM = 64
ATTN_WIDTH = N_Q_HEADS * HEAD_DIM
KV_WIDTH = N_KV_HEADS * HEAD_DIM
WINDOW = 128
BLOCK = 128
NEG_INF = -1e30
N_BUCKETS = 32
MAX_DISTANCE = 128
LRU_WIDTH = D_MODEL
LRU_BLOCKS = 16
LRU_BLOCK_W = LRU_WIDTH // LRU_BLOCKS
CONV_WIDTH = 4
LRU_C = 8.0
D_FF = 4 * D_MODEL
EPS = 1e-6
IN_SPLITS = (LRU_WIDTH, LRU_WIDTH, ATTN_WIDTH, KV_WIDTH, KV_WIDTH, D_MODEL, D_MODEL)
IN_WIDTH = sum(IN_SPLITS)

kernel_name = "hybrid_rglru_swa_sink_sqrelu_adaln"


def _rms_norm(x, g):
    xf = x.astype(jnp.float32)
    y = xf * lax.rsqrt(jnp.mean(xf * xf, axis=-1, keepdims=True) + EPS)
    return (y * g.astype(jnp.float32)).astype(x.dtype)


def _modulate(h, shift, scale):
    return h * (1.0 + scale[:, None, :]) + shift[:, None, :]


def _causal_depthwise_conv(x, w, b):
    s = x.shape[1]
    xp = jnp.pad(x, ((0, 0), (CONV_WIDTH - 1, 0), (0, 0)))
    y = b
    for k in range(CONV_WIDTH):
        y = y + xp[:, k:k + s] * w[k]
    return y


def _rg_lru(x, wa, ba, wx, bx, lam):
    b_, s, w = x.shape
    xb = x.reshape(b_, s, LRU_BLOCKS, LRU_BLOCK_W)
    r = jax.nn.sigmoid(jnp.einsum("bshi,hij->bshj", xb, wa).reshape(b_, s, w) + ba)
    i = jax.nn.sigmoid(jnp.einsum("bshi,hij->bshj", xb, wx).reshape(b_, s, w) + bx)
    log_a = -LRU_C * r.astype(jnp.float32) * jax.nn.softplus(-lam.astype(jnp.float32))
    a = jnp.exp(log_a)
    mult = jnp.sqrt(-jnp.expm1(2.0 * log_a))
    mult = jnp.where(jnp.arange(s)[None, :, None] == 0, 1.0, mult)
    u = mult * (i * x).astype(jnp.float32)

    def step(h, au):
        a_t, u_t = au
        h = a_t * h + u_t
        return h, h

    h0 = jnp.zeros((b_, w), jnp.float32)
    _, hs = lax.scan(step, h0, (jnp.swapaxes(a, 0, 1), jnp.swapaxes(u, 0, 1)))
    return jnp.swapaxes(hs, 0, 1).astype(x.dtype)


def _t5_causal_bucket(rel):
    max_exact = N_BUCKETS // 2
    relf = jnp.maximum(rel, 1).astype(jnp.float32)
    large = max_exact + (jnp.log(relf / max_exact) / math.log(MAX_DISTANCE / max_exact)
                         * (N_BUCKETS - max_exact)).astype(jnp.int32)
    large = jnp.minimum(large, N_BUCKETS - 1)
    return jnp.where(rel < max_exact, rel, large)


def _band_bias_and_mask(rel_bias, n_blocks):
    qi = jnp.arange(BLOCK)[:, None]
    ki = jnp.arange(2 * BLOCK)[None, :]
    rel = qi + BLOCK - ki
    bucket = _t5_causal_bucket(jnp.maximum(rel, 0))
    bias = jnp.transpose(rel_bias[bucket], (2, 0, 1)).astype(jnp.float32)
    bias = bias.reshape(N_KV_HEADS, GROUP, BLOCK, 2 * BLOCK)
    kpos = jnp.arange(n_blocks)[:, None, None] * BLOCK - BLOCK + ki[None]
    valid = (kpos >= 0) & (rel[None] >= 0) & (rel[None] < WINDOW)
    return bias, valid


def _swa_sink_attention(q, k, v, sinks, rel_bias):
    b_, s, _ = q.shape
    n = s // BLOCK
    bias, valid = _band_bias_and_mask(rel_bias, n)
    q = q.reshape(b_, n, BLOCK, N_KV_HEADS, GROUP, HEAD_DIM)

    def band(t):
        t = t.reshape(b_, s, N_KV_HEADS, HEAD_DIM)
        tp = jnp.pad(t, ((0, 0), (BLOCK, 0), (0, 0), (0, 0)))
        tp = tp.reshape(b_, n + 1, BLOCK, N_KV_HEADS, HEAD_DIM)
        return jnp.concatenate([tp[:, :-1], tp[:, 1:]], axis=2)

    kb, vb = band(k), band(v)
    logits = jnp.einsum("bnqkgd,bnskd->bnkgqs", q, kb,
                        preferred_element_type=jnp.float32) * (HEAD_DIM ** -0.5)
    logits = jnp.where(valid[None, :, None, None], logits + bias[None, None], NEG_INF)
    sink = sinks.astype(jnp.float32).reshape(N_KV_HEADS, GROUP)[None, None, :, :, None, None]
    m = jnp.maximum(jnp.max(logits, axis=-1, keepdims=True), sink)
    e = jnp.exp(logits - m)
    p = e / (jnp.sum(e, axis=-1, keepdims=True) + jnp.exp(sink - m))
    o = jnp.einsum("bnkgqs,bnskd->bnqkgd", p.astype(vb.dtype), vb)
    return o.reshape(b_, s, ATTN_WIDTH)


def setup_inputs(seed: int = 0) -> dict:
    key = jax.random.key(seed)
    ks = jax.random.split(key, 24)
    f32 = jnp.float32
    L, D = DEPTH, D_MODEL

    def nrm(k, shape, scale):
        return jax.random.normal(k, shape, f32) * scale

    a_c = jax.random.uniform(ks[12], (L, LRU_WIDTH), f32, 0.9, 0.999)
    a0 = a_c ** (1.0 / LRU_C)
    lam = jnp.log(a0) - jnp.log1p(-a0)
    return {
        "x": nrm(ks[0], (BATCH, SEQ, D), 1.0),
        "c": nrm(ks[1], (BATCH, D), 1.0),
        "w_ada": nrm(ks[2], (L, D, 6 * D), 0.5 * D ** -0.5),
        "b_ada": nrm(ks[3], (L, 6 * D), 0.02),
        "norm1_g": 1.0 + nrm(ks[4], (L, D), 0.02),
        "w_in": nrm(ks[5], (L, D, IN_WIDTH), D ** -0.5),
        "conv_w": nrm(ks[6], (L, CONV_WIDTH, LRU_WIDTH), CONV_WIDTH ** -0.5),
        "conv_b": nrm(ks[7], (L, LRU_WIDTH), 0.02),
        "lru_wa": nrm(ks[8], (L, LRU_BLOCKS, LRU_BLOCK_W, LRU_BLOCK_W), LRU_BLOCK_W ** -0.5),
        "lru_ba": nrm(ks[9], (L, LRU_WIDTH), 0.02),
        "lru_wx": nrm(ks[10], (L, LRU_BLOCKS, LRU_BLOCK_W, LRU_BLOCK_W), LRU_BLOCK_W ** -0.5),
        "lru_bx": nrm(ks[11], (L, LRU_WIDTH), 0.02),
        "lru_lambda": lam,
        "w_lru_out": nrm(ks[13], (L, LRU_WIDTH, D), LRU_WIDTH ** -0.5),
        "w_attn_out": nrm(ks[14], (L, ATTN_WIDTH, D), ATTN_WIDTH ** -0.5),
        "attn_sinks": nrm(ks[15], (L, N_Q_HEADS), 1.0),
        "rel_bias": nrm(ks[16], (N_BUCKETS, N_Q_HEADS), 0.5),
        "w_out": nrm(ks[17], (L, D, D), D ** -0.5),
        "norm2_g": 1.0 + nrm(ks[18], (L, D), 0.02),
        "w_ff1": nrm(ks[19], (L, D, D_FF), D ** -0.5),
        "w_ff2": nrm(ks[20], (L, D_FF, D), D_FF ** -0.5),
        "final_g": 1.0 + nrm(ks[21], (D,), 0.02),
    }


def reference(x, c, w_ada, b_ada, norm1_g, w_in, conv_w, conv_b, lru_wa, lru_ba, lru_wx,
              lru_bx, lru_lambda, w_lru_out, w_attn_out, attn_sinks, rel_bias, w_out,
              norm2_g, w_ff1, w_ff2, final_g):
    split_idx = list(np.cumsum(IN_SPLITS)[:-1])
    c_act = jax.nn.silu(c)
    for l in range(DEPTH):
        mod = jnp.dot(c_act, w_ada[l]) + b_ada[l]
        shift1, scale1, gate1, shift2, scale2, gate2 = jnp.split(mod, 6, axis=-1)

        h = _modulate(_rms_norm(x, norm1_g[l]), shift1, scale1)
        proj = jnp.einsum("bsd,de->bse", h, w_in[l])
        lru_x, lru_gate, q, k, v, g_a, g_b = jnp.split(proj, split_idx, axis=-1)

        xc = _causal_depthwise_conv(lru_x, conv_w[l], conv_b[l])
        rec = _rg_lru(xc, lru_wa[l], lru_ba[l], lru_wx[l], lru_bx[l], lru_lambda[l])
        y_a = jnp.einsum("bsw,wd->bsd", rec * jax.nn.gelu(lru_gate, approximate=True),
                         w_lru_out[l])

        att = _swa_sink_attention(q, k, v, attn_sinks[l], rel_bias)
        y_b = jnp.einsum("bsw,wd->bsd", att, w_attn_out[l])

        merged = jax.nn.sigmoid(g_a) * y_a + jax.nn.sigmoid(g_b) * y_b
        x = x + gate1[:, None, :] * jnp.einsum("bsd,de->bse", merged, w_out[l])

        h2 = _modulate(_rms_norm(x, norm2_g[l]), shift2, scale2)
        ff = jnp.square(jax.nn.relu(jnp.einsum("bsd,df->bsf", h2, w_ff1[l])))
        x = x + gate2[:, None, :] * jnp.einsum("bsf,fd->bsd", ff, w_ff2[l])

    return _rms_norm(x, final_g)
```

```python
import functools
import math

import jax
import jax.numpy as jnp
from jax import lax
from jax.experimental import pallas as pl
from jax.experimental.pallas import tpu as pltpu

EPS = 1e-6
LRU_C = 8.0
ATTN_BLOCK = 128
NEG_INF = -1e30
MAX_DISTANCE = 128
LANES = 128
SUBLANES = 8
VMEM_LIMIT = 56 * 1024 * 1024

F32 = jnp.float32
BF16 = jnp.bfloat16


def _tile(n, target, align):
    best = None
    t = align
    while t <= min(n, target):
        if n % t == 0:
            best = t
        t += align
    if best is None:
        raise ValueError(f"no tile for {n} (target {target}, align {align})")
    return best


def _params(*sem):
    return pltpu.CompilerParams(dimension_semantics=sem, vmem_limit_bytes=VMEM_LIMIT)


def _sigmoid(v):
    return 1.0 / (1.0 + jnp.exp(-v))


def _adaln_kernel(c_ref, w_ref, b_ref, o_ref):
    c = c_ref[...]
    act = (c * _sigmoid(c)).astype(BF16)
    o_ref[...] = jnp.dot(act, w_ref[...].astype(BF16), preferred_element_type=F32) + b_ref[...]


def _adaln(c, w, b):
    bsz, d = c.shape
    n = w.shape[1]
    tn = _tile(n, 1024, LANES)
    return pl.pallas_call(
        _adaln_kernel,
        grid=(n // tn,),
        in_specs=[pl.BlockSpec((bsz, d), lambda j: (0, 0)),
                  pl.BlockSpec((d, tn), lambda j: (0, j)),
                  pl.BlockSpec((1, tn), lambda j: (0, j))],
        out_specs=pl.BlockSpec((bsz, tn), lambda j: (0, j)),
        out_shape=jax.ShapeDtypeStruct((bsz, n), F32),
        compiler_params=_params("parallel"),
        name="adaln_mod",
    )(c, w, b.reshape(1, n))


def _rms_mod(x, g, shift, scale):
    var = jnp.mean(x * x, axis=-1, keepdims=True)
    y = x * lax.rsqrt(var + EPS) * g
    return y * (1.0 + scale) + shift


def _in_proj_kernel(x_ref, sh_ref, sc_ref, g_ref, w_ref, o_ref, h_ref):
    @pl.when(pl.program_id(1) == 0)
    def _():
        h_ref[...] = _rms_mod(x_ref[...], g_ref[...], sh_ref[...], sc_ref[...]).astype(BF16)

    o_ref[...] = jnp.dot(h_ref[...], w_ref[...], preferred_element_type=F32).astype(o_ref.dtype)


def _in_proj(x2d, shift, scale, g, w, seq):
    m, d = x2d.shape
    n = w.shape[1]
    tm = _tile(seq, 1024, SUBLANES)
    tn = _tile(n, 1536, LANES)
    per_b = seq // tm
    row = lambda i, j: (i // per_b, 0, 0)
    return pl.pallas_call(
        _in_proj_kernel,
        grid=(m // tm, n // tn),
        in_specs=[pl.BlockSpec((tm, d), lambda i, j: (i, 0)),
                  pl.BlockSpec((None, 1, d), row),
                  pl.BlockSpec((None, 1, d), row),
                  pl.BlockSpec((1, d), lambda i, j: (0, 0)),
                  pl.BlockSpec((d, tn), lambda i, j: (0, j))],
        out_specs=pl.BlockSpec((tm, tn), lambda i, j: (i, j)),
        out_shape=jax.ShapeDtypeStruct((m, n), BF16),
        scratch_shapes=[pltpu.VMEM((tm, d), BF16)],
        compiler_params=_params("parallel", "arbitrary"),
        name="in_proj",
    )(x2d, shift, scale, g, w)


def _gelu_tanh(v):
    return 0.5 * v * (1.0 + jnp.tanh(math.sqrt(2.0 / math.pi) * (v + 0.044715 * (v * v * v))))


def _rglru_kernel(x_ref, gate_ref, w_ref, cw_ref, cb_ref, ba_ref, bx_ref, lam_ref, o_ref,
                  xbuf, hcar, *, conv_k):
    t = pl.program_id(2)
    tt, cw = x_ref.shape
    groups = tt // SUBLANES

    @pl.when(t == 0)
    def _():
        xbuf[0:SUBLANES, :] = jnp.zeros((SUBLANES, cw), F32)
        hcar[...] = jnp.zeros_like(hcar)

    x = x_ref[...].astype(F32)
    xbuf[SUBLANES:SUBLANES + tt, :] = x
    xc = cb_ref[...] + cw_ref[conv_k - 1:conv_k, :] * x
    for k in range(conv_k - 1):
        xc = xc + cw_ref[k:k + 1, :] * xbuf[pl.ds(SUBLANES - (conv_k - 1) + k, tt), :]
    xbuf[0:SUBLANES, :] = x[tt - SUBLANES:, :]

    lam = lam_ref[...]
    softplus_neg_lam = jnp.maximum(-lam, 0.0) + jnp.log(1.0 + jnp.exp(-jnp.abs(lam)))
    log_a_coef = -LRU_C * softplus_neg_lam
    row_in_group = lax.broadcasted_iota(jnp.int32, (groups, SUBLANES, LANES), 1)
    first_row = (lax.broadcasted_iota(jnp.int32, (tt, LANES), 0) == 0) & (t == 0)

    for c in range(cw // LANES):
        sl = slice(c * LANES, (c + 1) * LANES)
        xh = xc[:, sl]
        z = jnp.dot(xh.astype(BF16), w_ref[c], preferred_element_type=F32)
        r = _sigmoid(z[:, :LANES] + ba_ref[:, sl])
        i = _sigmoid(z[:, LANES:] + bx_ref[:, sl])
        log_a = log_a_coef[:, sl] * r
        a = jnp.exp(log_a)
        mult = jnp.sqrt(1.0 - jnp.exp(2.0 * log_a))
        mult = jnp.where(first_row, 1.0, mult)
        u = mult * (i * xh)

        a3 = a.reshape(groups, SUBLANES, LANES)
        u3 = u.reshape(groups, SUBLANES, LANES)
        for dist in (1, 2, 4):
            keep = row_in_group >= dist
            a_prev = pltpu.roll(a3, dist, axis=1)
            u_prev = pltpu.roll(u3, dist, axis=1)
            u3 = jnp.where(keep, a3 * u_prev, 0.0) + u3
            a3 = jnp.where(keep, a3 * a_prev, a3)
        h_prev = hcar[0:1, sl]
        hs = []
        for gi in range(groups):
            hg = a3[gi] * h_prev + u3[gi]
            hs.append(hg)
            h_prev = hg[SUBLANES - 1:SUBLANES, :]
        hcar[0:1, sl] = h_prev
        h = jnp.concatenate(hs, axis=0)
        gate = gate_ref[:, sl].astype(F32)
        o_ref[:, sl] = (h * _gelu_tanh(gate)).astype(o_ref.dtype)


def _rglru(proj, wcat, conv_w, conv_b, ba, bx, lam, bsz, seq, lru_w):
    m = proj.shape[0]
    conv_k = conv_w.shape[0]
    tt = _tile(seq, 512, SUBLANES)
    cw = _tile(lru_w, 512, LANES)
    per_b = seq // tt
    gate_off = lru_w // cw
    vec = pl.BlockSpec((1, cw), lambda b, c, t: (0, c))
    return pl.pallas_call(
        functools.partial(_rglru_kernel, conv_k=conv_k),
        grid=(bsz, lru_w // cw, per_b),
        in_specs=[pl.BlockSpec((tt, cw), lambda b, c, t: (b * per_b + t, c)),
                  pl.BlockSpec((tt, cw), lambda b, c, t: (b * per_b + t, gate_off + c)),
                  pl.BlockSpec((cw // LANES, LANES, 2 * LANES), lambda b, c, t: (c, 0, 0)),
                  pl.BlockSpec((conv_k, cw), lambda b, c, t: (0, c)),
                  vec, vec, vec, vec],
        out_specs=pl.BlockSpec((tt, cw), lambda b, c, t: (b * per_b + t, c)),
        out_shape=jax.ShapeDtypeStruct((m, lru_w), BF16),
        scratch_shapes=[pltpu.VMEM((tt + SUBLANES, cw), F32), pltpu.VMEM((SUBLANES, cw), F32)],
        compiler_params=_params("parallel", "parallel", "arbitrary"),
        name="rglru",
    )(proj, proj, wcat, conv_w, conv_b.reshape(1, -1), ba.reshape(1, -1), bx.reshape(1, -1),
      lam.reshape(1, -1))


def _t5_bucket(rel, n_buckets):
    max_exact = n_buckets // 2
    relf = jnp.maximum(rel, 1).astype(F32)
    large = max_exact + (jnp.log(relf / max_exact) / math.log(MAX_DISTANCE / max_exact)
                         * (n_buckets - max_exact)).astype(jnp.int32)
    large = jnp.minimum(large, n_buckets - 1)
    return jnp.where(rel < max_exact, rel, large)


def _band_bias(rel_bias):
    n_buckets = rel_bias.shape[0]
    qi = jnp.arange(ATTN_BLOCK)[:, None]
    ki = jnp.arange(2 * ATTN_BLOCK)[None, :]
    rel = qi + ATTN_BLOCK - ki
    bucket = _t5_bucket(jnp.maximum(rel, 0), n_buckets)
    bias = jnp.transpose(rel_bias[bucket], (2, 0, 1)).astype(F32)
    valid = (rel >= 0) & (rel < ATTN_BLOCK)
    later = jnp.where(valid[None], bias, NEG_INF)
    first = jnp.where((valid & (ki >= ATTN_BLOCK))[None], bias, NEG_INF)
    return jnp.stack([first, later])


def _attn_kernel(sink_ref, q_ref, kp_ref, kc_ref, vp_ref, vc_ref, bm_ref, o_ref, *,
                 n_kv, group, hd, scale):
    fold = math.log2(scale).is_integer()
    for kv in range(n_kv):
        ks = slice(kv * hd, (kv + 1) * hd)
        kk = jnp.concatenate([kp_ref[:, ks], kc_ref[:, ks]], axis=0)
        vv = jnp.concatenate([vp_ref[:, ks], vc_ref[:, ks]], axis=0)
        for g in range(group):
            h = kv * group + g
            hs = slice(h * hd, (h + 1) * hd)
            q = q_ref[:, hs]
            if fold:
                q = q * scale
            s = lax.dot_general(q, kk, (((1,), (1,)), ((), ())), preferred_element_type=F32)
            if not fold:
                s = s * scale
            s = s + bm_ref[h]
            sink = sink_ref[h]
            mx = jnp.maximum(jnp.max(s, axis=-1, keepdims=True), sink)
            e = jnp.exp(s - mx)
            den = jnp.sum(e, axis=-1, keepdims=True) + jnp.exp(sink - mx)
            o = jnp.dot(e.astype(BF16), vv, preferred_element_type=F32)
            o_ref[:, hs] = (o / den).astype(o_ref.dtype)


def _attention(proj, sinks, band_bias, bsz, seq, q_off, k_off, v_off, attn_w, kv_w):
    m = proj.shape[0]
    n_heads = sinks.shape[0]
    hd = attn_w // n_heads
    n_kv = kv_w // hd
    nblk = seq // ATTN_BLOCK
    assert q_off % attn_w == 0 and k_off % kv_w == 0 and v_off % kv_w == 0
    qb, kb, vb = q_off // attn_w, k_off // kv_w, v_off // kv_w
    cur = lambda col: (lambda b, n: (b * nblk + n, col))
    prev = lambda col: (lambda b, n: (b * nblk + jnp.maximum(n - 1, 0), col))
    kernel = functools.partial(_attn_kernel, n_kv=n_kv, group=n_heads // n_kv, hd=hd,
                               scale=hd ** -0.5)
    return pl.pallas_call(
        kernel,
        grid=(bsz, nblk),
        in_specs=[pl.BlockSpec(memory_space=pltpu.SMEM),
                  pl.BlockSpec((ATTN_BLOCK, attn_w), cur(qb)),
                  pl.BlockSpec((ATTN_BLOCK, kv_w), prev(kb)),
                  pl.BlockSpec((ATTN_BLOCK, kv_w), cur(kb)),
                  pl.BlockSpec((ATTN_BLOCK, kv_w), prev(vb)),
                  pl.BlockSpec((ATTN_BLOCK, kv_w), cur(vb)),
                  pl.BlockSpec((None, n_heads, ATTN_BLOCK, 2 * ATTN_BLOCK),
                               lambda b, n: (jnp.minimum(n, 1), 0, 0, 0))],
        out_specs=pl.BlockSpec((ATTN_BLOCK, attn_w), lambda b, n: (b * nblk + n, 0)),
        out_shape=jax.ShapeDtypeStruct((m, attn_w), BF16),
        compiler_params=_params("parallel", "arbitrary"),
        name="swa_attn",
    )(sinks, proj, proj, proj, proj, proj, band_bias)


def _merge_kernel(rec_ref, att_ref, ga_ref, gb_ref, wl_ref, wa_ref, o_ref):
    ya = jnp.dot(rec_ref[...], wl_ref[...], preferred_element_type=F32)
    yb = jnp.dot(att_ref[...], wa_ref[...], preferred_element_type=F32)
    ga = _sigmoid(ga_ref[...].astype(F32))
    gb = _sigmoid(gb_ref[...].astype(F32))
    o_ref[...] = (ga * ya + gb * yb).astype(o_ref.dtype)


def _merge(rec, att, proj, w_lru, w_att, ga_off, gb_off):
    m, kw = rec.shape
    d = w_lru.shape[1]
    tm = _tile(m, 1024, SUBLANES)
    tn = _tile(math.gcd(math.gcd(ga_off, gb_off), d), 512, LANES)
    return pl.pallas_call(
        _merge_kernel,
        grid=(m // tm, d // tn),
        in_specs=[pl.BlockSpec((tm, kw), lambda i, j: (i, 0)),
                  pl.BlockSpec((tm, att.shape[1]), lambda i, j: (i, 0)),
                  pl.BlockSpec((tm, tn), lambda i, j: (i, ga_off // tn + j)),
                  pl.BlockSpec((tm, tn), lambda i, j: (i, gb_off // tn + j)),
                  pl.BlockSpec((kw, tn), lambda i, j: (0, j)),
                  pl.BlockSpec((att.shape[1], tn), lambda i, j: (0, j))],
        out_specs=pl.BlockSpec((tm, tn), lambda i, j: (i, j)),
        out_shape=jax.ShapeDtypeStruct((m, d), BF16),
        compiler_params=_params("parallel", "arbitrary"),
        name="merge",
    )(rec, att, proj, proj, w_lru, w_att)


def _out_proj_kernel(a_ref, w_ref, x_ref, gate_ref, o_ref):
    y = jnp.dot(a_ref[...], w_ref[...], preferred_element_type=F32)
    o_ref[...] = x_ref[...] + gate_ref[...] * y


def _out_proj(merged, w, x2d, gate, seq):
    m, kw = merged.shape
    d = w.shape[1]
    tm = _tile(seq, 1024, SUBLANES)
    tn = _tile(d, 1024, LANES)
    per_b = seq // tm
    return pl.pallas_call(
        _out_proj_kernel,
        grid=(m // tm, d // tn),
        in_specs=[pl.BlockSpec((tm, kw), lambda i, j: (i, 0)),
                  pl.BlockSpec((kw, tn), lambda i, j: (0, j)),
                  pl.BlockSpec((tm, tn), lambda i, j: (i, j)),
                  pl.BlockSpec((None, 1, tn), lambda i, j: (i // per_b, 0, j))],
        out_specs=pl.BlockSpec((tm, tn), lambda i, j: (i, j)),
        out_shape=jax.ShapeDtypeStruct((m, d), F32),
        compiler_params=_params("parallel", "arbitrary"),
        name="out_proj",
    )(merged, w, x2d, gate)


def _mlp_kernel(x_ref, sh_ref, sc_ref, gate_ref, g2_ref, gf_ref, w1_ref, w2_ref, o_ref, h_ref):
    f = pl.program_id(1)

    @pl.when(f == 0)
    def _():
        h_ref[...] = _rms_mod(x_ref[...], g2_ref[...], sh_ref[...], sc_ref[...]).astype(BF16)

    ff = jnp.dot(h_ref[...], w1_ref[...], preferred_element_type=F32)
    ff = jnp.square(jnp.maximum(ff, 0.0)).astype(BF16)
    part = jnp.dot(ff, w2_ref[...], preferred_element_type=F32)

    @pl.when(f == 0)
    def _():
        o_ref[...] = part

    @pl.when(f > 0)
    def _():
        o_ref[...] += part

    @pl.when(f == pl.num_programs(1) - 1)
    def _():
        x2 = x_ref[...] + gate_ref[...] * o_ref[...]
        var = jnp.mean(x2 * x2, axis=-1, keepdims=True)
        o_ref[...] = x2 * lax.rsqrt(var + EPS) * gf_ref[...]


def _mlp(x1, shift, scale, gate, g2, gf, w1, w2, seq):
    m, d = x1.shape
    dff = w1.shape[1]
    tm = _tile(seq, 512, SUBLANES)
    tf = _tile(dff, 1024, LANES)
    per_b = seq // tm
    row = lambda i, f: (i // per_b, 0, 0)
    const = lambda i, f: (0, 0)
    return pl.pallas_call(
        _mlp_kernel,
        grid=(m // tm, dff // tf),
        in_specs=[pl.BlockSpec((tm, d), lambda i, f: (i, 0)),
                  pl.BlockSpec((None, 1, d), row),
                  pl.BlockSpec((None, 1, d), row),
                  pl.BlockSpec((None, 1, d), row),
                  pl.BlockSpec((1, d), const),
                  pl.BlockSpec((1, d), const),
                  pl.BlockSpec((d, tf), lambda i, f: (0, f)),
                  pl.BlockSpec((tf, d), lambda i, f: (f, 0))],
        out_specs=pl.BlockSpec((tm, d), lambda i, f: (i, 0)),
        out_shape=jax.ShapeDtypeStruct((m, d), F32),
        scratch_shapes=[pltpu.VMEM((tm, d), BF16)],
        compiler_params=_params("parallel", "arbitrary"),
        name="mlp",
    )(x1, shift, scale, gate, g2, gf, w1, w2)


def kernel(x, c, w_ada, b_ada, norm1_g, w_in, conv_w, conv_b, lru_wa, lru_ba, lru_wx, lru_bx,
           lru_lambda, w_lru_out, w_attn_out, attn_sinks, rel_bias, w_out, norm2_g, w_ff1, w_ff2,
           final_g):
    bsz, seq, d = x.shape
    depth = w_in.shape[0]
    lru_w = w_lru_out.shape[1]
    attn_w = w_attn_out.shape[1]
    kv_w = (w_in.shape[2] - 2 * lru_w - attn_w - 2 * d) // 2
    q_off = 2 * lru_w
    k_off = q_off + attn_w
    v_off = k_off + kv_w
    ga_off = v_off + kv_w
    gb_off = ga_off + d

    band_bias = _band_bias(rel_bias)
    xs = x.reshape(bsz * seq, d)
    for l in range(depth):
        mod = _adaln(c, w_ada[l], b_ada[l])
        shift1, scale1, gate1, shift2, scale2, gate2 = [
            t.reshape(bsz, 1, d) for t in jnp.split(mod, 6, axis=-1)]

        proj = _in_proj(xs, shift1, scale1, norm1_g[l].reshape(1, d), w_in[l].astype(BF16), seq)
        wcat = jnp.concatenate([lru_wa[l], lru_wx[l]], axis=-1).astype(BF16)
        rec = _rglru(proj, wcat, conv_w[l], conv_b[l], lru_ba[l], lru_bx[l], lru_lambda[l],
                     bsz, seq, lru_w)
        att = _attention(proj, attn_sinks[l], band_bias, bsz, seq, q_off, k_off, v_off,
                         attn_w, kv_w)
        merged = _merge(rec, att, proj, w_lru_out[l].astype(BF16), w_attn_out[l].astype(BF16),
                        ga_off, gb_off)
        x1 = _out_proj(merged, w_out[l].astype(BF16), xs, gate1, seq)
        last = l == depth - 1
        if not last:
            raise NotImplementedError("the fused final norm assumes a single layer")
        xs = _mlp(x1, shift2, scale2, gate2, norm2_g[l].reshape(1, d), final_g.reshape(1, d),
                  w_ff1[l].astype(BF16), w_ff2[l].astype(BF16), seq)
    return xs.reshape(bsz, seq, d)
```

```python
import functools
import math

import jax
import jax.numpy as jnp
from jax import lax
from jax.experimental import pallas as pl
from jax.experimental.pallas import tpu as pltpu

EPS = 1e-6
LRU_C = 8.0
LOG2E = 1.4426950408889634
ATTN_BLOCK = 128
NEG_INF = -1e30
MAX_DISTANCE = 128
LANES = 128
SUBLANES = 8
VMEM_LIMIT = 56 * 1024 * 1024

F32 = jnp.float32
BF16 = jnp.bfloat16


def _tile(n, target, align):
    best = None
    t = align
    while t <= min(n, target):
        if n % t == 0:
            best = t
        t += align
    if best is None:
        raise ValueError(f"no tile for {n} (target {target}, align {align})")
    return best


def _params(*sem):
    return pltpu.CompilerParams(dimension_semantics=sem, vmem_limit_bytes=VMEM_LIMIT)


def _sigmoid(v):
    return 1.0 / (1.0 + jnp.exp2(v * -LOG2E))


def _adaln_kernel(c_ref, w_ref, b_ref, o_ref):
    c = c_ref[...]
    act = (c * _sigmoid(c)).astype(BF16)
    o_ref[...] = jnp.dot(act, w_ref[...].astype(BF16), preferred_element_type=F32) + b_ref[...]


def _adaln(c, w, b):
    bsz, d = c.shape
    n = w.shape[1]
    tn = _tile(n, 1024, LANES)
    return pl.pallas_call(
        _adaln_kernel,
        grid=(n // tn,),
        in_specs=[pl.BlockSpec((bsz, d), lambda j: (0, 0)),
                  pl.BlockSpec((d, tn), lambda j: (0, j)),
                  pl.BlockSpec((1, tn), lambda j: (0, j))],
        out_specs=pl.BlockSpec((bsz, tn), lambda j: (0, j)),
        out_shape=jax.ShapeDtypeStruct((bsz, n), F32),
        compiler_params=_params("parallel"),
        name="adaln_mod",
    )(c, w, b.reshape(1, n))


def _rms_mod(x, g, shift, scale):
    var = jnp.mean(x * x, axis=-1, keepdims=True)
    y = x * lax.rsqrt(var + EPS) * g
    return y * (1.0 + scale) + shift


def _in_proj_kernel(x_ref, sh_ref, sc_ref, g_ref, w_ref, o_ref, h_ref):
    @pl.when(pl.program_id(1) == 0)
    def _():
        h_ref[...] = _rms_mod(x_ref[...], g_ref[...], sh_ref[...], sc_ref[...]).astype(BF16)

    o_ref[...] = jnp.dot(h_ref[...], w_ref[...], preferred_element_type=F32).astype(o_ref.dtype)


def _in_proj(x2d, shift, scale, g, w, seq):
    m, d = x2d.shape
    n = w.shape[1]
    tm = _tile(seq, 1024, SUBLANES)
    tn = _tile(n, 1536, LANES)
    per_b = seq // tm
    row = lambda i, j: (i // per_b, 0, 0)
    return pl.pallas_call(
        _in_proj_kernel,
        grid=(m // tm, n // tn),
        in_specs=[pl.BlockSpec((tm, d), lambda i, j: (i, 0)),
                  pl.BlockSpec((None, 1, d), row),
                  pl.BlockSpec((None, 1, d), row),
                  pl.BlockSpec((1, d), lambda i, j: (0, 0)),
                  pl.BlockSpec((d, tn), lambda i, j: (0, j))],
        out_specs=pl.BlockSpec((tm, tn), lambda i, j: (i, j)),
        out_shape=jax.ShapeDtypeStruct((m, n), BF16),
        scratch_shapes=[pltpu.VMEM((tm, d), BF16)],
        compiler_params=_params("parallel", "arbitrary"),
        name="in_proj",
    )(x2d, shift, scale, g, w)


def _gelu_tanh(v):
    k1 = -2.0 * math.sqrt(2.0 / math.pi) * LOG2E
    return v / (1.0 + jnp.exp2(v * (k1 + (k1 * 0.044715) * (v * v))))


def _rglru_kernel(x_ref, gate_ref, w_ref, cw_ref, cb_ref, ba_ref, bx_ref, lam_ref, o_ref,
                  xtail, hcar, *, conv_k):
    t = pl.program_id(2)
    tt, cw = x_ref.shape
    groups = tt // SUBLANES

    @pl.when(t == 0)
    def _():
        xtail[...] = jnp.zeros_like(xtail)
        hcar[...] = jnp.zeros_like(hcar)

    assert conv_k - 1 < SUBLANES
    x = x_ref[...].astype(F32)
    xs = jnp.concatenate([xtail[...], x], axis=0).reshape(groups + 1, SUBLANES, cw)
    xtail[...] = x[tt - SUBLANES:, :]
    first = lax.broadcasted_iota(jnp.int32, (groups + 1, SUBLANES, cw), 1) == 0
    xc = cb_ref[...] + cw_ref[conv_k - 1:conv_k, :] * x
    for k in range(conv_k - 2, -1, -1):
        rot = pltpu.roll(xs, 1, axis=1)
        xs = jnp.where(first, jnp.concatenate([rot[:1], rot[:-1]], axis=0), rot)
        xc = xc + cw_ref[k:k + 1, :] * xs[1:].reshape(tt, cw)

    lam = lam_ref[...]
    softplus_neg_lam = jnp.maximum(-lam, 0.0) + jnp.log(1.0 + jnp.exp(-jnp.abs(lam)))
    log2_a_coef = (-LRU_C * LOG2E) * softplus_neg_lam
    row0 = lax.broadcasted_iota(jnp.int32, (groups, SUBLANES, LANES), 1) == 0
    seq_start = (lax.broadcasted_iota(jnp.int32, (SUBLANES, LANES), 0) == 0) & (t == 0)

    for c in range(cw // LANES):
        sl = slice(c * LANES, (c + 1) * LANES)
        xh = xc[:, sl]
        z = jnp.dot(xh.astype(BF16), w_ref[c], preferred_element_type=F32)
        r = _sigmoid(z[:, :LANES] + ba_ref[:, sl])
        i = _sigmoid(z[:, LANES:] + bx_ref[:, sl])
        a = jnp.exp2(log2_a_coef[:, sl] * r)
        om = 1.0 - a * a
        mult = jnp.where(om > 0.0, om * lax.rsqrt(om), 0.0)
        mult = jnp.concatenate([jnp.where(seq_start, 1.0, mult[:SUBLANES]), mult[SUBLANES:]],
                               axis=0)
        u3 = (mult * (i * xh)).reshape(groups, SUBLANES, LANES)
        a3 = a.reshape(groups, SUBLANES, LANES)

        az = jnp.where(row0, 0.0, a3)
        a_in = jnp.where(row0, a3, 0.0)
        az2 = az * pltpu.roll(az, 1, axis=1)
        az4 = az2 * pltpu.roll(az2, 2, axis=1)
        h_prev = hcar[0:1, sl]
        hs = []
        for gi in range(groups):
            hg = u3[gi] + a_in[gi] * h_prev
            hg = hg + az[gi] * pltpu.roll(hg, 1, axis=0)
            hg = hg + az2[gi] * pltpu.roll(hg, 2, axis=0)
            hg = hg + az4[gi] * pltpu.roll(hg, 4, axis=0)
            hs.append(hg)
            h_prev = hg[SUBLANES - 1:SUBLANES, :]
        hcar[0:1, sl] = h_prev
        h = jnp.concatenate(hs, axis=0)
        gate = gate_ref[:, sl].astype(F32)
        o_ref[:, sl] = (h * _gelu_tanh(gate)).astype(o_ref.dtype)


def _rglru(proj, wcat, conv_w, conv_b, ba, bx, lam, bsz, seq, lru_w):
    m = proj.shape[0]
    conv_k = conv_w.shape[0]
    tt = _tile(seq, 256, SUBLANES)
    cw = _tile(lru_w, 1024, LANES)
    per_b = seq // tt
    gate_off = lru_w // cw
    vec = pl.BlockSpec((1, cw), lambda b, c, t: (0, c))
    return pl.pallas_call(
        functools.partial(_rglru_kernel, conv_k=conv_k),
        grid=(bsz, lru_w // cw, per_b),
        in_specs=[pl.BlockSpec((tt, cw), lambda b, c, t: (b * per_b + t, c)),
                  pl.BlockSpec((tt, cw), lambda b, c, t: (b * per_b + t, gate_off + c)),
                  pl.BlockSpec((cw // LANES, LANES, 2 * LANES), lambda b, c, t: (c, 0, 0)),
                  pl.BlockSpec((conv_k, cw), lambda b, c, t: (0, c)),
                  vec, vec, vec, vec],
        out_specs=pl.BlockSpec((tt, cw), lambda b, c, t: (b * per_b + t, c)),
        out_shape=jax.ShapeDtypeStruct((m, lru_w), BF16),
        scratch_shapes=[pltpu.VMEM((SUBLANES, cw), F32), pltpu.VMEM((SUBLANES, cw), F32)],
        compiler_params=_params("parallel", "parallel", "arbitrary"),
        name="rglru",
    )(proj, proj, wcat, conv_w, conv_b.reshape(1, -1), ba.reshape(1, -1), bx.reshape(1, -1),
      lam.reshape(1, -1))


def _t5_bucket(rel, n_buckets):
    max_exact = n_buckets // 2
    relf = jnp.maximum(rel, 1).astype(F32)
    large = max_exact + (jnp.log(relf / max_exact) / math.log(MAX_DISTANCE / max_exact)
                         * (n_buckets - max_exact)).astype(jnp.int32)
    large = jnp.minimum(large, n_buckets - 1)
    return jnp.where(rel < max_exact, rel, large)


def _band_bias_kernel(rb_ref, sink_ref, bucket_ref, o_ref, *, pairs, n_buckets):
    kv = pl.program_id(0)
    blk = ATTN_BLOCK
    bucket = bucket_ref[...]
    qi = lax.broadcasted_iota(jnp.int32, (blk, 2 * blk), 0)
    ki = lax.broadcasted_iota(jnp.int32, (blk, 2 * blk), 1)
    rel = qi + blk - ki
    valid = (rel >= 0) & (rel < blk)
    valid_first = valid & (ki >= blk)
    for p in range(pairs):
        for par in range(2):
            h = (kv * pairs + p) * 2 + par
            bias = jnp.zeros((blk, 2 * blk), F32)
            for b in range(n_buckets):
                bias = jnp.where(bucket == b, rb_ref[b, h], bias)
            sink = sink_ref[h]
            rows, cols = slice(p * blk, (p + 1) * blk), slice(par * 2 * blk, (par + 1) * 2 * blk)
            o_ref[0, rows, cols] = jnp.where(ki == 0, sink, jnp.where(valid_first, bias, NEG_INF))
            o_ref[1, rows, cols] = jnp.where(ki == 0, sink, jnp.where(valid, bias, NEG_INF))


def _band_bias(rel_bias, sinks, n_kv):
    n_buckets, n_heads = rel_bias.shape
    pairs = n_heads // n_kv // 2
    blk = ATTN_BLOCK
    qi = jnp.arange(blk)[:, None]
    ki = jnp.arange(2 * blk)[None, :]
    bucket = _t5_bucket(jnp.maximum(qi + blk - ki, 0), n_buckets)
    smem = pl.BlockSpec(memory_space=pltpu.SMEM)
    return pl.pallas_call(
        functools.partial(_band_bias_kernel, pairs=pairs, n_buckets=n_buckets),
        grid=(n_kv,),
        in_specs=[smem, smem, pl.BlockSpec((blk, 2 * blk), lambda kv: (0, 0))],
        out_specs=pl.BlockSpec((2, None, pairs * blk, 4 * blk), lambda kv: (0, kv, 0, 0)),
        out_shape=jax.ShapeDtypeStruct((2, n_kv, pairs * blk, 4 * blk), F32),
        compiler_params=_params("parallel"),
        name="band_bias",
    )(rel_bias, sinks, bucket)


def _swap_lane_halves(v):
    u = pltpu.bitcast(v, jnp.uint32)
    return pltpu.bitcast(pltpu.roll(u, LANES // 2, axis=1), BF16)


def _attn_kernel(q_ref, kp_ref, kc_ref, vp_ref, vc_ref, bm_ref, o_ref, *, n_kv, pairs, scale):
    blk = ATTN_BLOCK
    fold = math.log2(scale).is_integer()
    lane = lax.broadcasted_iota(jnp.int32, (2 * blk, LANES), 1)
    key = lax.broadcasted_iota(jnp.int32, (2 * blk, LANES), 0)
    low = jnp.where(lane < LANES // 2, 1.0, 0.0).astype(BF16)
    high = jnp.where(lane < LANES // 2, 0.0, 1.0).astype(BF16)
    not_key0 = jnp.where(key == 0, 0.0, 1.0).astype(BF16)
    ones_blk = jnp.concatenate([low, high], axis=0)

    def block_diag(prev_ref, cur_ref, kv):
        sl = slice((kv // 2) * LANES, (kv // 2 + 1) * LANES)
        t = jnp.concatenate([prev_ref[:, sl], cur_ref[:, sl]], axis=0)
        t = t * not_key0
        s = _swap_lane_halves(t)
        lo, hi = (t, s) if kv % 2 == 0 else (s, t)
        return jnp.concatenate([lo * low, hi * high], axis=0)

    for kv in range(n_kv):
        kblk = block_diag(kp_ref, kc_ref, kv)
        vaug = jnp.concatenate([block_diag(vp_ref, vc_ref, kv), ones_blk], axis=1)
        q = jnp.concatenate([q_ref[:, (kv * pairs + p) * LANES:(kv * pairs + p + 1) * LANES]
                             for p in range(pairs)], axis=0)
        if fold:
            q = q * scale
        s = lax.dot_general(q, kblk, (((1,), (1,)), ((), ())), preferred_element_type=F32)
        if not fold:
            s = s * scale
        s = s + bm_ref[kv]
        es = []
        for par in range(2):
            sp = s[:, par * 2 * blk:(par + 1) * 2 * blk]
            es.append(jnp.exp(sp - jnp.max(sp, axis=-1, keepdims=True)).astype(BF16))
        o = jnp.dot(jnp.concatenate(es, axis=1), vaug, preferred_element_type=F32)
        res = (o[:, :LANES] / o[:, LANES:]).astype(o_ref.dtype)
        for p in range(pairs):
            o_ref[:, (kv * pairs + p) * LANES:(kv * pairs + p + 1) * LANES] = res[p * blk:(p + 1) * blk]


def _attention(proj, band_bias, bsz, seq, q_off, k_off, v_off, attn_w, kv_w, n_heads):
    m = proj.shape[0]
    hd = attn_w // n_heads
    n_kv = kv_w // hd
    pairs = n_heads // n_kv // 2
    blk = ATTN_BLOCK
    nblk = seq // blk
    assert 2 * hd == LANES and n_heads == 2 * pairs * n_kv and kv_w % LANES == 0
    assert q_off % attn_w == 0 and k_off % kv_w == 0 and v_off % kv_w == 0
    qb, kb, vb = q_off // attn_w, k_off // kv_w, v_off // kv_w
    cur = lambda col: (lambda b, n: (b * nblk + n, col))
    prev = lambda col: (lambda b, n: (b * nblk + jnp.maximum(n - 1, 0), col))
    kernel = functools.partial(_attn_kernel, n_kv=n_kv, pairs=pairs, scale=hd ** -0.5)
    return pl.pallas_call(
        kernel,
        grid=(bsz, nblk),
        in_specs=[pl.BlockSpec((blk, attn_w), cur(qb)),
                  pl.BlockSpec((blk, kv_w), prev(kb)),
                  pl.BlockSpec((blk, kv_w), cur(kb)),
                  pl.BlockSpec((blk, kv_w), prev(vb)),
                  pl.BlockSpec((blk, kv_w), cur(vb)),
                  pl.BlockSpec((None, n_kv, pairs * blk, 4 * blk),
                               lambda b, n: (jnp.minimum(n, 1), 0, 0, 0))],
        out_specs=pl.BlockSpec((blk, attn_w), lambda b, n: (b * nblk + n, 0)),
        out_shape=jax.ShapeDtypeStruct((m, attn_w), BF16),
        compiler_params=_params("parallel", "arbitrary"),
        name="swa_attn",
    )(proj, proj, proj, proj, proj, band_bias)


def _merge_kernel(rec_ref, att_ref, ga_ref, gb_ref, wl_ref, wa_ref, o_ref):
    ya = jnp.dot(rec_ref[...], wl_ref[...], preferred_element_type=F32)
    yb = jnp.dot(att_ref[...], wa_ref[...], preferred_element_type=F32)
    ga = _sigmoid(ga_ref[...].astype(F32))
    gb = _sigmoid(gb_ref[...].astype(F32))
    o_ref[...] = (ga * ya + gb * yb).astype(o_ref.dtype)


def _merge(rec, att, proj, w_lru, w_att, ga_off, gb_off):
    m, kw = rec.shape
    d = w_lru.shape[1]
    tm = _tile(m, 1024, SUBLANES)
    tn = _tile(math.gcd(math.gcd(ga_off, gb_off), d), 512, LANES)
    return pl.pallas_call(
        _merge_kernel,
        grid=(m // tm, d // tn),
        in_specs=[pl.BlockSpec((tm, kw), lambda i, j: (i, 0)),
                  pl.BlockSpec((tm, att.shape[1]), lambda i, j: (i, 0)),
                  pl.BlockSpec((tm, tn), lambda i, j: (i, ga_off // tn + j)),
                  pl.BlockSpec((tm, tn), lambda i, j: (i, gb_off // tn + j)),
                  pl.BlockSpec((kw, tn), lambda i, j: (0, j)),
                  pl.BlockSpec((att.shape[1], tn), lambda i, j: (0, j))],
        out_specs=pl.BlockSpec((tm, tn), lambda i, j: (i, j)),
        out_shape=jax.ShapeDtypeStruct((m, d), BF16),
        compiler_params=_params("parallel", "arbitrary"),
        name="merge",
    )(rec, att, proj, proj, w_lru, w_att)


def _out_proj_kernel(a_ref, w_ref, x_ref, gate_ref, o_ref):
    y = jnp.dot(a_ref[...], w_ref[...], preferred_element_type=F32)
    o_ref[...] = x_ref[...] + gate_ref[...] * y


def _out_proj(merged, w, x2d, gate, seq):
    m, kw = merged.shape
    d = w.shape[1]
    tm = _tile(seq, 1024, SUBLANES)
    tn = _tile(d, 1024, LANES)
    per_b = seq // tm
    return pl.pallas_call(
        _out_proj_kernel,
        grid=(m // tm, d // tn),
        in_specs=[pl.BlockSpec((tm, kw), lambda i, j: (i, 0)),
                  pl.BlockSpec((kw, tn), lambda i, j: (0, j)),
                  pl.BlockSpec((tm, tn), lambda i, j: (i, j)),
                  pl.BlockSpec((None, 1, tn), lambda i, j: (i // per_b, 0, j))],
        out_specs=pl.BlockSpec((tm, tn), lambda i, j: (i, j)),
        out_shape=jax.ShapeDtypeStruct((m, d), F32),
        compiler_params=_params("parallel", "arbitrary"),
        name="out_proj",
    )(merged, w, x2d, gate)


def _mlp_kernel(x_ref, sh_ref, sc_ref, gate_ref, g2_ref, gf_ref, w1_ref, w2_ref, o_ref, h_ref):
    f = pl.program_id(1)

    @pl.when(f == 0)
    def _():
        h_ref[...] = _rms_mod(x_ref[...], g2_ref[...], sh_ref[...], sc_ref[...]).astype(BF16)

    ff = jnp.dot(h_ref[...], w1_ref[...], preferred_element_type=F32)
    ff = jnp.square(jnp.maximum(ff, 0.0)).astype(BF16)
    part = jnp.dot(ff, w2_ref[...], preferred_element_type=F32)

    @pl.when(f == 0)
    def _():
        o_ref[...] = part

    @pl.when(f > 0)
    def _():
        o_ref[...] += part

    @pl.when(f == pl.num_programs(1) - 1)
    def _():
        x2 = x_ref[...] + gate_ref[...] * o_ref[...]
        var = jnp.mean(x2 * x2, axis=-1, keepdims=True)
        o_ref[...] = x2 * lax.rsqrt(var + EPS) * gf_ref[...]


def _mlp(x1, shift, scale, gate, g2, gf, w1, w2, seq):
    m, d = x1.shape
    dff = w1.shape[1]
    tm = _tile(seq, 512, SUBLANES)
    tf = _tile(dff, 1024, LANES)
    per_b = seq // tm
    row = lambda i, f: (i // per_b, 0, 0)
    const = lambda i, f: (0, 0)
    return pl.pallas_call(
        _mlp_kernel,
        grid=(m // tm, dff // tf),
        in_specs=[pl.BlockSpec((tm, d), lambda i, f: (i, 0)),
                  pl.BlockSpec((None, 1, d), row),
                  pl.BlockSpec((None, 1, d), row),
                  pl.BlockSpec((None, 1, d), row),
                  pl.BlockSpec((1, d), const),
                  pl.BlockSpec((1, d), const),
                  pl.BlockSpec((d, tf), lambda i, f: (0, f)),
                  pl.BlockSpec((tf, d), lambda i, f: (f, 0))],
        out_specs=pl.BlockSpec((tm, d), lambda i, f: (i, 0)),
        out_shape=jax.ShapeDtypeStruct((m, d), F32),
        scratch_shapes=[pltpu.VMEM((tm, d), BF16)],
        compiler_params=_params("parallel", "arbitrary"),
        name="mlp",
    )(x1, shift, scale, gate, g2, gf, w1, w2)


def kernel(x, c, w_ada, b_ada, norm1_g, w_in, conv_w, conv_b, lru_wa, lru_ba, lru_wx, lru_bx,
           lru_lambda, w_lru_out, w_attn_out, attn_sinks, rel_bias, w_out, norm2_g, w_ff1, w_ff2,
           final_g):
    bsz, seq, d = x.shape
    depth = w_in.shape[0]
    lru_w = w_lru_out.shape[1]
    attn_w = w_attn_out.shape[1]
    kv_w = (w_in.shape[2] - 2 * lru_w - attn_w - 2 * d) // 2
    q_off = 2 * lru_w
    k_off = q_off + attn_w
    v_off = k_off + kv_w
    ga_off = v_off + kv_w
    gb_off = ga_off + d

    n_heads = attn_sinks.shape[1]
    n_kv = kv_w // (attn_w // n_heads)
    xs = x.reshape(bsz * seq, d)
    for l in range(depth):
        band_bias = _band_bias(rel_bias, attn_sinks[l], n_kv)
        mod = _adaln(c, w_ada[l], b_ada[l])
        shift1, scale1, gate1, shift2, scale2, gate2 = [
            t.reshape(bsz, 1, d) for t in jnp.split(mod, 6, axis=-1)]

        proj = _in_proj(xs, shift1, scale1, norm1_g[l].reshape(1, d), w_in[l].astype(BF16), seq)
        wcat = jnp.concatenate([lru_wa[l], lru_wx[l]], axis=-1).astype(BF16)
        rec = _rglru(proj, wcat, conv_w[l], conv_b[l], lru_ba[l], lru_bx[l], lru_lambda[l],
                     bsz, seq, lru_w)
        att = _attention(proj, band_bias, bsz, seq, q_off, k_off, v_off, attn_w, kv_w, n_heads)
        merged = _merge(rec, att, proj, w_lru_out[l].astype(BF16), w_attn_out[l].astype(BF16),
                        ga_off, gb_off)
        x1 = _out_proj(merged, w_out[l].astype(BF16), xs, gate1, seq)
        last = l == depth - 1
        if not last:
            raise NotImplementedError("the fused final norm assumes a single layer")
        xs = _mlp(x1, shift2, scale2, gate2, norm2_g[l].reshape(1, d), final_g.reshape(1, d),
                  w_ff1[l].astype(BF16), w_ff2[l].astype(BF16), seq)
    return xs.reshape(bsz, seq, d)
```

```python
import functools
import math

import jax
import jax.numpy as jnp
from jax import lax
from jax.experimental import pallas as pl
from jax.experimental.pallas import tpu as pltpu

EPS = 1e-6
LRU_C = 8.0
LOG2E = 1.4426950408889634
ATTN_BLOCK = 128
NEG_INF = -1e30
MAX_DISTANCE = 128
LANES = 128
SUBLANES = 8
VMEM_LIMIT = 56 * 1024 * 1024

F32 = jnp.float32
BF16 = jnp.bfloat16


def _tile(n, target, align):
    best = None
    t = align
    while t <= min(n, target):
        if n % t == 0:
            best = t
        t += align
    if best is None:
        raise ValueError(f"no tile for {n} (target {target}, align {align})")
    return best


def _params(*sem):
    return pltpu.CompilerParams(dimension_semantics=sem, vmem_limit_bytes=VMEM_LIMIT)


def _sigmoid(v):
    return 1.0 / (1.0 + jnp.exp2(v * -LOG2E))


def _adaln_kernel(c_ref, w_ref, b_ref, o_ref):
    c = c_ref[...]
    act = (c * _sigmoid(c)).astype(BF16)
    o_ref[...] = jnp.dot(act, w_ref[...].astype(BF16), preferred_element_type=F32) + b_ref[...]


def _adaln(c, w, b):
    bsz, d = c.shape
    n = w.shape[1]
    tn = _tile(n, 1024, LANES)
    return pl.pallas_call(
        _adaln_kernel,
        grid=(n // tn,),
        in_specs=[pl.BlockSpec((bsz, d), lambda j: (0, 0)),
                  pl.BlockSpec((d, tn), lambda j: (0, j)),
                  pl.BlockSpec((1, tn), lambda j: (0, j))],
        out_specs=pl.BlockSpec((bsz, tn), lambda j: (0, j)),
        out_shape=jax.ShapeDtypeStruct((bsz, n), F32),
        compiler_params=_params("parallel"),
        name="adaln_mod",
    )(c, w, b.reshape(1, n))


def _rms_mod(x, g, shift, scale):
    var = jnp.mean(x * x, axis=-1, keepdims=True)
    y = x * lax.rsqrt(var + EPS) * g
    return y * (1.0 + scale) + shift


ROW_CHUNKS = 4


def _in_proj_kernel(x_ref, sh_ref, sc_ref, g_ref, w_ref, o_ref, h_ref):
    j = pl.program_id(1)
    rc = x_ref.shape[0] // ROW_CHUNKS

    @pl.when(j == 0)
    def _():
        for r in range(ROW_CHUNKS):
            rows = pl.ds(r * rc, rc)
            h = _rms_mod(x_ref[rows, :], g_ref[...], sh_ref[...], sc_ref[...]).astype(BF16)
            h_ref[rows, :] = h
            o_ref[rows, :] = jnp.dot(h, w_ref[...], preferred_element_type=F32).astype(o_ref.dtype)

    @pl.when(j > 0)
    def _():
        o_ref[...] = jnp.dot(h_ref[...], w_ref[...],
                             preferred_element_type=F32).astype(o_ref.dtype)


def _in_proj(x2d, shift, scale, g, w, seq):
    m, d = x2d.shape
    n = w.shape[1]
    tm = _tile(seq, 1024, SUBLANES)
    tn = _tile(n, 1536, LANES)
    per_b = seq // tm
    row = lambda i, j: (i // per_b, 0, 0)
    return pl.pallas_call(
        _in_proj_kernel,
        grid=(m // tm, n // tn),
        in_specs=[pl.BlockSpec((tm, d), lambda i, j: (i, 0)),
                  pl.BlockSpec((None, 1, d), row),
                  pl.BlockSpec((None, 1, d), row),
                  pl.BlockSpec((1, d), lambda i, j: (0, 0)),
                  pl.BlockSpec((d, tn), lambda i, j: (0, j))],
        out_specs=pl.BlockSpec((tm, tn), lambda i, j: (i, j)),
        out_shape=jax.ShapeDtypeStruct((m, n), BF16),
        scratch_shapes=[pltpu.VMEM((tm, d), BF16)],
        compiler_params=_params("parallel", "arbitrary"),
        name="in_proj",
    )(x2d, shift, scale, g, w)


def _gelu_tanh(v):
    k1 = -2.0 * math.sqrt(2.0 / math.pi) * LOG2E
    return v / (1.0 + jnp.exp2(v * (k1 + (k1 * 0.044715) * (v * v))))


def _rglru_kernel(x_ref, gate_ref, w_ref, cw_ref, cb_ref, ba_ref, bx_ref, lam_ref, o_ref,
                  xtail, hcar, *, conv_k):
    t = pl.program_id(2)
    tt, cw = x_ref.shape
    groups = tt // SUBLANES

    @pl.when(t == 0)
    def _():
        xtail[...] = jnp.zeros_like(xtail)
        hcar[...] = jnp.zeros_like(hcar)

    assert conv_k - 1 < SUBLANES
    x = x_ref[...].astype(F32)
    xs = jnp.concatenate([xtail[...], x], axis=0).reshape(groups + 1, SUBLANES, cw)
    xtail[...] = x[tt - SUBLANES:, :]
    first = lax.broadcasted_iota(jnp.int32, (groups + 1, SUBLANES, cw), 1) == 0
    xc = cb_ref[...] + cw_ref[conv_k - 1:conv_k, :] * x
    for k in range(conv_k - 2, -1, -1):
        rot = pltpu.roll(xs, 1, axis=1)
        xs = jnp.where(first, jnp.concatenate([rot[:1], rot[:-1]], axis=0), rot)
        xc = xc + cw_ref[k:k + 1, :] * xs[1:].reshape(tt, cw)

    lam = lam_ref[...]
    softplus_neg_lam = jnp.maximum(-lam, 0.0) + jnp.log(1.0 + jnp.exp(-jnp.abs(lam)))
    log2_a_coef = (-LRU_C * LOG2E) * softplus_neg_lam
    row0 = lax.broadcasted_iota(jnp.int32, (groups, SUBLANES, LANES), 1) == 0
    seq_start = (lax.broadcasted_iota(jnp.int32, (SUBLANES, LANES), 0) == 0) & (t == 0)

    for c in range(cw // LANES):
        sl = slice(c * LANES, (c + 1) * LANES)
        xh = xc[:, sl]
        z = jnp.dot(xh.astype(BF16), w_ref[c], preferred_element_type=F32)
        r = _sigmoid(z[:, :LANES] + ba_ref[:, sl])
        i = _sigmoid(z[:, LANES:] + bx_ref[:, sl])
        a = jnp.exp2(log2_a_coef[:, sl] * r)
        om = 1.0 - a * a
        mult = jnp.where(om > 0.0, om * lax.rsqrt(om), 0.0)
        mult = jnp.concatenate([jnp.where(seq_start, 1.0, mult[:SUBLANES]), mult[SUBLANES:]],
                               axis=0)
        u3 = (mult * (i * xh)).reshape(groups, SUBLANES, LANES)
        a3 = a.reshape(groups, SUBLANES, LANES)

        az = jnp.where(row0, 0.0, a3)
        a_in = jnp.where(row0, a3, 0.0)
        az2 = az * pltpu.roll(az, 1, axis=1)
        az4 = az2 * pltpu.roll(az2, 2, axis=1)
        h_prev = hcar[0:1, sl]
        hs = []
        for gi in range(groups):
            hg = u3[gi] + a_in[gi] * h_prev
            hg = hg + az[gi] * pltpu.roll(hg, 1, axis=0)
            hg = hg + az2[gi] * pltpu.roll(hg, 2, axis=0)
            hg = hg + az4[gi] * pltpu.roll(hg, 4, axis=0)
            hs.append(hg)
            h_prev = hg[SUBLANES - 1:SUBLANES, :]
        hcar[0:1, sl] = h_prev
        h = jnp.concatenate(hs, axis=0)
        gate = gate_ref[:, sl].astype(F32)
        o_ref[:, sl] = (h * _gelu_tanh(gate)).astype(o_ref.dtype)


def _rglru(proj, wcat, conv_w, conv_b, ba, bx, lam, bsz, seq, lru_w):
    m = proj.shape[0]
    conv_k = conv_w.shape[0]
    tt = _tile(seq, 256, SUBLANES)
    cw = _tile(lru_w, 1024, LANES)
    per_b = seq // tt
    gate_off = lru_w // cw
    vec = pl.BlockSpec((1, cw), lambda b, c, t: (0, c))
    return pl.pallas_call(
        functools.partial(_rglru_kernel, conv_k=conv_k),
        grid=(bsz, lru_w // cw, per_b),
        in_specs=[pl.BlockSpec((tt, cw), lambda b, c, t: (b * per_b + t, c)),
                  pl.BlockSpec((tt, cw), lambda b, c, t: (b * per_b + t, gate_off + c)),
                  pl.BlockSpec((cw // LANES, LANES, 2 * LANES), lambda b, c, t: (c, 0, 0)),
                  pl.BlockSpec((conv_k, cw), lambda b, c, t: (0, c)),
                  vec, vec, vec, vec],
        out_specs=pl.BlockSpec((tt, cw), lambda b, c, t: (b * per_b + t, c)),
        out_shape=jax.ShapeDtypeStruct((m, lru_w), BF16),
        scratch_shapes=[pltpu.VMEM((SUBLANES, cw), F32), pltpu.VMEM((SUBLANES, cw), F32)],
        compiler_params=_params("parallel", "parallel", "arbitrary"),
        name="rglru",
    )(proj, proj, wcat, conv_w, conv_b.reshape(1, -1), ba.reshape(1, -1), bx.reshape(1, -1),
      lam.reshape(1, -1))


def _t5_bucket(rel, n_buckets):
    max_exact = n_buckets // 2
    relf = jnp.maximum(rel, 1).astype(F32)
    large = max_exact + (jnp.log(relf / max_exact) / math.log(MAX_DISTANCE / max_exact)
                         * (n_buckets - max_exact)).astype(jnp.int32)
    large = jnp.minimum(large, n_buckets - 1)
    return jnp.where(rel < max_exact, rel, large)


def _band_bias_kernel(rb_ref, sink_ref, bucket_ref, o_ref, *, pairs, n_buckets):
    kv = pl.program_id(0)
    blk = ATTN_BLOCK
    bucket = bucket_ref[...]
    qi = lax.broadcasted_iota(jnp.int32, (blk, 2 * blk), 0)
    ki = lax.broadcasted_iota(jnp.int32, (blk, 2 * blk), 1)
    rel = qi + blk - ki
    valid = (rel >= 0) & (rel < blk)
    valid_first = valid & (ki >= blk)
    for p in range(pairs):
        for par in range(2):
            h = (kv * pairs + p) * 2 + par
            bias = jnp.zeros((blk, 2 * blk), F32)
            for b in range(n_buckets):
                bias = jnp.where(bucket == b, rb_ref[b, h], bias)
            sink = sink_ref[h]
            rows, cols = slice(p * blk, (p + 1) * blk), slice(par * 2 * blk, (par + 1) * 2 * blk)
            o_ref[0, rows, cols] = jnp.where(ki == 0, sink, jnp.where(valid_first, bias, NEG_INF))
            o_ref[1, rows, cols] = jnp.where(ki == 0, sink, jnp.where(valid, bias, NEG_INF))


def _band_bias(rel_bias, sinks, n_kv):
    n_buckets, n_heads = rel_bias.shape
    pairs = n_heads // n_kv // 2
    blk = ATTN_BLOCK
    qi = jnp.arange(blk)[:, None]
    ki = jnp.arange(2 * blk)[None, :]
    bucket = _t5_bucket(jnp.maximum(qi + blk - ki, 0), n_buckets)
    smem = pl.BlockSpec(memory_space=pltpu.SMEM)
    return pl.pallas_call(
        functools.partial(_band_bias_kernel, pairs=pairs, n_buckets=n_buckets),
        grid=(n_kv,),
        in_specs=[smem, smem, pl.BlockSpec((blk, 2 * blk), lambda kv: (0, 0))],
        out_specs=pl.BlockSpec((2, None, pairs * blk, 4 * blk), lambda kv: (0, kv, 0, 0)),
        out_shape=jax.ShapeDtypeStruct((2, n_kv, pairs * blk, 4 * blk), F32),
        compiler_params=_params("parallel"),
        name="band_bias",
    )(rel_bias, sinks, bucket)


def _swap_lane_halves(v):
    u = pltpu.bitcast(v, jnp.uint32)
    return pltpu.bitcast(pltpu.roll(u, LANES // 2, axis=1), BF16)


def _attn_kernel(q_ref, kp_ref, kc_ref, vp_ref, vc_ref, bm_ref, o_ref, *, n_kv, pairs, scale):
    blk = ATTN_BLOCK
    fold = math.log2(scale).is_integer()
    lane = lax.broadcasted_iota(jnp.int32, (2 * blk, LANES), 1)
    key = lax.broadcasted_iota(jnp.int32, (2 * blk, LANES), 0)
    low = jnp.where(lane < LANES // 2, 1.0, 0.0).astype(BF16)
    high = jnp.where(lane < LANES // 2, 0.0, 1.0).astype(BF16)
    not_key0 = jnp.where(key == 0, 0.0, 1.0).astype(BF16)
    ones_blk = jnp.concatenate([low, high], axis=0)

    def block_diag(prev_ref, cur_ref, kv):
        sl = slice((kv // 2) * LANES, (kv // 2 + 1) * LANES)
        t = jnp.concatenate([prev_ref[:, sl], cur_ref[:, sl]], axis=0)
        t = t * not_key0
        s = _swap_lane_halves(t)
        lo, hi = (t, s) if kv % 2 == 0 else (s, t)
        return jnp.concatenate([lo * low, hi * high], axis=0)

    for kv in range(n_kv):
        kblk = block_diag(kp_ref, kc_ref, kv)
        vaug = jnp.concatenate([block_diag(vp_ref, vc_ref, kv), ones_blk], axis=1)
        q = jnp.concatenate([q_ref[:, (kv * pairs + p) * LANES:(kv * pairs + p + 1) * LANES]
                             for p in range(pairs)], axis=0)
        if fold:
            q = q * scale
        s = lax.dot_general(q, kblk, (((1,), (1,)), ((), ())), preferred_element_type=F32)
        if not fold:
            s = s * scale
        s = s + bm_ref[kv]
        es = []
        for par in range(2):
            sp = s[:, par * 2 * blk:(par + 1) * 2 * blk]
            es.append(jnp.exp(sp - jnp.max(sp, axis=-1, keepdims=True)).astype(BF16))
        o = jnp.dot(jnp.concatenate(es, axis=1), vaug, preferred_element_type=F32)
        res = (o[:, :LANES] / o[:, LANES:]).astype(o_ref.dtype)
        for p in range(pairs):
            o_ref[:, (kv * pairs + p) * LANES:(kv * pairs + p + 1) * LANES] = res[p * blk:(p + 1) * blk]


def _attention(proj, band_bias, bsz, seq, q_off, k_off, v_off, attn_w, kv_w, n_heads):
    m = proj.shape[0]
    hd = attn_w // n_heads
    n_kv = kv_w // hd
    pairs = n_heads // n_kv // 2
    blk = ATTN_BLOCK
    nblk = seq // blk
    assert 2 * hd == LANES and n_heads == 2 * pairs * n_kv and kv_w % LANES == 0
    assert q_off % attn_w == 0 and k_off % kv_w == 0 and v_off % kv_w == 0
    qb, kb, vb = q_off // attn_w, k_off // kv_w, v_off // kv_w
    cur = lambda col: (lambda b, n: (b * nblk + n, col))
    prev = lambda col: (lambda b, n: (b * nblk + jnp.maximum(n - 1, 0), col))
    kernel = functools.partial(_attn_kernel, n_kv=n_kv, pairs=pairs, scale=hd ** -0.5)
    return pl.pallas_call(
        kernel,
        grid=(bsz, nblk),
        in_specs=[pl.BlockSpec((blk, attn_w), cur(qb)),
                  pl.BlockSpec((blk, kv_w), prev(kb)),
                  pl.BlockSpec((blk, kv_w), cur(kb)),
                  pl.BlockSpec((blk, kv_w), prev(vb)),
                  pl.BlockSpec((blk, kv_w), cur(vb)),
                  pl.BlockSpec((None, n_kv, pairs * blk, 4 * blk),
                               lambda b, n: (jnp.minimum(n, 1), 0, 0, 0))],
        out_specs=pl.BlockSpec((blk, attn_w), lambda b, n: (b * nblk + n, 0)),
        out_shape=jax.ShapeDtypeStruct((m, attn_w), BF16),
        compiler_params=_params("parallel", "arbitrary"),
        name="swa_attn",
    )(proj, proj, proj, proj, proj, band_bias)


COL_CHUNKS = 4


def _mix_out_kernel(rec_ref, att_ref, ga_ref, gb_ref, x_ref, gate_ref, wl_ref, wa_ref, wo_ref,
                    o_ref, m_ref):
    d = o_ref.shape[1]
    cc = d // COL_CHUNKS
    for c in range(COL_CHUNKS):
        cols = slice(c * cc, (c + 1) * cc)
        ya = jnp.dot(rec_ref[...], wl_ref[:, cols], preferred_element_type=F32)
        yb = jnp.dot(att_ref[...], wa_ref[:, cols], preferred_element_type=F32)
        ga = _sigmoid(ga_ref[:, cols].astype(F32))
        gb = _sigmoid(gb_ref[:, cols].astype(F32))
        m_ref[:, cols] = (ga * ya + gb * yb).astype(BF16)
    for c in range(COL_CHUNKS):
        cols = slice(c * cc, (c + 1) * cc)
        y = jnp.dot(m_ref[...], wo_ref[:, cols], preferred_element_type=F32)
        o_ref[:, cols] = x_ref[:, cols] + gate_ref[:, cols] * y


def _mix_out(rec, att, proj, x2d, gate, w_lru, w_att, w_out, ga_off, gb_off, seq):
    m, d = x2d.shape
    assert ga_off % d == 0 and gb_off % d == 0
    tm = _tile(seq, 256, SUBLANES)
    per_b = seq // tm
    resident = lambda shape: pl.BlockSpec(shape, lambda i: (0, 0), pipeline_mode=pl.Buffered(1))
    return pl.pallas_call(
        _mix_out_kernel,
        grid=(m // tm,),
        in_specs=[pl.BlockSpec((tm, rec.shape[1]), lambda i: (i, 0)),
                  pl.BlockSpec((tm, att.shape[1]), lambda i: (i, 0)),
                  pl.BlockSpec((tm, d), lambda i: (i, ga_off // d)),
                  pl.BlockSpec((tm, d), lambda i: (i, gb_off // d)),
                  pl.BlockSpec((tm, d), lambda i: (i, 0)),
                  pl.BlockSpec((None, 1, d), lambda i: (i // per_b, 0, 0)),
                  resident(w_lru.shape), resident(w_att.shape), resident(w_out.shape)],
        out_specs=pl.BlockSpec((tm, d), lambda i: (i, 0)),
        out_shape=jax.ShapeDtypeStruct((m, d), F32),
        scratch_shapes=[pltpu.VMEM((tm, d), BF16)],
        compiler_params=_params("parallel"),
        name="mix_out",
    )(rec, att, proj, proj, x2d, gate, w_lru, w_att, w_out)


def _mlp_kernel(x_ref, sh_ref, sc_ref, gate_ref, g2_ref, gf_ref, w1_ref, w2_ref, o_ref,
                h_ref, ff_ref, *, nf):
    f = pl.program_id(1)
    rc = x_ref.shape[0] // ROW_CHUNKS

    def up(rows, slot):
        ff = jnp.dot(h_ref[rows, :], w1_ref[...], preferred_element_type=F32)
        ff_ref[slot, rows, :] = jnp.square(jnp.maximum(ff, 0.0)).astype(BF16)

    def down(rows, slot):
        return jnp.dot(ff_ref[slot, rows, :], w2_ref[...], preferred_element_type=F32)

    @pl.when(f == 0)
    def _():
        o_ref[...] = jnp.zeros_like(o_ref)
        for r in range(ROW_CHUNKS):
            rows = pl.ds(r * rc, rc)
            h_ref[rows, :] = _rms_mod(x_ref[rows, :], g2_ref[...], sh_ref[...],
                                      sc_ref[...]).astype(BF16)
            up(rows, 0)

    @pl.when((f > 0) & (f < nf))
    def _():
        o_ref[...] += down(slice(None), (f - 1) % 2)
        up(slice(None), f % 2)

    @pl.when(f == nf)
    def _():
        for r in range(ROW_CHUNKS):
            rows = pl.ds(r * rc, rc)
            x2 = x_ref[rows, :] + gate_ref[...] * (o_ref[rows, :] + down(rows, (nf - 1) % 2))
            var = jnp.mean(x2 * x2, axis=-1, keepdims=True)
            o_ref[rows, :] = x2 * lax.rsqrt(var + EPS) * gf_ref[...]


def _mlp(x1, shift, scale, gate, g2, gf, w1, w2, seq):
    m, d = x1.shape
    dff = w1.shape[1]
    tm = _tile(seq, 512, SUBLANES * ROW_CHUNKS)
    tf = _tile(dff, 1024, LANES)
    nf = dff // tf
    per_b = seq // tm
    row = lambda i, f: (i // per_b, 0, 0)
    const = lambda i, f: (0, 0)
    return pl.pallas_call(
        functools.partial(_mlp_kernel, nf=nf),
        grid=(m // tm, nf + 1),
        in_specs=[pl.BlockSpec((tm, d), lambda i, f: (i, 0)),
                  pl.BlockSpec((None, 1, d), row),
                  pl.BlockSpec((None, 1, d), row),
                  pl.BlockSpec((None, 1, d), row),
                  pl.BlockSpec((1, d), const),
                  pl.BlockSpec((1, d), const),
                  pl.BlockSpec((d, tf), lambda i, f: (0, jnp.minimum(f, nf - 1))),
                  pl.BlockSpec((tf, d), lambda i, f: (jnp.maximum(f - 1, 0), 0))],
        out_specs=pl.BlockSpec((tm, d), lambda i, f: (i, 0)),
        out_shape=jax.ShapeDtypeStruct((m, d), F32),
        scratch_shapes=[pltpu.VMEM((tm, d), BF16), pltpu.VMEM((2, tm, tf), BF16)],
        compiler_params=_params("parallel", "arbitrary"),
        name="mlp",
    )(x1, shift, scale, gate, g2, gf, w1, w2)


def kernel(x, c, w_ada, b_ada, norm1_g, w_in, conv_w, conv_b, lru_wa, lru_ba, lru_wx, lru_bx,
           lru_lambda, w_lru_out, w_attn_out, attn_sinks, rel_bias, w_out, norm2_g, w_ff1, w_ff2,
           final_g):
    bsz, seq, d = x.shape
    depth = w_in.shape[0]
    lru_w = w_lru_out.shape[1]
    attn_w = w_attn_out.shape[1]
    kv_w = (w_in.shape[2] - 2 * lru_w - attn_w - 2 * d) // 2
    n_heads = attn_sinks.shape[1]
    n_kv = kv_w // (attn_w // n_heads)
    q_off = 2 * lru_w
    src_k = q_off + attn_w
    src_ga = src_k + 2 * kv_w
    ga_off = q_off + attn_w
    gb_off = ga_off + d
    k_off = gb_off + d
    v_off = k_off + kv_w

    xs = x.reshape(bsz * seq, d)
    for l in range(depth):
        band_bias = _band_bias(rel_bias, attn_sinks[l], n_kv)
        mod = _adaln(c, w_ada[l], b_ada[l])
        shift1, scale1, gate1, shift2, scale2, gate2 = [
            t.reshape(bsz, 1, d) for t in jnp.split(mod, 6, axis=-1)]

        w_in_l = jnp.concatenate(
            [w_in[l, :, :src_k], w_in[l, :, src_ga:], w_in[l, :, src_k:src_ga]],
            axis=1).astype(BF16)
        proj = _in_proj(xs, shift1, scale1, norm1_g[l].reshape(1, d), w_in_l, seq)
        wcat = jnp.concatenate([lru_wa[l], lru_wx[l]], axis=-1).astype(BF16)
        rec = _rglru(proj, wcat, conv_w[l], conv_b[l], lru_ba[l], lru_bx[l], lru_lambda[l],
                     bsz, seq, lru_w)
        att = _attention(proj, band_bias, bsz, seq, q_off, k_off, v_off, attn_w, kv_w, n_heads)
        x1 = _mix_out(rec, att, proj, xs, gate1, w_lru_out[l].astype(BF16),
                      w_attn_out[l].astype(BF16), w_out[l].astype(BF16), ga_off, gb_off, seq)
        if l != depth - 1:
            raise NotImplementedError("the fused final norm assumes a single layer")
        xs = _mlp(x1, shift2, scale2, gate2, norm2_g[l].reshape(1, d), final_g.reshape(1, d),
                  w_ff1[l].astype(BF16), w_ff2[l].astype(BF16), seq)
    return xs.reshape(bsz, seq, d)
```

```python
import functools
import math

import jax
import jax.numpy as jnp
from jax import lax
from jax.experimental import pallas as pl
from jax.experimental.pallas import tpu as pltpu

EPS = 1e-6
LRU_C = 8.0
LOG2E = 1.4426950408889634
ATTN_BLOCK = 128
NEG_INF = -1e30
MAX_DISTANCE = 128
LANES = 128
SUBLANES = 8
VMEM_LIMIT = 56 * 1024 * 1024

F32 = jnp.float32
BF16 = jnp.bfloat16


def _tile(n, target, align):
    best = None
    t = align
    while t <= min(n, target):
        if n % t == 0:
            best = t
        t += align
    if best is None:
        raise ValueError(f"no tile for {n} (target {target}, align {align})")
    return best


def _params(*sem):
    return pltpu.CompilerParams(dimension_semantics=sem, vmem_limit_bytes=VMEM_LIMIT)


def _sigmoid(v):
    return 1.0 / (1.0 + jnp.exp2(v * -LOG2E))


def _adaln_kernel(c_ref, w_ref, b_ref, o_ref):
    c = c_ref[...]
    act = (c * _sigmoid(c)).astype(BF16)
    o_ref[...] = jnp.dot(act, w_ref[...].astype(BF16), preferred_element_type=F32) + b_ref[...]


def _adaln(c, w, b):
    bsz, d = c.shape
    n = w.shape[1]
    tn = _tile(n, 1024, LANES)
    return pl.pallas_call(
        _adaln_kernel,
        grid=(n // tn,),
        in_specs=[pl.BlockSpec((bsz, d), lambda j: (0, 0)),
                  pl.BlockSpec((d, tn), lambda j: (0, j)),
                  pl.BlockSpec((1, tn), lambda j: (0, j))],
        out_specs=pl.BlockSpec((bsz, tn), lambda j: (0, j)),
        out_shape=jax.ShapeDtypeStruct((bsz, n), F32),
        compiler_params=_params("parallel"),
        name="adaln_mod",
    )(c, w, b.reshape(1, n))


def _rms_mod(x, g, shift, scale):
    var = jnp.mean(x * x, axis=-1, keepdims=True)
    y = x * lax.rsqrt(var + EPS) * g
    return y * (1.0 + scale) + shift


ROW_CHUNKS = 4


def _in_proj_kernel(x_ref, sh_ref, sc_ref, g_ref, w_ref, o_ref, h_ref):
    j = pl.program_id(1)
    rc = x_ref.shape[0] // ROW_CHUNKS

    @pl.when(j == 0)
    def _():
        for r in range(ROW_CHUNKS):
            rows = pl.ds(r * rc, rc)
            h = _rms_mod(x_ref[rows, :], g_ref[...], sh_ref[...], sc_ref[...]).astype(BF16)
            h_ref[rows, :] = h
            o_ref[rows, :] = jnp.dot(h, w_ref[...], preferred_element_type=F32).astype(o_ref.dtype)

    @pl.when(j > 0)
    def _():
        o_ref[...] = jnp.dot(h_ref[...], w_ref[...],
                             preferred_element_type=F32).astype(o_ref.dtype)


def _in_proj(x2d, shift, scale, g, w, seq):
    m, d = x2d.shape
    n = w.shape[1]
    tm = _tile(seq, 1024, SUBLANES)
    tn = _tile(n, 1536, LANES)
    per_b = seq // tm
    row = lambda i, j: (i // per_b, 0, 0)
    return pl.pallas_call(
        _in_proj_kernel,
        grid=(m // tm, n // tn),
        in_specs=[pl.BlockSpec((tm, d), lambda i, j: (i, 0)),
                  pl.BlockSpec((None, 1, d), row),
                  pl.BlockSpec((None, 1, d), row),
                  pl.BlockSpec((1, d), lambda i, j: (0, 0)),
                  pl.BlockSpec((d, tn), lambda i, j: (0, j))],
        out_specs=pl.BlockSpec((tm, tn), lambda i, j: (i, j)),
        out_shape=jax.ShapeDtypeStruct((m, n), BF16),
        scratch_shapes=[pltpu.VMEM((tm, d), BF16)],
        compiler_params=_params("parallel", "arbitrary"),
        name="in_proj",
    )(x2d, shift, scale, g, w)


def _gelu_tanh(v):
    k1 = -2.0 * math.sqrt(2.0 / math.pi) * LOG2E
    return v / (1.0 + jnp.exp2(v * (k1 + (k1 * 0.044715) * (v * v))))


def _rglru_kernel(x_ref, gate_ref, w_ref, cw_ref, cb_ref, ba_ref, bx_ref, lam_ref, o_ref,
                  xtail, hcar, *, conv_k):
    t = pl.program_id(2)
    tt, cw = x_ref.shape
    groups = tt // SUBLANES

    @pl.when(t == 0)
    def _():
        xtail[...] = jnp.zeros_like(xtail)
        hcar[...] = jnp.zeros_like(hcar)

    assert conv_k - 1 < SUBLANES
    x = x_ref[...].astype(F32)
    xs = jnp.concatenate([xtail[...], x], axis=0).reshape(groups + 1, SUBLANES, cw)
    xtail[...] = x[tt - SUBLANES:, :]
    first = lax.broadcasted_iota(jnp.int32, (groups + 1, SUBLANES, cw), 1) == 0
    xc = cb_ref[...] + cw_ref[conv_k - 1:conv_k, :] * x
    for k in range(conv_k - 2, -1, -1):
        rot = pltpu.roll(xs, 1, axis=1)
        xs = jnp.where(first, jnp.concatenate([rot[:1], rot[:-1]], axis=0), rot)
        xc = xc + cw_ref[k:k + 1, :] * xs[1:].reshape(tt, cw)

    lam = lam_ref[...]
    softplus_neg_lam = jnp.maximum(-lam, 0.0) + jnp.log(1.0 + jnp.exp(-jnp.abs(lam)))
    log2_a_coef = (-LRU_C * LOG2E) * softplus_neg_lam
    row0 = lax.broadcasted_iota(jnp.int32, (groups, SUBLANES, LANES), 1) == 0
    seq_start = (lax.broadcasted_iota(jnp.int32, (SUBLANES, LANES), 0) == 0) & (t == 0)

    for c in range(cw // LANES):
        sl = slice(c * LANES, (c + 1) * LANES)
        xh = xc[:, sl]
        z = jnp.dot(xh.astype(BF16), w_ref[c], preferred_element_type=F32)
        r = _sigmoid(z[:, :LANES] + ba_ref[:, sl])
        i = _sigmoid(z[:, LANES:] + bx_ref[:, sl])
        a = jnp.exp2(log2_a_coef[:, sl] * r)
        om = 1.0 - a * a
        mult = jnp.where(om > 0.0, om * lax.rsqrt(om), 0.0)
        mult = jnp.concatenate([jnp.where(seq_start, 1.0, mult[:SUBLANES]), mult[SUBLANES:]],
                               axis=0)
        u3 = (mult * (i * xh)).reshape(groups, SUBLANES, LANES)
        a3 = a.reshape(groups, SUBLANES, LANES)

        az = jnp.where(row0, 0.0, a3)
        a_in = jnp.where(row0, a3, 0.0)
        az2 = az * pltpu.roll(az, 1, axis=1)
        az4 = az2 * pltpu.roll(az2, 2, axis=1)
        h_prev = hcar[0:1, sl]
        hs = []
        for gi in range(groups):
            hg = u3[gi] + a_in[gi] * h_prev
            hg = hg + az[gi] * pltpu.roll(hg, 1, axis=0)
            hg = hg + az2[gi] * pltpu.roll(hg, 2, axis=0)
            hg = hg + az4[gi] * pltpu.roll(hg, 4, axis=0)
            hs.append(hg)
            h_prev = hg[SUBLANES - 1:SUBLANES, :]
        hcar[0:1, sl] = h_prev
        h = jnp.concatenate(hs, axis=0)
        gate = gate_ref[:, sl].astype(F32)
        o_ref[:, sl] = (h * _gelu_tanh(gate)).astype(o_ref.dtype)


def _rglru(proj, wcat, conv_w, conv_b, ba, bx, lam, bsz, seq, lru_w):
    m = proj.shape[0]
    conv_k = conv_w.shape[0]
    tt = _tile(seq, 256, SUBLANES)
    cw = _tile(lru_w, 1024, LANES)
    per_b = seq // tt
    gate_off = lru_w // cw
    vec = pl.BlockSpec((1, cw), lambda b, c, t: (0, c))
    return pl.pallas_call(
        functools.partial(_rglru_kernel, conv_k=conv_k),
        grid=(bsz, lru_w // cw, per_b),
        in_specs=[pl.BlockSpec((tt, cw), lambda b, c, t: (b * per_b + t, c)),
                  pl.BlockSpec((tt, cw), lambda b, c, t: (b * per_b + t, gate_off + c)),
                  pl.BlockSpec((cw // LANES, LANES, 2 * LANES), lambda b, c, t: (c, 0, 0)),
                  pl.BlockSpec((conv_k, cw), lambda b, c, t: (0, c)),
                  vec, vec, vec, vec],
        out_specs=pl.BlockSpec((tt, cw), lambda b, c, t: (b * per_b + t, c)),
        out_shape=jax.ShapeDtypeStruct((m, lru_w), BF16),
        scratch_shapes=[pltpu.VMEM((SUBLANES, cw), F32), pltpu.VMEM((SUBLANES, cw), F32)],
        compiler_params=_params("parallel", "parallel", "arbitrary"),
        name="rglru",
    )(proj, proj, wcat, conv_w, conv_b.reshape(1, -1), ba.reshape(1, -1), bx.reshape(1, -1),
      lam.reshape(1, -1))


def _t5_bucket(rel, n_buckets):
    max_exact = n_buckets // 2
    relf = jnp.maximum(rel, 1).astype(F32)
    large = max_exact + (jnp.log(relf / max_exact) / math.log(MAX_DISTANCE / max_exact)
                         * (n_buckets - max_exact)).astype(jnp.int32)
    large = jnp.minimum(large, n_buckets - 1)
    return jnp.where(rel < max_exact, rel, large)


def _band_bias_kernel(rb_ref, sink_ref, bucket_ref, o_ref, *, pairs, n_buckets):
    kv = pl.program_id(0)
    blk = ATTN_BLOCK
    bucket = bucket_ref[...]
    qi = lax.broadcasted_iota(jnp.int32, (blk, 2 * blk), 0)
    ki = lax.broadcasted_iota(jnp.int32, (blk, 2 * blk), 1)
    rel = qi + blk - ki
    valid = (rel >= 0) & (rel < blk)
    for p in range(pairs):
        for par in range(2):
            h = (kv * pairs + p) * 2 + par
            bias = jnp.zeros((blk, 2 * blk), F32)
            for b in range(n_buckets):
                bias = jnp.where(bucket == b, rb_ref[b, h], bias)
            sink = sink_ref[h]
            rows, cols = slice(p * blk, (p + 1) * blk), slice(par * 2 * blk, (par + 1) * 2 * blk)
            o_ref[rows, cols] = jnp.where(ki == 0, sink, jnp.where(valid, bias, NEG_INF))


def _band_bias(rel_bias, sinks, n_kv):
    n_buckets, n_heads = rel_bias.shape
    pairs = n_heads // n_kv // 2
    blk = ATTN_BLOCK
    qi = jnp.arange(blk)[:, None]
    ki = jnp.arange(2 * blk)[None, :]
    bucket = _t5_bucket(jnp.maximum(qi + blk - ki, 0), n_buckets)
    smem = pl.BlockSpec(memory_space=pltpu.SMEM)
    return pl.pallas_call(
        functools.partial(_band_bias_kernel, pairs=pairs, n_buckets=n_buckets),
        grid=(n_kv,),
        in_specs=[smem, smem, pl.BlockSpec((blk, 2 * blk), lambda kv: (0, 0))],
        out_specs=pl.BlockSpec((None, pairs * blk, 4 * blk), lambda kv: (kv, 0, 0)),
        out_shape=jax.ShapeDtypeStruct((n_kv, pairs * blk, 4 * blk), F32),
        compiler_params=_params("parallel"),
        name="band_bias",
    )(rel_bias, sinks, bucket)


def _swap_lane_halves(v):
    u = pltpu.bitcast(v, jnp.uint32)
    return pltpu.bitcast(pltpu.roll(u, LANES // 2, axis=1), BF16)


COL_CHUNKS = 4
QUERY_BLOCKS = 2


def _mix_kernel(q_ref, kp_ref, kc_ref, vp_ref, vc_ref, bm_ref, rec_ref, ga_ref, gb_ref, x_ref,
                gate_ref, wl_ref, wa_ref, wo_ref, o_ref, att_ref, ya_ref, m_ref, *,
                n_kv, pairs, scale, per_b):
    blk = ATTN_BLOCK
    d = o_ref.shape[1]
    fold = math.log2(scale).is_integer()
    lane = lax.broadcasted_iota(jnp.int32, (blk, LANES), 1)
    key = lax.broadcasted_iota(jnp.int32, (blk, LANES), 0)
    low = jnp.where(lane < LANES // 2, 1.0, 0.0).astype(BF16)
    high = jnp.where(lane < LANES // 2, 0.0, 1.0).astype(BF16)
    not_key0 = jnp.where(key == 0, 0.0, 1.0).astype(BF16)
    ones_blk = jnp.concatenate([low, low, high, high], axis=0)
    key_col = lax.broadcasted_iota(jnp.int32, (1, 4 * blk), 1) % (2 * blk)
    seq_start = pl.program_id(0) % per_b == 0
    start_mask = jnp.where(seq_start & (key_col >= 1) & (key_col < blk), NEG_INF, 0.0)

    def halves(t, kv):
        s = _swap_lane_halves(t)
        lo, hi = (t, s) if kv % 2 == 0 else (s, t)
        return lo * low, hi * high

    def window(prev, cur):
        return jnp.concatenate([prev[0] * not_key0, cur[0], prev[1] * not_key0, cur[1]], axis=0)

    cc = d // n_kv
    for kv in range(n_kv):
        sl = slice((kv // 2) * LANES, (kv // 2 + 1) * LANES)
        kb = [halves(kp_ref[:, sl], kv)] + [halves(kc_ref[t * blk:(t + 1) * blk, sl], kv)
                                            for t in range(QUERY_BLOCKS)]
        vb = [halves(vp_ref[:, sl], kv)] + [halves(vc_ref[t * blk:(t + 1) * blk, sl], kv)
                                            for t in range(QUERY_BLOCKS)]
        for t in range(QUERY_BLOCKS):
            rows = slice(t * blk, (t + 1) * blk)
            kblk = window(kb[t], kb[t + 1])
            vaug = jnp.concatenate([window(vb[t], vb[t + 1]), ones_blk], axis=1)
            q = jnp.concatenate(
                [q_ref[rows, (kv * pairs + p) * LANES:(kv * pairs + p + 1) * LANES]
                 for p in range(pairs)], axis=0)
            if fold:
                q = q * scale
            s = lax.dot_general(q, kblk, (((1,), (1,)), ((), ())), preferred_element_type=F32)
            if not fold:
                s = s * scale
            s = s + bm_ref[kv]
            if t == 0:
                s = s + start_mask
            es = []
            for par in range(2):
                sp = s[:, par * 2 * blk:(par + 1) * 2 * blk]
                es.append(jnp.exp(sp - jnp.max(sp, axis=-1, keepdims=True)).astype(BF16))
            o = jnp.dot(jnp.concatenate(es, axis=1), vaug, preferred_element_type=F32)
            res = (o[:, :LANES] / o[:, LANES:]).astype(BF16)
            for p in range(pairs):
                att_ref[rows, (kv * pairs + p) * LANES:(kv * pairs + p + 1) * LANES] = (
                    res[p * blk:(p + 1) * blk])
        cols = slice(kv * cc, (kv + 1) * cc)
        ya = jnp.dot(rec_ref[...], wl_ref[:, cols], preferred_element_type=F32)
        ya_ref[:, cols] = _sigmoid(ga_ref[:, cols].astype(F32)) * ya

    cc = d // COL_CHUNKS
    for c in range(COL_CHUNKS):
        cols = slice(c * cc, (c + 1) * cc)
        yb = jnp.dot(att_ref[...], wa_ref[:, cols], preferred_element_type=F32)
        gb = _sigmoid(gb_ref[:, cols].astype(F32))
        m_ref[:, cols] = (ya_ref[:, cols] + gb * yb).astype(BF16)
    for c in range(COL_CHUNKS):
        cols = slice(c * cc, (c + 1) * cc)
        y = jnp.dot(m_ref[...], wo_ref[:, cols], preferred_element_type=F32)
        o_ref[:, cols] = x_ref[:, cols] + gate_ref[:, cols] * y


def _mix(proj, rec, x2d, gate, band_bias, w_lru, w_att, w_out, seq, q_off, k_off, v_off, ga_off,
         gb_off, kv_w, n_heads):
    m, d = x2d.shape
    attn_w = w_att.shape[0]
    hd = attn_w // n_heads
    n_kv = kv_w // hd
    pairs = n_heads // n_kv // 2
    blk = ATTN_BLOCK
    tm = QUERY_BLOCKS * blk
    per_b = seq // tm
    assert 2 * hd == LANES and n_heads == 2 * pairs * n_kv and kv_w % LANES == 0
    assert q_off % attn_w == 0 and k_off % kv_w == 0 and v_off % kv_w == 0
    assert ga_off % d == 0 and gb_off % d == 0 and seq % tm == 0 and d % n_kv == 0
    kb, vb = k_off // kv_w, v_off // kv_w
    prev = lambda col: (lambda i: (jnp.maximum(QUERY_BLOCKS * i - 1, 0), col))
    resident = lambda a: pl.BlockSpec(a.shape, lambda i: (0,) * a.ndim,
                                      pipeline_mode=pl.Buffered(1))
    kernel = functools.partial(_mix_kernel, n_kv=n_kv, pairs=pairs, scale=hd ** -0.5, per_b=per_b)
    return pl.pallas_call(
        kernel,
        grid=(m // tm,),
        in_specs=[pl.BlockSpec((tm, attn_w), lambda i: (i, q_off // attn_w)),
                  pl.BlockSpec((blk, kv_w), prev(kb)),
                  pl.BlockSpec((tm, kv_w), lambda i: (i, kb)),
                  pl.BlockSpec((blk, kv_w), prev(vb)),
                  pl.BlockSpec((tm, kv_w), lambda i: (i, vb)),
                  resident(band_bias),
                  pl.BlockSpec((tm, rec.shape[1]), lambda i: (i, 0)),
                  pl.BlockSpec((tm, d), lambda i: (i, ga_off // d)),
                  pl.BlockSpec((tm, d), lambda i: (i, gb_off // d)),
                  pl.BlockSpec((tm, d), lambda i: (i, 0)),
                  pl.BlockSpec((None, 1, d), lambda i: (i // per_b, 0, 0)),
                  resident(w_lru), resident(w_att), resident(w_out)],
        out_specs=pl.BlockSpec((tm, d), lambda i: (i, 0)),
        out_shape=jax.ShapeDtypeStruct((m, d), F32),
        scratch_shapes=[pltpu.VMEM((tm, attn_w), BF16), pltpu.VMEM((tm, d), F32),
                        pltpu.VMEM((tm, d), BF16)],
        compiler_params=_params("arbitrary"),
        name="mix",
    )(proj, proj, proj, proj, proj, band_bias, rec, proj, proj, x2d, gate, w_lru, w_att, w_out)


def _mlp_kernel(x_ref, sh_ref, sc_ref, gate_ref, g2_ref, gf_ref, w1_ref, w2_ref, o_ref,
                h_ref, ff_ref, *, nf):
    f = pl.program_id(1)
    rc = x_ref.shape[0] // ROW_CHUNKS

    def up(rows, slot):
        ff = jnp.dot(h_ref[rows, :], w1_ref[...], preferred_element_type=F32)
        ff_ref[slot, rows, :] = jnp.square(jnp.maximum(ff, 0.0)).astype(BF16)

    def down(rows, slot):
        return jnp.dot(ff_ref[slot, rows, :], w2_ref[...], preferred_element_type=F32)

    @pl.when(f == 0)
    def _():
        o_ref[...] = jnp.zeros_like(o_ref)
        for r in range(ROW_CHUNKS):
            rows = pl.ds(r * rc, rc)
            h_ref[rows, :] = _rms_mod(x_ref[rows, :], g2_ref[...], sh_ref[...],
                                      sc_ref[...]).astype(BF16)
            up(rows, 0)

    @pl.when((f > 0) & (f < nf))
    def _():
        o_ref[...] += down(slice(None), (f - 1) % 2)
        up(slice(None), f % 2)

    @pl.when(f == nf)
    def _():
        for r in range(ROW_CHUNKS):
            rows = pl.ds(r * rc, rc)
            x2 = x_ref[rows, :] + gate_ref[...] * (o_ref[rows, :] + down(rows, (nf - 1) % 2))
            var = jnp.mean(x2 * x2, axis=-1, keepdims=True)
            o_ref[rows, :] = x2 * lax.rsqrt(var + EPS) * gf_ref[...]


def _mlp(x1, shift, scale, gate, g2, gf, w1, w2, seq):
    m, d = x1.shape
    dff = w1.shape[1]
    tm = _tile(seq, 512, SUBLANES * ROW_CHUNKS)
    tf = _tile(dff, 1024, LANES)
    nf = dff // tf
    per_b = seq // tm
    row = lambda i, f: (i // per_b, 0, 0)
    const = lambda i, f: (0, 0)
    return pl.pallas_call(
        functools.partial(_mlp_kernel, nf=nf),
        grid=(m // tm, nf + 1),
        in_specs=[pl.BlockSpec((tm, d), lambda i, f: (i, 0)),
                  pl.BlockSpec((None, 1, d), row),
                  pl.BlockSpec((None, 1, d), row),
                  pl.BlockSpec((None, 1, d), row),
                  pl.BlockSpec((1, d), const),
                  pl.BlockSpec((1, d), const),
                  pl.BlockSpec((d, tf), lambda i, f: (0, jnp.minimum(f, nf - 1))),
                  pl.BlockSpec((tf, d), lambda i, f: ((f + nf - 1) % nf, 0))],
        out_specs=pl.BlockSpec((tm, d), lambda i, f: (i, 0)),
        out_shape=jax.ShapeDtypeStruct((m, d), F32),
        scratch_shapes=[pltpu.VMEM((tm, d), BF16), pltpu.VMEM((2, tm, tf), BF16)],
        compiler_params=_params("parallel", "arbitrary"),
        name="mlp",
    )(x1, shift, scale, gate, g2, gf, w1, w2)


def kernel(x, c, w_ada, b_ada, norm1_g, w_in, conv_w, conv_b, lru_wa, lru_ba, lru_wx, lru_bx,
           lru_lambda, w_lru_out, w_attn_out, attn_sinks, rel_bias, w_out, norm2_g, w_ff1, w_ff2,
           final_g):
    bsz, seq, d = x.shape
    depth = w_in.shape[0]
    lru_w = w_lru_out.shape[1]
    attn_w = w_attn_out.shape[1]
    kv_w = (w_in.shape[2] - 2 * lru_w - attn_w - 2 * d) // 2
    n_heads = attn_sinks.shape[1]
    n_kv = kv_w // (attn_w // n_heads)
    q_off = 2 * lru_w
    src_k = q_off + attn_w
    src_ga = src_k + 2 * kv_w
    ga_off = q_off + attn_w
    gb_off = ga_off + d
    k_off = gb_off + d
    v_off = k_off + kv_w

    xs = x.reshape(bsz * seq, d)
    for l in range(depth):
        band_bias = _band_bias(rel_bias, attn_sinks[l], n_kv)
        mod = _adaln(c, w_ada[l], b_ada[l])
        shift1, scale1, gate1, shift2, scale2, gate2 = [
            t.reshape(bsz, 1, d) for t in jnp.split(mod, 6, axis=-1)]

        w_in_b = w_in[l].astype(BF16)
        w_in_l = jnp.concatenate(
            [w_in_b[:, :src_k], w_in_b[:, src_ga:], w_in_b[:, src_k:src_ga]], axis=1)
        proj = _in_proj(xs, shift1, scale1, norm1_g[l].reshape(1, d), w_in_l, seq)
        wcat = jnp.concatenate([lru_wa[l], lru_wx[l]], axis=-1).astype(BF16)
        rec = _rglru(proj, wcat, conv_w[l], conv_b[l], lru_ba[l], lru_bx[l], lru_lambda[l],
                     bsz, seq, lru_w)
        x1 = _mix(proj, rec, xs, gate1, band_bias, w_lru_out[l].astype(BF16),
                  w_attn_out[l].astype(BF16), w_out[l].astype(BF16), seq, q_off, k_off, v_off,
                  ga_off, gb_off, kv_w, n_heads)
        if l != depth - 1:
            raise NotImplementedError("the fused final norm assumes a single layer")
        xs = _mlp(x1, shift2, scale2, gate2, norm2_g[l].reshape(1, d), final_g.reshape(1, d),
                  w_ff1[l].astype(BF16), w_ff2[l].astype(BF16), seq)
    return xs.reshape(bsz, seq, d)
```

```python
import functools
import math

import jax
import jax.numpy as jnp
from jax import lax
from jax.experimental import pallas as pl
from jax.experimental.pallas import tpu as pltpu

EPS = 1e-6
LRU_C = 8.0
LOG2E = 1.4426950408889634
ATTN_BLOCK = 128
NEG_INF = -1e30
MAX_DISTANCE = 128
LANES = 128
SUBLANES = 8
MXU_WIDTH = 256
VMEM_LIMIT = 56 * 1024 * 1024
ROW_CHUNKS = 4
QUERY_BLOCKS = 2

F32 = jnp.float32
BF16 = jnp.bfloat16


def _tile(n, target, align):
    best = None
    t = align
    while t <= min(n, target):
        if n % t == 0:
            best = t
        t += align
    if best is None:
        raise ValueError(f"no tile for {n} (target {target}, align {align})")
    return best


def _params(*sem):
    return pltpu.CompilerParams(dimension_semantics=sem, vmem_limit_bytes=VMEM_LIMIT)


def _sigmoid(v):
    return 1.0 / (1.0 + jnp.exp2(v * -LOG2E))


def _rms_mod(x, g, shift, scale):
    var = jnp.mean(x * x, axis=-1, keepdims=True)
    y = x * lax.rsqrt(var + EPS) * g
    return y * (1.0 + scale) + shift


def _adaln_kernel(c_ref, w_ref, b_ref, o_ref):
    c = c_ref[...]
    act = (c * _sigmoid(c)).astype(BF16)
    o_ref[...] = jnp.dot(act, w_ref[...].astype(BF16), preferred_element_type=F32) + b_ref[...]


def _adaln(c, w, b):
    bsz, d = c.shape
    n = w.shape[1]
    tn = _tile(n, 1024, LANES)
    return pl.pallas_call(
        _adaln_kernel,
        grid=(n // tn,),
        in_specs=[pl.BlockSpec((bsz, d), lambda j: (0, 0)),
                  pl.BlockSpec((d, tn), lambda j: (0, j)),
                  pl.BlockSpec((1, tn), lambda j: (0, j))],
        out_specs=pl.BlockSpec((bsz, tn), lambda j: (0, j)),
        out_shape=jax.ShapeDtypeStruct((bsz, n), F32),
        compiler_params=_params("parallel"),
        name="adaln_mod",
    )(c, w, b.reshape(1, n))


def _gelu_tanh(v):
    k1 = -2.0 * math.sqrt(2.0 / math.pi) * LOG2E
    return v / (1.0 + jnp.exp2(v * (k1 + (k1 * 0.044715) * (v * v))))


def _rglru_tile(x, gate, seq_start, w_ref, cw_ref, cb_ref, ba_ref, bx_ref, lam_ref, o_ref,
                xtail, hcar, conv_k, other_mxu_work):
    tt, cw = x.shape
    groups = tt // SUBLANES

    assert conv_k - 1 < SUBLANES
    xs = jnp.concatenate([xtail[...], x], axis=0).reshape(groups + 1, SUBLANES, cw)
    xtail[...] = x[tt - SUBLANES:, :]
    first = lax.broadcasted_iota(jnp.int32, (groups + 1, SUBLANES, cw), 1) == 0
    xc = cb_ref[...] + cw_ref[conv_k - 1:conv_k, :] * x
    for k in range(conv_k - 2, -1, -1):
        rot = pltpu.roll(xs, 1, axis=1)
        xs = jnp.where(first, jnp.concatenate([rot[:1], rot[:-1]], axis=0), rot)
        xc = xc + cw_ref[k:k + 1, :] * xs[1:].reshape(tt, cw)

    lam = lam_ref[...]
    softplus_neg_lam = jnp.maximum(-lam, 0.0) + jnp.log(1.0 + jnp.exp(-jnp.abs(lam)))
    log2_a_coef = (-LRU_C * LOG2E) * softplus_neg_lam
    row0 = lax.broadcasted_iota(jnp.int32, (groups, SUBLANES, LANES), 1) == 0
    start_row = (lax.broadcasted_iota(jnp.int32, (SUBLANES, LANES), 0) == 0) & seq_start

    n_chunks = cw // LANES

    def gates(c):
        return jnp.dot(xc[:, c * LANES:(c + 1) * LANES].astype(BF16), w_ref[c],
                       preferred_element_type=F32)

    pieces = iter(other_mxu_work)

    def emit():
        piece = next(pieces, None)
        if piece is not None:
            piece()

    emit()
    zs = [gates(0)]
    for c in range(n_chunks):
        sl = slice(c * LANES, (c + 1) * LANES)
        if c + 1 < n_chunks:
            zs.append(gates(c + 1))
        emit()
        xh = xc[:, sl]
        z = zs[c]
        r = _sigmoid(z[:, :LANES] + ba_ref[:, sl])
        i = _sigmoid(z[:, LANES:] + bx_ref[:, sl])
        a = jnp.exp2(log2_a_coef[:, sl] * r)
        om = 1.0 - a * a
        mult = jnp.where(om > 0.0, om * lax.rsqrt(om), 0.0)
        mult = jnp.concatenate([jnp.where(start_row, 1.0, mult[:SUBLANES]), mult[SUBLANES:]],
                               axis=0)
        u3 = (mult * (i * xh)).reshape(groups, SUBLANES, LANES)
        a3 = a.reshape(groups, SUBLANES, LANES)

        az = jnp.where(row0, 0.0, a3)
        a_in = jnp.where(row0, a3, 0.0)
        az2 = az * pltpu.roll(az, 1, axis=1)
        az4 = az2 * pltpu.roll(az2, 2, axis=1)
        h_prev = hcar[0:1, sl]
        hs = []
        for gi in range(groups):
            if gi == groups // 2:
                emit()
            hg = u3[gi] + a_in[gi] * h_prev
            hg = hg + az[gi] * pltpu.roll(hg, 1, axis=0)
            hg = hg + az2[gi] * pltpu.roll(hg, 2, axis=0)
            hg = hg + az4[gi] * pltpu.roll(hg, 4, axis=0)
            hs.append(hg)
            h_prev = hg[SUBLANES - 1:SUBLANES, :]
        hcar[0:1, sl] = h_prev
        h = jnp.concatenate(hs, axis=0)
        o_ref[:, sl] = (h * _gelu_tanh(gate(sl))).astype(o_ref.dtype)
    for piece in pieces:
        piece()


def _proj_kernel(x_ref, sh_ref, sc_ref, g_ref, wx_ref, wg_ref, wr_ref, wcat_ref, cw_ref, cb_ref,
                 ba_ref, bx_ref, lam_ref, rest_ref, rec_ref, h_ref, lx_ref, gt_ref, xtail, hcar, *,
                 nl, per_b, conv_k):
    i = pl.program_id(0)
    j = pl.program_id(1)
    tm = x_ref.shape[0]
    rc = tm // ROW_CHUNKS
    cw = rec_ref.shape[1]
    per_step = wx_ref.shape[1] // cw

    def lru_cols(rows, h, base):
        yx = jnp.dot(h, wx_ref[...], preferred_element_type=F32).astype(BF16)
        yg = jnp.dot(h, wg_ref[...], preferred_element_type=F32).astype(BF16)
        for k in range(per_step):
            lx_ref[base + k, rows, :] = yx[:, k * cw:(k + 1) * cw]
            gt_ref[base + k, rows, :] = yg[:, k * cw:(k + 1) * cw]

    @pl.when(j == 0)
    def _():
        for r in range(ROW_CHUNKS):
            rows = pl.ds(r * rc, rc)
            h = _rms_mod(x_ref[rows, :], g_ref[...], sh_ref[...], sc_ref[...]).astype(BF16)
            h_ref[rows, :] = h
            lru_cols(rows, h, 0)

    @pl.when((j > 0) & (j < nl))
    def _():
        lru_cols(slice(None), h_ref[...], j * per_step)

    @pl.when(j >= nl)
    def _():
        c = j - nl
        seq_start = i % per_b == 0

        @pl.when(seq_start)
        def _():
            xtail[c] = jnp.zeros(xtail.shape[1:], F32)
            hcar[c] = jnp.zeros(hcar.shape[1:], F32)

        tr = rest_ref.shape[1]

        def rest_cols(lo):
            cols = slice(lo, min(lo + MXU_WIDTH, tr))
            rest_ref[:, cols] = jnp.dot(h_ref[...], wr_ref[:, cols],
                                        preferred_element_type=F32).astype(rest_ref.dtype)

        pieces = [functools.partial(rest_cols, lo) for lo in range(0, tr, MXU_WIDTH)]
        _rglru_tile(lx_ref[c].astype(F32), lambda sl: gt_ref[c, :, sl].astype(F32), seq_start,
                    wcat_ref, cw_ref, cb_ref, ba_ref, bx_ref, lam_ref, rec_ref,
                    xtail.at[c], hcar.at[c], conv_k, pieces)


def _proj(x2d, shift, scale, g, w_lru_cols, w_rest, wcat, conv_w, conv_b, ba, bx, lam, seq):
    m, d = x2d.shape
    lru_w = w_lru_cols.shape[1] // 2
    rest_w = w_rest.shape[1]
    conv_k = conv_w.shape[0]
    nr = max(n for n in (1, 2, 4) if lru_w % (n * LANES) == 0 and rest_w % (n * LANES) == 0)
    cw, tr = lru_w // nr, rest_w // nr
    tl = _tile(lru_w, 1024, cw)
    nl = lru_w // tl
    tm = _tile(seq, 512, SUBLANES * ROW_CHUNKS)
    per_b = seq // tm
    row = lambda i, j: (i // per_b, 0, 0)
    lstep = lambda j: jnp.minimum(j, nl - 1)
    chunk = lambda j: jnp.maximum(j - nl, 0)
    vec = pl.BlockSpec((1, cw), lambda i, j: (0, chunk(j)))
    kernel = functools.partial(_proj_kernel, nl=nl, per_b=per_b, conv_k=conv_k)
    return pl.pallas_call(
        kernel,
        grid=(m // tm, nl + nr),
        in_specs=[pl.BlockSpec((tm, d), lambda i, j: (i, 0)),
                  pl.BlockSpec((None, 1, d), row),
                  pl.BlockSpec((None, 1, d), row),
                  pl.BlockSpec((1, d), lambda i, j: (0, 0)),
                  pl.BlockSpec((d, tl), lambda i, j: (0, lstep(j))),
                  pl.BlockSpec((d, tl), lambda i, j: (0, nl + lstep(j))),
                  pl.BlockSpec((d, tr), lambda i, j: (0, chunk(j))),
                  pl.BlockSpec((cw // LANES, LANES, 2 * LANES), lambda i, j: (chunk(j), 0, 0)),
                  pl.BlockSpec((conv_k, cw), lambda i, j: (0, chunk(j))),
                  vec, vec, vec, vec],
        out_specs=[pl.BlockSpec((tm, tr), lambda i, j: (i, chunk(j))),
                   pl.BlockSpec((tm, cw), lambda i, j: (i, chunk(j)))],
        out_shape=[jax.ShapeDtypeStruct((m, rest_w), BF16),
                   jax.ShapeDtypeStruct((m, lru_w), BF16)],
        scratch_shapes=[pltpu.VMEM((tm, d), BF16),
                        pltpu.VMEM((nr, tm, cw), BF16), pltpu.VMEM((nr, tm, cw), BF16),
                        pltpu.VMEM((nr, SUBLANES, cw), F32), pltpu.VMEM((nr, SUBLANES, cw), F32)],
        compiler_params=_params("arbitrary", "arbitrary"),
        name="proj",
    )(x2d, shift, scale, g, w_lru_cols, w_lru_cols, w_rest, wcat, conv_w, conv_b.reshape(1, -1),
      ba.reshape(1, -1), bx.reshape(1, -1), lam.reshape(1, -1))


def _t5_bucket(rel, n_buckets):
    max_exact = n_buckets // 2
    relf = jnp.maximum(rel, 1).astype(F32)
    large = max_exact + (jnp.log(relf / max_exact) / math.log(MAX_DISTANCE / max_exact)
                         * (n_buckets - max_exact)).astype(jnp.int32)
    large = jnp.minimum(large, n_buckets - 1)
    return jnp.where(rel < max_exact, rel, large)


def _band_bias_kernel(rb_ref, sink_ref, bucket_ref, o_ref, *, pairs, n_buckets):
    kv = pl.program_id(0)
    blk = ATTN_BLOCK
    bucket = bucket_ref[...]
    qi = lax.broadcasted_iota(jnp.int32, (blk, 2 * blk), 0)
    ki = lax.broadcasted_iota(jnp.int32, (blk, 2 * blk), 1)
    rel = qi + blk - ki
    valid = (rel >= 0) & (rel < blk)
    for p in range(pairs):
        for par in range(2):
            h = (kv * pairs + p) * 2 + par
            bias = jnp.zeros((blk, 2 * blk), F32)
            for b in range(n_buckets):
                bias = jnp.where(bucket == b, rb_ref[b, h], bias)
            sink = sink_ref[h]
            rows, cols = slice(p * blk, (p + 1) * blk), slice(par * 2 * blk, (par + 1) * 2 * blk)
            o_ref[rows, cols] = jnp.where(ki == 0, sink, jnp.where(valid, bias, NEG_INF))


def _band_bias(rel_bias, sinks, n_kv):
    n_buckets, n_heads = rel_bias.shape
    pairs = n_heads // n_kv // 2
    blk = ATTN_BLOCK
    qi = jnp.arange(blk)[:, None]
    ki = jnp.arange(2 * blk)[None, :]
    bucket = _t5_bucket(jnp.maximum(qi + blk - ki, 0), n_buckets)
    smem = pl.BlockSpec(memory_space=pltpu.SMEM)
    return pl.pallas_call(
        functools.partial(_band_bias_kernel, pairs=pairs, n_buckets=n_buckets),
        grid=(n_kv,),
        in_specs=[smem, smem, pl.BlockSpec((blk, 2 * blk), lambda kv: (0, 0))],
        out_specs=pl.BlockSpec((None, pairs * blk, 4 * blk), lambda kv: (kv, 0, 0)),
        out_shape=jax.ShapeDtypeStruct((n_kv, pairs * blk, 4 * blk), F32),
        compiler_params=_params("parallel"),
        name="band_bias",
    )(rel_bias, sinks, bucket)


def _swap_lane_halves(v):
    u = pltpu.bitcast(v, jnp.uint32)
    return pltpu.bitcast(pltpu.roll(u, LANES // 2, axis=1), BF16)


def _mix_kernel(*refs, n_kv, pairs, scale, per_b):
    q_ref, kp_ref, kc_ref, vp_ref, vc_ref, bm_ref, rec_ref = refs[:7]
    ga_refs, gb_refs = refs[7:7 + n_kv], refs[7 + n_kv:7 + 2 * n_kv]
    x_ref, gate_ref, wl_ref, wa_ref, wo_ref, o_ref, att_ref, ya_ref, m_ref = refs[7 + 2 * n_kv:]
    blk = ATTN_BLOCK
    d = o_ref.shape[1]
    cc = d // n_kv
    fold = math.log2(scale).is_integer()
    lane = lax.broadcasted_iota(jnp.int32, (blk, LANES), 1)
    key = lax.broadcasted_iota(jnp.int32, (blk, LANES), 0)
    low = jnp.where(lane < LANES // 2, 1.0, 0.0).astype(BF16)
    high = jnp.where(lane < LANES // 2, 0.0, 1.0).astype(BF16)
    not_key0 = jnp.where(key == 0, 0.0, 1.0).astype(BF16)
    ones_blk = jnp.concatenate([low, low, high, high], axis=0)
    key_col = lax.broadcasted_iota(jnp.int32, (1, 4 * blk), 1) % (2 * blk)
    seq_start = pl.program_id(0) % per_b == 0
    start_mask = jnp.where(seq_start & (key_col >= 1) & (key_col < blk), NEG_INF, 0.0)

    def halves(t, kv):
        s = _swap_lane_halves(t)
        lo, hi = (t, s) if kv % 2 == 0 else (s, t)
        return lo * low, hi * high

    def window(prev, cur):
        return jnp.concatenate([prev[0] * not_key0, cur[0], prev[1] * not_key0, cur[1]], axis=0)

    for kv in range(n_kv):
        sl = slice((kv // 2) * LANES, (kv // 2 + 1) * LANES)
        kb = [halves(kp_ref[:, sl], kv)] + [halves(kc_ref[t * blk:(t + 1) * blk, sl], kv)
                                            for t in range(QUERY_BLOCKS)]
        vb = [halves(vp_ref[:, sl], kv)] + [halves(vc_ref[t * blk:(t + 1) * blk, sl], kv)
                                            for t in range(QUERY_BLOCKS)]
        for t in range(QUERY_BLOCKS):
            rows = slice(t * blk, (t + 1) * blk)
            kblk = window(kb[t], kb[t + 1])
            vaug = jnp.concatenate([window(vb[t], vb[t + 1]), ones_blk], axis=1)
            q = jnp.concatenate(
                [q_ref[rows, (kv * pairs + p) * LANES:(kv * pairs + p + 1) * LANES]
                 for p in range(pairs)], axis=0)
            if fold:
                q = q * scale
            s = lax.dot_general(q, kblk, (((1,), (1,)), ((), ())), preferred_element_type=F32)
            if not fold:
                s = s * scale
            s = s + bm_ref[kv]
            if t == 0:
                s = s + start_mask
            es = []
            for par in range(2):
                sp = s[:, par * 2 * blk:(par + 1) * 2 * blk]
                es.append(jnp.exp(sp - jnp.max(sp, axis=-1, keepdims=True)).astype(BF16))
            o = jnp.dot(jnp.concatenate(es, axis=1), vaug, preferred_element_type=F32)
            res = (o[:, :LANES] / o[:, LANES:]).astype(BF16)
            for p in range(pairs):
                att_ref[rows, (kv * pairs + p) * LANES:(kv * pairs + p + 1) * LANES] = (
                    res[p * blk:(p + 1) * blk])
        cols = slice(kv * cc, (kv + 1) * cc)
        ya = jnp.dot(rec_ref[...], wl_ref[:, cols], preferred_element_type=F32)
        ya_ref[:, cols] = _sigmoid(ga_refs[kv][...].astype(F32)) * ya

    for c in range(n_kv):
        cols = slice(c * cc, (c + 1) * cc)
        yb = jnp.dot(att_ref[...], wa_ref[:, cols], preferred_element_type=F32)
        gb = _sigmoid(gb_refs[c][...].astype(F32))
        m_ref[:, cols] = (ya_ref[:, cols] + gb * yb).astype(BF16)
    for c in range(n_kv):
        cols = slice(c * cc, (c + 1) * cc)
        y = jnp.dot(m_ref[...], wo_ref[:, cols], preferred_element_type=F32)
        o_ref[:, cols] = x_ref[:, cols] + gate_ref[:, cols] * y


def _mix(rest, rec, x2d, gate, band_bias, w_lru, w_att, w_out, seq, q_off, k_off, v_off, ga_off,
         gb_off, kv_w, n_heads):
    m, d = x2d.shape
    attn_w = w_att.shape[0]
    hd = attn_w // n_heads
    n_kv = kv_w // hd
    pairs = n_heads // n_kv // 2
    blk = ATTN_BLOCK
    tm = QUERY_BLOCKS * blk
    per_b = seq // tm
    cc = d // n_kv
    assert 2 * hd == LANES and n_heads == 2 * pairs * n_kv and kv_w % LANES == 0
    assert q_off % attn_w == 0 and k_off % kv_w == 0 and v_off % kv_w == 0
    assert seq % tm == 0 and cc % LANES == 0 and ga_off % cc == 0 and gb_off % cc == 0
    kb, vb = k_off // kv_w, v_off // kv_w
    prev = lambda col: (lambda i: (jnp.maximum(QUERY_BLOCKS * i - 1, 0), col))
    col_chunk = lambda off: [pl.BlockSpec((tm, cc), functools.partial(lambda c, i: (i, c),
                                                                      off // cc + c))
                             for c in range(n_kv)]
    resident = lambda a: pl.BlockSpec(a.shape, lambda i: (0,) * a.ndim,
                                      pipeline_mode=pl.Buffered(1))
    kernel = functools.partial(_mix_kernel, n_kv=n_kv, pairs=pairs, scale=hd ** -0.5, per_b=per_b)
    return pl.pallas_call(
        kernel,
        grid=(m // tm,),
        in_specs=[pl.BlockSpec((tm, attn_w), lambda i: (i, q_off // attn_w)),
                  pl.BlockSpec((blk, kv_w), prev(kb)),
                  pl.BlockSpec((tm, kv_w), lambda i: (i, kb)),
                  pl.BlockSpec((blk, kv_w), prev(vb)),
                  pl.BlockSpec((tm, kv_w), lambda i: (i, vb)),
                  resident(band_bias),
                  pl.BlockSpec((tm, rec.shape[1]), lambda i: (i, 0))]
                 + col_chunk(ga_off) + col_chunk(gb_off)
                 + [pl.BlockSpec((tm, d), lambda i: (i, 0)),
                    pl.BlockSpec((None, 1, d), lambda i: (i // per_b, 0, 0)),
                    resident(w_lru), resident(w_att), resident(w_out)],
        out_specs=pl.BlockSpec((tm, d), lambda i: (i, 0)),
        out_shape=jax.ShapeDtypeStruct((m, d), F32),
        scratch_shapes=[pltpu.VMEM((tm, attn_w), BF16), pltpu.VMEM((tm, d), F32),
                        pltpu.VMEM((tm, d), BF16)],
        compiler_params=_params("parallel"),
        name="mix",
    )(rest, rest, rest, rest, rest, band_bias, rec, *([rest] * (2 * n_kv)), x2d, gate,
      w_lru, w_att, w_out)


def _mlp_kernel(x_ref, sh_ref, sc_ref, gate_ref, g2_ref, gf_ref, w1_ref, w2_ref, o_ref,
                h_ref, ff_ref, *, nf):
    f = pl.program_id(1)
    rc = x_ref.shape[0] // ROW_CHUNKS

    def up(rows, slot):
        ff = jnp.dot(h_ref[rows, :], w1_ref[...], preferred_element_type=F32)
        ff_ref[slot, rows, :] = jnp.square(jnp.maximum(ff, 0.0)).astype(BF16)

    def down(rows, slot):
        return jnp.dot(ff_ref[slot, rows, :], w2_ref[...], preferred_element_type=F32)

    @pl.when(f == 0)
    def _():
        o_ref[...] = jnp.zeros_like(o_ref)
        for r in range(ROW_CHUNKS):
            rows = pl.ds(r * rc, rc)
            h_ref[rows, :] = _rms_mod(x_ref[rows, :], g2_ref[...], sh_ref[...],
                                      sc_ref[...]).astype(BF16)
            up(rows, 0)

    @pl.when((f > 0) & (f < nf))
    def _():
        o_ref[...] += down(slice(None), (f - 1) % 2)
        up(slice(None), f % 2)

    @pl.when(f == nf)
    def _():
        for r in range(ROW_CHUNKS):
            rows = pl.ds(r * rc, rc)
            x2 = x_ref[rows, :] + gate_ref[...] * (o_ref[rows, :] + down(rows, (nf - 1) % 2))
            var = jnp.mean(x2 * x2, axis=-1, keepdims=True)
            o_ref[rows, :] = x2 * lax.rsqrt(var + EPS) * gf_ref[...]


def _mlp(x1, shift, scale, gate, g2, gf, w1, w2, seq):
    m, d = x1.shape
    dff = w1.shape[1]
    tm = _tile(seq, 512, SUBLANES * ROW_CHUNKS)
    tf = _tile(dff, 1024, LANES)
    nf = dff // tf
    per_b = seq // tm
    row = lambda i, f: (i // per_b, 0, 0)
    const = lambda i, f: (0, 0)
    return pl.pallas_call(
        functools.partial(_mlp_kernel, nf=nf),
        grid=(m // tm, nf + 1),
        in_specs=[pl.BlockSpec((tm, d), lambda i, f: (i, 0)),
                  pl.BlockSpec((None, 1, d), row),
                  pl.BlockSpec((None, 1, d), row),
                  pl.BlockSpec((None, 1, d), row),
                  pl.BlockSpec((1, d), const),
                  pl.BlockSpec((1, d), const),
                  pl.BlockSpec((d, tf), lambda i, f: (0, jnp.minimum(f, nf - 1))),
                  pl.BlockSpec((tf, d), lambda i, f: (jnp.maximum(f - 1, 0), 0))],
        out_specs=pl.BlockSpec((tm, d), lambda i, f: (i, 0)),
        out_shape=jax.ShapeDtypeStruct((m, d), F32),
        scratch_shapes=[pltpu.VMEM((tm, d), BF16), pltpu.VMEM((2, tm, tf), BF16)],
        compiler_params=_params("parallel", "arbitrary"),
        name="mlp",
    )(x1, shift, scale, gate, g2, gf, w1, w2)


def kernel(x, c, w_ada, b_ada, norm1_g, w_in, conv_w, conv_b, lru_wa, lru_ba, lru_wx, lru_bx,
           lru_lambda, w_lru_out, w_attn_out, attn_sinks, rel_bias, w_out, norm2_g, w_ff1, w_ff2,
           final_g):
    bsz, seq, d = x.shape
    depth = w_in.shape[0]
    lru_w = w_lru_out.shape[1]
    attn_w = w_attn_out.shape[1]
    kv_w = (w_in.shape[2] - 2 * lru_w - attn_w - 2 * d) // 2
    n_heads = attn_sinks.shape[1]
    n_kv = kv_w // (attn_w // n_heads)
    q_off = 0
    k_off = attn_w
    v_off = k_off + kv_w
    ga_off = v_off + kv_w
    gb_off = ga_off + d

    xs = x.reshape(bsz * seq, d)
    for l in range(depth):
        band_bias = _band_bias(rel_bias, attn_sinks[l], n_kv)
        mod = _adaln(c, w_ada[l], b_ada[l])
        shift1, scale1, gate1, shift2, scale2, gate2 = [
            t.reshape(bsz, 1, d) for t in jnp.split(mod, 6, axis=-1)]

        wcat = jnp.concatenate([lru_wa[l], lru_wx[l]], axis=-1).astype(BF16)
        rest, rec = _proj(xs, shift1, scale1, norm1_g[l].reshape(1, d),
                          w_in[l, :, :2 * lru_w].astype(BF16), w_in[l, :, 2 * lru_w:].astype(BF16),
                          wcat, conv_w[l], conv_b[l], lru_ba[l], lru_bx[l], lru_lambda[l], seq)
        x1 = _mix(rest, rec, xs, gate1, band_bias, w_lru_out[l].astype(BF16),
                  w_attn_out[l].astype(BF16), w_out[l].astype(BF16), seq, q_off, k_off, v_off,
                  ga_off, gb_off, kv_w, n_heads)
        if l != depth - 1:
            raise NotImplementedError("the fused final norm assumes a single layer")
        xs = _mlp(x1, shift2, scale2, gate2, norm2_g[l].reshape(1, d), final_g.reshape(1, d),
                  w_ff1[l].astype(BF16), w_ff2[l].astype(BF16), seq)
    return xs.reshape(bsz, seq, d)
```

```python
import functools
import math

import jax
import jax.numpy as jnp
from jax import lax
from jax.experimental import pallas as pl
from jax.experimental.pallas import tpu as pltpu

EPS = 1e-6
LRU_C = 8.0
LOG2E = 1.4426950408889634
ATTN_BLOCK = 128
NEG_INF = -1e30
MAX_DISTANCE = 128
LANES = 128
SUBLANES = 8
MXU_WIDTH = 256
VMEM_LIMIT = 56 * 1024 * 1024
ROW_CHUNKS = 2
QUERY_BLOCKS = 2

F32 = jnp.float32
BF16 = jnp.bfloat16


def _tile(n, target, align):
    best = None
    t = align
    while t <= min(n, target):
        if n % t == 0:
            best = t
        t += align
    if best is None:
        raise ValueError(f"no tile for {n} (target {target}, align {align})")
    return best


def _params(*sem):
    return pltpu.CompilerParams(dimension_semantics=sem, vmem_limit_bytes=VMEM_LIMIT)


def _sigmoid(v):
    return 1.0 / (1.0 + jnp.exp2(v * -LOG2E))


def _rms_mod(x, g, shift, scale):
    var = jnp.mean(x * x, axis=-1, keepdims=True)
    y = x * lax.rsqrt(var + EPS) * g
    return y * (1.0 + scale) + shift


def _adaln_kernel(c_ref, w_ref, b_ref, o_ref):
    c = c_ref[...]
    act = (c * _sigmoid(c)).astype(BF16)
    o_ref[...] = jnp.dot(act, w_ref[...].astype(BF16), preferred_element_type=F32) + b_ref[...]


def _adaln(c, w, b):
    bsz, d = c.shape
    n = w.shape[1]
    tn = _tile(n, 1024, LANES)
    return pl.pallas_call(
        _adaln_kernel,
        grid=(n // tn,),
        in_specs=[pl.BlockSpec((bsz, d), lambda j: (0, 0)),
                  pl.BlockSpec((d, tn), lambda j: (0, j)),
                  pl.BlockSpec((1, tn), lambda j: (0, j))],
        out_specs=pl.BlockSpec((bsz, tn), lambda j: (0, j)),
        out_shape=jax.ShapeDtypeStruct((bsz, n), F32),
        compiler_params=_params("parallel"),
        name="adaln_mod",
    )(c, w, b.reshape(1, n))


def _gelu_tanh(v):
    k1 = -2.0 * math.sqrt(2.0 / math.pi) * LOG2E
    return v / (1.0 + jnp.exp2(v * (k1 + (k1 * 0.044715) * (v * v))))


def _rglru_tile(x, gate, seq_start, w_ref, cw_ref, cb_ref, ba_ref, bx_ref, lam_ref, o_ref,
                xtail, hcar, conv_k, other_mxu_work):
    tt, cw = x.shape
    groups = tt // SUBLANES

    assert conv_k - 1 < SUBLANES
    xs = jnp.concatenate([xtail[...], x], axis=0).reshape(groups + 1, SUBLANES, cw)
    xtail[...] = x[tt - SUBLANES:, :]
    first = lax.broadcasted_iota(jnp.int32, (groups + 1, SUBLANES, cw), 1) == 0
    xc = cb_ref[...] + cw_ref[conv_k - 1:conv_k, :] * x
    for k in range(conv_k - 2, -1, -1):
        rot = pltpu.roll(xs, 1, axis=1)
        xs = jnp.where(first, jnp.concatenate([rot[:1], rot[:-1]], axis=0), rot)
        xc = xc + cw_ref[k:k + 1, :] * xs[1:].reshape(tt, cw)

    lam = lam_ref[...]
    softplus_neg_lam = jnp.maximum(-lam, 0.0) + jnp.log(1.0 + jnp.exp(-jnp.abs(lam)))
    log2_a_coef = (-LRU_C * LOG2E) * softplus_neg_lam
    row0 = lax.broadcasted_iota(jnp.int32, (groups, SUBLANES, LANES), 1) == 0
    start_row = (lax.broadcasted_iota(jnp.int32, (SUBLANES, LANES), 0) == 0) & seq_start

    n_chunks = cw // LANES

    def gates(c):
        return jnp.dot(xc[:, c * LANES:(c + 1) * LANES].astype(BF16), w_ref[c],
                       preferred_element_type=F32)

    pieces = iter(other_mxu_work)

    def emit():
        piece = next(pieces, None)
        if piece is not None:
            piece()

    emit()
    zs = [gates(0)]
    for c in range(n_chunks):
        sl = slice(c * LANES, (c + 1) * LANES)
        if c + 1 < n_chunks:
            zs.append(gates(c + 1))
        emit()
        xh = xc[:, sl]
        z = zs[c]
        r = _sigmoid(z[:, :LANES] + ba_ref[:, sl])
        i = _sigmoid(z[:, LANES:] + bx_ref[:, sl])
        a = jnp.exp2(log2_a_coef[:, sl] * r)
        om = 1.0 - a * a
        mult = jnp.where(om > 0.0, om * lax.rsqrt(om), 0.0)
        mult = jnp.concatenate([jnp.where(start_row, 1.0, mult[:SUBLANES]), mult[SUBLANES:]],
                               axis=0)
        u3 = (mult * (i * xh)).reshape(groups, SUBLANES, LANES)
        a3 = a.reshape(groups, SUBLANES, LANES)

        az = jnp.where(row0, 0.0, a3)
        a_in = jnp.where(row0, a3, 0.0)
        az2 = az * pltpu.roll(az, 1, axis=1)
        az4 = az2 * pltpu.roll(az2, 2, axis=1)
        h_prev = hcar[0:1, sl]
        hs = []
        for gi in range(groups):
            if gi == groups // 2:
                emit()
            hg = u3[gi] + a_in[gi] * h_prev
            hg = hg + az[gi] * pltpu.roll(hg, 1, axis=0)
            hg = hg + az2[gi] * pltpu.roll(hg, 2, axis=0)
            hg = hg + az4[gi] * pltpu.roll(hg, 4, axis=0)
            hs.append(hg)
            h_prev = hg[SUBLANES - 1:SUBLANES, :]
        hcar[0:1, sl] = h_prev
        h = jnp.concatenate(hs, axis=0)
        o_ref[:, sl] = (h * _gelu_tanh(gate(sl))).astype(o_ref.dtype)
    for piece in pieces:
        piece()


def _proj_kernel(x_ref, sh_ref, sc_ref, g_ref, wx_ref, wg_ref, wr_ref, wcat_ref, cw_ref, cb_ref,
                 ba_ref, bx_ref, lam_ref, rest_ref, rec_ref, h_ref, lx_ref, gt_ref, xtail, hcar, *,
                 nl, per_b, conv_k):
    i = pl.program_id(0)
    j = pl.program_id(1)
    tm = x_ref.shape[0]
    rc = tm // ROW_CHUNKS
    cw = rec_ref.shape[1]
    per_step = wx_ref.shape[1] // cw

    def lru_cols(rows, h, base):
        yx = jnp.dot(h, wx_ref[...], preferred_element_type=F32).astype(BF16)
        yg = jnp.dot(h, wg_ref[...], preferred_element_type=F32).astype(BF16)
        for k in range(per_step):
            lx_ref[base + k, rows, :] = yx[:, k * cw:(k + 1) * cw]
            gt_ref[base + k, rows, :] = yg[:, k * cw:(k + 1) * cw]

    @pl.when(j == 0)
    def _():
        for r in range(ROW_CHUNKS):
            rows = pl.ds(r * rc, rc)
            h = _rms_mod(x_ref[rows, :], g_ref[...], sh_ref[...], sc_ref[...]).astype(BF16)
            h_ref[rows, :] = h
            lru_cols(rows, h, 0)

    @pl.when((j > 0) & (j < nl))
    def _():
        lru_cols(slice(None), h_ref[...], j * per_step)

    @pl.when(j >= nl)
    def _():
        c = j - nl
        seq_start = i % per_b == 0

        @pl.when(seq_start)
        def _():
            xtail[c] = jnp.zeros(xtail.shape[1:], F32)
            hcar[c] = jnp.zeros(hcar.shape[1:], F32)

        tr = rest_ref.shape[1]

        def rest_cols(lo):
            cols = slice(lo, min(lo + MXU_WIDTH, tr))
            rest_ref[:, cols] = jnp.dot(h_ref[...], wr_ref[:, cols],
                                        preferred_element_type=F32).astype(rest_ref.dtype)

        pieces = [functools.partial(rest_cols, lo) for lo in range(0, tr, MXU_WIDTH)]
        _rglru_tile(lx_ref[c].astype(F32), lambda sl: gt_ref[c, :, sl].astype(F32), seq_start,
                    wcat_ref, cw_ref, cb_ref, ba_ref, bx_ref, lam_ref, rec_ref,
                    xtail.at[c], hcar.at[c], conv_k, pieces)


def _proj(x2d, shift, scale, g, w_lru_cols, w_rest, wcat, conv_w, conv_b, ba, bx, lam, seq):
    m, d = x2d.shape
    lru_w = w_lru_cols.shape[1] // 2
    rest_w = w_rest.shape[1]
    conv_k = conv_w.shape[0]
    nr = max(n for n in (1, 2, 4) if lru_w % (n * LANES) == 0 and rest_w % (n * LANES) == 0)
    cw, tr = lru_w // nr, rest_w // nr
    tl = _tile(lru_w, 1024, cw)
    nl = lru_w // tl
    tm = _tile(seq, 512, SUBLANES * ROW_CHUNKS)
    per_b = seq // tm
    row = lambda i, j: (i // per_b, 0, 0)
    lstep = lambda j: jnp.minimum(j, nl - 1)
    chunk = lambda j: jnp.maximum(j - nl, 0)
    vec = pl.BlockSpec((1, cw), lambda i, j: (0, chunk(j)))
    kernel = functools.partial(_proj_kernel, nl=nl, per_b=per_b, conv_k=conv_k)
    return pl.pallas_call(
        kernel,
        grid=(m // tm, nl + nr),
        in_specs=[pl.BlockSpec((tm, d), lambda i, j: (i, 0)),
                  pl.BlockSpec((None, 1, d), row),
                  pl.BlockSpec((None, 1, d), row),
                  pl.BlockSpec((1, d), lambda i, j: (0, 0)),
                  pl.BlockSpec((d, tl), lambda i, j: (0, lstep(j))),
                  pl.BlockSpec((d, tl), lambda i, j: (0, nl + lstep(j))),
                  pl.BlockSpec((d, tr), lambda i, j: (0, chunk(j))),
                  pl.BlockSpec((cw // LANES, LANES, 2 * LANES), lambda i, j: (chunk(j), 0, 0)),
                  pl.BlockSpec((conv_k, cw), lambda i, j: (0, chunk(j))),
                  vec, vec, vec, vec],
        out_specs=[pl.BlockSpec((tm, tr), lambda i, j: (i, chunk(j))),
                   pl.BlockSpec((tm, cw), lambda i, j: (i, chunk(j)))],
        out_shape=[jax.ShapeDtypeStruct((m, rest_w), BF16),
                   jax.ShapeDtypeStruct((m, lru_w), BF16)],
        scratch_shapes=[pltpu.VMEM((tm, d), BF16),
                        pltpu.VMEM((nr, tm, cw), BF16), pltpu.VMEM((nr, tm, cw), BF16),
                        pltpu.VMEM((nr, SUBLANES, cw), F32), pltpu.VMEM((nr, SUBLANES, cw), F32)],
        compiler_params=_params("arbitrary", "arbitrary"),
        name="proj",
    )(x2d, shift, scale, g, w_lru_cols, w_lru_cols, w_rest, wcat, conv_w, conv_b.reshape(1, -1),
      ba.reshape(1, -1), bx.reshape(1, -1), lam.reshape(1, -1))


def _t5_bucket(rel, n_buckets):
    max_exact = n_buckets // 2
    relf = jnp.maximum(rel, 1).astype(F32)
    large = max_exact + (jnp.log(relf / max_exact) / math.log(MAX_DISTANCE / max_exact)
                         * (n_buckets - max_exact)).astype(jnp.int32)
    large = jnp.minimum(large, n_buckets - 1)
    return jnp.where(rel < max_exact, rel, large)


def _band_bias_kernel(rb_ref, sink_ref, bucket_ref, o_ref, *, pairs, n_buckets):
    kv = pl.program_id(0)
    blk = ATTN_BLOCK
    bucket = bucket_ref[...]
    qi = lax.broadcasted_iota(jnp.int32, (blk, 2 * blk), 0)
    ki = lax.broadcasted_iota(jnp.int32, (blk, 2 * blk), 1)
    rel = qi + blk - ki
    valid = (rel >= 0) & (rel < blk)
    for p in range(pairs):
        for par in range(2):
            h = (kv * pairs + p) * 2 + par
            bias = jnp.zeros((blk, 2 * blk), F32)
            for b in range(n_buckets):
                bias = jnp.where(bucket == b, rb_ref[b, h], bias)
            sink = sink_ref[h]
            rows, cols = slice(p * blk, (p + 1) * blk), slice(par * 2 * blk, (par + 1) * 2 * blk)
            o_ref[rows, cols] = jnp.where(ki == 0, sink, jnp.where(valid, bias, NEG_INF))


def _band_bias(rel_bias, sinks, n_kv):
    n_buckets, n_heads = rel_bias.shape
    pairs = n_heads // n_kv // 2
    blk = ATTN_BLOCK
    qi = jnp.arange(blk)[:, None]
    ki = jnp.arange(2 * blk)[None, :]
    bucket = _t5_bucket(jnp.maximum(qi + blk - ki, 0), n_buckets)
    smem = pl.BlockSpec(memory_space=pltpu.SMEM)
    return pl.pallas_call(
        functools.partial(_band_bias_kernel, pairs=pairs, n_buckets=n_buckets),
        grid=(n_kv,),
        in_specs=[smem, smem, pl.BlockSpec((blk, 2 * blk), lambda kv: (0, 0))],
        out_specs=pl.BlockSpec((None, pairs * blk, 4 * blk), lambda kv: (kv, 0, 0)),
        out_shape=jax.ShapeDtypeStruct((n_kv, pairs * blk, 4 * blk), F32),
        compiler_params=_params("parallel"),
        name="band_bias",
    )(rel_bias, sinks, bucket)


def _swap_lane_halves(v):
    u = pltpu.bitcast(v, jnp.uint32)
    return pltpu.bitcast(pltpu.roll(u, LANES // 2, axis=1), BF16)


def _mix_kernel(*refs, n_kv, pairs, scale, per_b):
    q_ref, kp_ref, kc_ref, vp_ref, vc_ref, bm_ref, rec_ref = refs[:7]
    ga_refs, gb_refs = refs[7:7 + n_kv], refs[7 + n_kv:7 + 2 * n_kv]
    x_ref, gate_ref, wl_ref, wa_ref, wo_ref, o_ref, att_ref, ya_ref, m_ref = refs[7 + 2 * n_kv:]
    blk = ATTN_BLOCK
    d = o_ref.shape[1]
    cc = d // n_kv
    fold = math.log2(scale).is_integer()
    lane = lax.broadcasted_iota(jnp.int32, (blk, LANES), 1)
    key = lax.broadcasted_iota(jnp.int32, (blk, LANES), 0)
    low = jnp.where(lane < LANES // 2, 1.0, 0.0).astype(BF16)
    high = jnp.where(lane < LANES // 2, 0.0, 1.0).astype(BF16)
    not_key0 = jnp.where(key == 0, 0.0, 1.0).astype(BF16)
    ones_blk = jnp.concatenate([low, low, high, high], axis=0)
    key_col = lax.broadcasted_iota(jnp.int32, (1, 4 * blk), 1) % (2 * blk)
    seq_start = pl.program_id(0) % per_b == 0
    start_mask = jnp.where(seq_start & (key_col >= 1) & (key_col < blk), NEG_INF, 0.0)

    def halves(t, kv):
        s = _swap_lane_halves(t)
        lo, hi = (t, s) if kv % 2 == 0 else (s, t)
        return lo * low, hi * high

    def window(prev, cur):
        return jnp.concatenate([prev[0] * not_key0, cur[0], prev[1] * not_key0, cur[1]], axis=0)

    for kv in range(n_kv):
        sl = slice((kv // 2) * LANES, (kv // 2 + 1) * LANES)
        kb = [halves(kp_ref[:, sl], kv)] + [halves(kc_ref[t * blk:(t + 1) * blk, sl], kv)
                                            for t in range(QUERY_BLOCKS)]
        vb = [halves(vp_ref[:, sl], kv)] + [halves(vc_ref[t * blk:(t + 1) * blk, sl], kv)
                                            for t in range(QUERY_BLOCKS)]
        for t in range(QUERY_BLOCKS):
            rows = slice(t * blk, (t + 1) * blk)
            kblk = window(kb[t], kb[t + 1])
            vaug = jnp.concatenate([window(vb[t], vb[t + 1]), ones_blk], axis=1)
            q = jnp.concatenate(
                [q_ref[rows, (kv * pairs + p) * LANES:(kv * pairs + p + 1) * LANES]
                 for p in range(pairs)], axis=0)
            if fold:
                q = q * scale
            s = lax.dot_general(q, kblk, (((1,), (1,)), ((), ())), preferred_element_type=F32)
            if not fold:
                s = s * scale
            s = s + bm_ref[kv]
            if t == 0:
                s = s + start_mask
            es = []
            for par in range(2):
                sp = s[:, par * 2 * blk:(par + 1) * 2 * blk]
                es.append(jnp.exp(sp - jnp.max(sp, axis=-1, keepdims=True)).astype(BF16))
            o = jnp.dot(jnp.concatenate(es, axis=1), vaug, preferred_element_type=F32)
            res = (o[:, :LANES] / o[:, LANES:]).astype(BF16)
            for p in range(pairs):
                att_ref[rows, (kv * pairs + p) * LANES:(kv * pairs + p + 1) * LANES] = (
                    res[p * blk:(p + 1) * blk])
        cols = slice(kv * cc, (kv + 1) * cc)
        ya = jnp.dot(rec_ref[...], wl_ref[:, cols], preferred_element_type=F32)
        ya_ref[:, cols] = _sigmoid(ga_refs[kv][...].astype(F32)) * ya

    for c in range(n_kv):
        cols = slice(c * cc, (c + 1) * cc)
        yb = jnp.dot(att_ref[...], wa_ref[:, cols], preferred_element_type=F32)
        gb = _sigmoid(gb_refs[c][...].astype(F32))
        m_ref[:, cols] = (ya_ref[:, cols] + gb * yb).astype(BF16)
    for c in range(n_kv):
        cols = slice(c * cc, (c + 1) * cc)
        y = jnp.dot(m_ref[...], wo_ref[:, cols], preferred_element_type=F32)
        o_ref[:, cols] = x_ref[:, cols] + gate_ref[:, cols] * y


def _mix(rest, rec, x2d, gate, band_bias, w_lru, w_att, w_out, seq, q_off, k_off, v_off, ga_off,
         gb_off, kv_w, n_heads):
    m, d = x2d.shape
    attn_w = w_att.shape[0]
    hd = attn_w // n_heads
    n_kv = kv_w // hd
    pairs = n_heads // n_kv // 2
    blk = ATTN_BLOCK
    tm = QUERY_BLOCKS * blk
    per_b = seq // tm
    cc = d // n_kv
    assert 2 * hd == LANES and n_heads == 2 * pairs * n_kv and kv_w % LANES == 0
    assert q_off % attn_w == 0 and k_off % kv_w == 0 and v_off % kv_w == 0
    assert seq % tm == 0 and cc % LANES == 0 and ga_off % cc == 0 and gb_off % cc == 0
    kb, vb = k_off // kv_w, v_off // kv_w
    prev = lambda col: (lambda i: (jnp.maximum(QUERY_BLOCKS * i - 1, 0), col))
    col_chunk = lambda off: [pl.BlockSpec((tm, cc), functools.partial(lambda c, i: (i, c),
                                                                      off // cc + c))
                             for c in range(n_kv)]
    resident = lambda a: pl.BlockSpec(a.shape, lambda i: (0,) * a.ndim,
                                      pipeline_mode=pl.Buffered(1))
    kernel = functools.partial(_mix_kernel, n_kv=n_kv, pairs=pairs, scale=hd ** -0.5, per_b=per_b)
    return pl.pallas_call(
        kernel,
        grid=(m // tm,),
        in_specs=[pl.BlockSpec((tm, attn_w), lambda i: (i, q_off // attn_w)),
                  pl.BlockSpec((blk, kv_w), prev(kb)),
                  pl.BlockSpec((tm, kv_w), lambda i: (i, kb)),
                  pl.BlockSpec((blk, kv_w), prev(vb)),
                  pl.BlockSpec((tm, kv_w), lambda i: (i, vb)),
                  resident(band_bias),
                  pl.BlockSpec((tm, rec.shape[1]), lambda i: (i, 0))]
                 + col_chunk(ga_off) + col_chunk(gb_off)
                 + [pl.BlockSpec((tm, d), lambda i: (i, 0)),
                    pl.BlockSpec((None, 1, d), lambda i: (i // per_b, 0, 0)),
                    resident(w_lru), resident(w_att), resident(w_out)],
        out_specs=pl.BlockSpec((tm, d), lambda i: (i, 0)),
        out_shape=jax.ShapeDtypeStruct((m, d), F32),
        scratch_shapes=[pltpu.VMEM((tm, attn_w), BF16), pltpu.VMEM((tm, d), F32),
                        pltpu.VMEM((tm, d), BF16)],
        compiler_params=_params("parallel"),
        name="mix",
    )(rest, rest, rest, rest, rest, band_bias, rec, *([rest] * (2 * n_kv)), x2d, gate,
      w_lru, w_att, w_out)


def _mlp_kernel(x_ref, sh_ref, sc_ref, gate_ref, g2_ref, gf_ref, w1_ref, w2_ref, o_ref,
                h_ref, ff_ref, *, nf):
    f = pl.program_id(1)
    rc = x_ref.shape[0] // ROW_CHUNKS

    def up(rows, slot):
        ff = jnp.dot(h_ref[rows, :], w1_ref[...], preferred_element_type=F32)
        ff_ref[slot, rows, :] = jnp.square(jnp.maximum(ff, 0.0)).astype(BF16)

    def down(rows, slot):
        return jnp.dot(ff_ref[slot, rows, :], w2_ref[...], preferred_element_type=F32)

    @pl.when(f == 0)
    def _():
        o_ref[...] = jnp.zeros_like(o_ref)
        for r in range(ROW_CHUNKS):
            rows = pl.ds(r * rc, rc)
            h_ref[rows, :] = _rms_mod(x_ref[rows, :], g2_ref[...], sh_ref[...],
                                      sc_ref[...]).astype(BF16)
            up(rows, 0)

    @pl.when((f > 0) & (f < nf))
    def _():
        o_ref[...] += down(slice(None), (f - 1) % 2)
        up(slice(None), f % 2)

    @pl.when(f == nf)
    def _():
        for r in range(ROW_CHUNKS):
            rows = pl.ds(r * rc, rc)
            x2 = x_ref[rows, :] + gate_ref[...] * (o_ref[rows, :] + down(rows, (nf - 1) % 2))
            var = jnp.mean(x2 * x2, axis=-1, keepdims=True)
            o_ref[rows, :] = x2 * lax.rsqrt(var + EPS) * gf_ref[...]


def _mlp(x1, shift, scale, gate, g2, gf, w1, w2, seq):
    m, d = x1.shape
    dff = w1.shape[1]
    tm = _tile(seq, 512, SUBLANES * ROW_CHUNKS)
    tf = _tile(dff, 1024, LANES)
    nf = dff // tf
    per_b = seq // tm
    row = lambda i, f: (i // per_b, 0, 0)
    const = lambda i, f: (0, 0)
    return pl.pallas_call(
        functools.partial(_mlp_kernel, nf=nf),
        grid=(m // tm, nf + 1),
        in_specs=[pl.BlockSpec((tm, d), lambda i, f: (i, 0)),
                  pl.BlockSpec((None, 1, d), row),
                  pl.BlockSpec((None, 1, d), row),
                  pl.BlockSpec((None, 1, d), row),
                  pl.BlockSpec((1, d), const),
                  pl.BlockSpec((1, d), const),
                  pl.BlockSpec((d, tf), lambda i, f: (0, jnp.minimum(f, nf - 1))),
                  pl.BlockSpec((tf, d), lambda i, f: (jnp.maximum(f - 1, 0), 0))],
        out_specs=pl.BlockSpec((tm, d), lambda i, f: (i, 0)),
        out_shape=jax.ShapeDtypeStruct((m, d), F32),
        scratch_shapes=[pltpu.VMEM((tm, d), BF16), pltpu.VMEM((2, tm, tf), BF16)],
        compiler_params=_params("parallel", "arbitrary"),
        name="mlp",
    )(x1, shift, scale, gate, g2, gf, w1, w2)


def kernel(x, c, w_ada, b_ada, norm1_g, w_in, conv_w, conv_b, lru_wa, lru_ba, lru_wx, lru_bx,
           lru_lambda, w_lru_out, w_attn_out, attn_sinks, rel_bias, w_out, norm2_g, w_ff1, w_ff2,
           final_g):
    bsz, seq, d = x.shape
    depth = w_in.shape[0]
    lru_w = w_lru_out.shape[1]
    attn_w = w_attn_out.shape[1]
    kv_w = (w_in.shape[2] - 2 * lru_w - attn_w - 2 * d) // 2
    n_heads = attn_sinks.shape[1]
    n_kv = kv_w // (attn_w // n_heads)
    q_off = 0
    k_off = attn_w
    v_off = k_off + kv_w
    ga_off = v_off + kv_w
    gb_off = ga_off + d

    xs = x.reshape(bsz * seq, d)
    for l in range(depth):
        band_bias = _band_bias(rel_bias, attn_sinks[l], n_kv)
        mod = _adaln(c, w_ada[l], b_ada[l])
        shift1, scale1, gate1, shift2, scale2, gate2 = [
            t.reshape(bsz, 1, d) for t in jnp.split(mod, 6, axis=-1)]

        wcat = jnp.concatenate([lru_wa[l], lru_wx[l]], axis=-1).astype(BF16)
        rest, rec = _proj(xs, shift1, scale1, norm1_g[l].reshape(1, d),
                          w_in[l, :, :2 * lru_w].astype(BF16), w_in[l, :, 2 * lru_w:].astype(BF16),
                          wcat, conv_w[l], conv_b[l], lru_ba[l], lru_bx[l], lru_lambda[l], seq)
        x1 = _mix(rest, rec, xs, gate1, band_bias, w_lru_out[l].astype(BF16),
                  w_attn_out[l].astype(BF16), w_out[l].astype(BF16), seq, q_off, k_off, v_off,
                  ga_off, gb_off, kv_w, n_heads)
        if l != depth - 1:
            raise NotImplementedError("the fused final norm assumes a single layer")
        xs = _mlp(x1, shift2, scale2, gate2, norm2_g[l].reshape(1, d), final_g.reshape(1, d),
                  w_ff1[l].astype(BF16), w_ff2[l].astype(BF16), seq)
    return xs.reshape(bsz, seq, d)
```

```python
import functools
import math

import jax
import jax.numpy as jnp
from jax import lax
from jax.experimental import pallas as pl
from jax.experimental.pallas import tpu as pltpu

EPS = 1e-6
LRU_C = 8.0
LOG2E = 1.4426950408889634
ATTN_BLOCK = 128
NEG_INF = -1e30
MAX_DISTANCE = 128
LANES = 128
SUBLANES = 8
MXU_WIDTH = 256
VMEM_LIMIT = 56 * 1024 * 1024
ROW_CHUNKS = 2
QUERY_BLOCKS = 2

F32 = jnp.float32
BF16 = jnp.bfloat16


def _tile(n, target, align):
    best = None
    t = align
    while t <= min(n, target):
        if n % t == 0:
            best = t
        t += align
    if best is None:
        raise ValueError(f"no tile for {n} (target {target}, align {align})")
    return best


def _params(*sem):
    return pltpu.CompilerParams(dimension_semantics=sem, vmem_limit_bytes=VMEM_LIMIT)


def _sigmoid(v):
    return 1.0 / (1.0 + jnp.exp2(v * -LOG2E))


def _rms_mod(x, g, shift, scale):
    var = jnp.mean(x * x, axis=-1, keepdims=True)
    y = x * lax.rsqrt(var + EPS) * g
    return y * (1.0 + scale) + shift


def _adaln_kernel(c_ref, w_ref, b_ref, o_ref):
    c = c_ref[...]
    act = (c * _sigmoid(c)).astype(BF16)
    o_ref[...] = jnp.dot(act, w_ref[...].astype(BF16), preferred_element_type=F32) + b_ref[...]


def _adaln(c, w, b):
    bsz, d = c.shape
    n = w.shape[1]
    tn = _tile(n, 1024, LANES)
    return pl.pallas_call(
        _adaln_kernel,
        grid=(n // tn,),
        in_specs=[pl.BlockSpec((bsz, d), lambda j: (0, 0)),
                  pl.BlockSpec((d, tn), lambda j: (0, j)),
                  pl.BlockSpec((1, tn), lambda j: (0, j))],
        out_specs=pl.BlockSpec((bsz, tn), lambda j: (0, j)),
        out_shape=jax.ShapeDtypeStruct((bsz, n), F32),
        compiler_params=_params("parallel"),
        name="adaln_mod",
    )(c, w, b.reshape(1, n))


def _gelu_tanh(v):
    k1 = -2.0 * math.sqrt(2.0 / math.pi) * LOG2E
    return v / (1.0 + jnp.exp2(v * (k1 + (k1 * 0.044715) * (v * v))))


def _rglru_tile(x, gate, seq_start, w_ref, cw_ref, cb_ref, ba_ref, bx_ref, lam_ref, o_ref,
                xtail, hcar, conv_k, other_mxu_work):
    tt, cw = x.shape
    groups = tt // SUBLANES

    assert conv_k - 1 < SUBLANES
    xs = jnp.concatenate([xtail[...], x], axis=0).reshape(groups + 1, SUBLANES, cw)
    xtail[...] = x[tt - SUBLANES:, :]
    first = lax.broadcasted_iota(jnp.int32, (groups + 1, SUBLANES, cw), 1) == 0
    xc = cb_ref[...] + cw_ref[conv_k - 1:conv_k, :] * x
    for k in range(conv_k - 2, -1, -1):
        rot = pltpu.roll(xs, 1, axis=1)
        xs = jnp.where(first, jnp.concatenate([rot[:1], rot[:-1]], axis=0), rot)
        xc = xc + cw_ref[k:k + 1, :] * xs[1:].reshape(tt, cw)

    lam = lam_ref[...]
    softplus_neg_lam = jnp.maximum(-lam, 0.0) + jnp.log(1.0 + jnp.exp(-jnp.abs(lam)))
    log2_a_coef = (-LRU_C * LOG2E) * softplus_neg_lam
    row0 = lax.broadcasted_iota(jnp.int32, (groups, SUBLANES, LANES), 1) == 0
    start_row = (lax.broadcasted_iota(jnp.int32, (SUBLANES, LANES), 0) == 0) & seq_start

    n_chunks = cw // LANES

    def gates(c):
        return jnp.dot(xc[:, c * LANES:(c + 1) * LANES].astype(BF16), w_ref[c],
                       preferred_element_type=F32)

    pieces = iter(other_mxu_work)

    def emit():
        piece = next(pieces, None)
        if piece is not None:
            piece()

    emit()
    zs = [gates(0)]
    for c in range(n_chunks):
        sl = slice(c * LANES, (c + 1) * LANES)
        if c + 1 < n_chunks:
            zs.append(gates(c + 1))
        emit()
        xh = xc[:, sl]
        z = zs[c]
        r = _sigmoid(z[:, :LANES] + ba_ref[:, sl])
        i = _sigmoid(z[:, LANES:] + bx_ref[:, sl])
        a = jnp.exp2(log2_a_coef[:, sl] * r)
        om = 1.0 - a * a
        mult = jnp.where(om > 0.0, om * lax.rsqrt(om), 0.0)
        mult = jnp.concatenate([jnp.where(start_row, 1.0, mult[:SUBLANES]), mult[SUBLANES:]],
                               axis=0)
        u3 = (mult * (i * xh)).reshape(groups, SUBLANES, LANES)
        a3 = a.reshape(groups, SUBLANES, LANES)

        az = jnp.where(row0, 0.0, a3)
        a_in = jnp.where(row0, a3, 0.0)
        az2 = az * pltpu.roll(az, 1, axis=1)
        az4 = az2 * pltpu.roll(az2, 2, axis=1)
        h_prev = hcar[0:1, sl]
        hs = []
        for gi in range(groups):
            if gi == groups // 2:
                emit()
            hg = u3[gi] + a_in[gi] * h_prev
            hg = hg + az[gi] * pltpu.roll(hg, 1, axis=0)
            hg = hg + az2[gi] * pltpu.roll(hg, 2, axis=0)
            hg = hg + az4[gi] * pltpu.roll(hg, 4, axis=0)
            hs.append(hg)
            h_prev = hg[SUBLANES - 1:SUBLANES, :]
        hcar[0:1, sl] = h_prev
        h = jnp.concatenate(hs, axis=0)
        o_ref[:, sl] = (h * _gelu_tanh(gate(sl))).astype(o_ref.dtype)
    for piece in pieces:
        piece()


def _proj_kernel(x_ref, sh_ref, sc_ref, g_ref, wx_ref, wg_ref, wr_ref, wcat_ref, cw_ref, cb_ref,
                 ba_ref, bx_ref, lam_ref, rest_ref, rec_ref, h_ref, lx_ref, gt_ref, xtail, hcar, *,
                 nl, per_b, conv_k):
    i = pl.program_id(0)
    j = pl.program_id(1)
    tm = x_ref.shape[0]
    rc = tm // ROW_CHUNKS
    cw = rec_ref.shape[1]
    per_step = wx_ref.shape[1] // cw

    def lru_cols(rows, h, base):
        yx = jnp.dot(h, wx_ref[...], preferred_element_type=F32).astype(BF16)
        yg = jnp.dot(h, wg_ref[...], preferred_element_type=F32).astype(BF16)
        for k in range(per_step):
            lx_ref[base + k, rows, :] = yx[:, k * cw:(k + 1) * cw]
            gt_ref[base + k, rows, :] = yg[:, k * cw:(k + 1) * cw]

    @pl.when(j == 0)
    def _():
        for r in range(ROW_CHUNKS):
            rows = pl.ds(r * rc, rc)
            h = _rms_mod(x_ref[rows, :], g_ref[...], sh_ref[...], sc_ref[...]).astype(BF16)
            h_ref[rows, :] = h
            lru_cols(rows, h, 0)

    @pl.when((j > 0) & (j < nl))
    def _():
        lru_cols(slice(None), h_ref[...], j * per_step)

    @pl.when(j >= nl)
    def _():
        c = j - nl
        seq_start = i % per_b == 0

        @pl.when(seq_start)
        def _():
            xtail[c] = jnp.zeros(xtail.shape[1:], F32)
            hcar[c] = jnp.zeros(hcar.shape[1:], F32)

        tr = rest_ref.shape[1]

        def rest_cols(lo):
            cols = slice(lo, min(lo + MXU_WIDTH, tr))
            rest_ref[:, cols] = jnp.dot(h_ref[...], wr_ref[:, cols],
                                        preferred_element_type=F32).astype(rest_ref.dtype)

        pieces = [functools.partial(rest_cols, lo) for lo in range(0, tr, MXU_WIDTH)]
        _rglru_tile(lx_ref[c].astype(F32), lambda sl: gt_ref[c, :, sl].astype(F32), seq_start,
                    wcat_ref, cw_ref, cb_ref, ba_ref, bx_ref, lam_ref, rec_ref,
                    xtail.at[c], hcar.at[c], conv_k, pieces)


def _proj(x2d, shift, scale, g, w_lru_cols, w_rest, wcat, conv_w, conv_b, ba, bx, lam, seq):
    m, d = x2d.shape
    lru_w = w_lru_cols.shape[1] // 2
    rest_w = w_rest.shape[1]
    conv_k = conv_w.shape[0]
    nr = max(n for n in (1, 2, 4) if lru_w % (n * LANES) == 0 and rest_w % (n * LANES) == 0)
    cw, tr = lru_w // nr, rest_w // nr
    tl = _tile(lru_w, 1024, cw)
    nl = lru_w // tl
    tm = _tile(seq, 512, SUBLANES * ROW_CHUNKS)
    per_b = seq // tm
    row = lambda i, j: (i // per_b, 0, 0)
    lstep = lambda j: jnp.minimum(j, nl - 1)
    chunk = lambda j: jnp.maximum(j - nl, 0)
    vec = pl.BlockSpec((1, cw), lambda i, j: (0, chunk(j)))
    kernel = functools.partial(_proj_kernel, nl=nl, per_b=per_b, conv_k=conv_k)
    return pl.pallas_call(
        kernel,
        grid=(m // tm, nl + nr),
        in_specs=[pl.BlockSpec((tm, d), lambda i, j: (i, 0)),
                  pl.BlockSpec((None, 1, d), row),
                  pl.BlockSpec((None, 1, d), row),
                  pl.BlockSpec((1, d), lambda i, j: (0, 0)),
                  pl.BlockSpec((d, tl), lambda i, j: (0, lstep(j))),
                  pl.BlockSpec((d, tl), lambda i, j: (0, nl + lstep(j))),
                  pl.BlockSpec((d, tr), lambda i, j: (0, chunk(j))),
                  pl.BlockSpec((cw // LANES, LANES, 2 * LANES), lambda i, j: (chunk(j), 0, 0)),
                  pl.BlockSpec((conv_k, cw), lambda i, j: (0, chunk(j))),
                  vec, vec, vec, vec],
        out_specs=[pl.BlockSpec((tm, tr), lambda i, j: (i, chunk(j))),
                   pl.BlockSpec((tm, cw), lambda i, j: (i, chunk(j)))],
        out_shape=[jax.ShapeDtypeStruct((m, rest_w), BF16),
                   jax.ShapeDtypeStruct((m, lru_w), BF16)],
        scratch_shapes=[pltpu.VMEM((tm, d), BF16),
                        pltpu.VMEM((nr, tm, cw), BF16), pltpu.VMEM((nr, tm, cw), BF16),
                        pltpu.VMEM((nr, SUBLANES, cw), F32), pltpu.VMEM((nr, SUBLANES, cw), F32)],
        compiler_params=_params("arbitrary", "arbitrary"),
        name="proj",
    )(x2d, shift, scale, g, w_lru_cols, w_lru_cols, w_rest, wcat, conv_w, conv_b.reshape(1, -1),
      ba.reshape(1, -1), bx.reshape(1, -1), lam.reshape(1, -1))


def _t5_bucket(rel, n_buckets):
    max_exact = n_buckets // 2
    relf = jnp.maximum(rel, 1).astype(F32)
    large = max_exact + (jnp.log(relf / max_exact) / math.log(MAX_DISTANCE / max_exact)
                         * (n_buckets - max_exact)).astype(jnp.int32)
    large = jnp.minimum(large, n_buckets - 1)
    return jnp.where(rel < max_exact, rel, large)


def _band_bias_kernel(rb_ref, sink_ref, bucket_ref, o_ref, *, pairs, n_buckets):
    kv = pl.program_id(0)
    blk = ATTN_BLOCK
    bucket = bucket_ref[...]
    qi = lax.broadcasted_iota(jnp.int32, (blk, 2 * blk), 0)
    ki = lax.broadcasted_iota(jnp.int32, (blk, 2 * blk), 1)
    rel = qi + blk - ki
    valid = (rel >= 0) & (rel < blk)
    for p in range(pairs):
        for par in range(2):
            h = (kv * pairs + p) * 2 + par
            bias = jnp.zeros((blk, 2 * blk), F32)
            for b in range(n_buckets):
                bias = jnp.where(bucket == b, rb_ref[b, h], bias)
            sink = sink_ref[h]
            rows, cols = slice(p * blk, (p + 1) * blk), slice(par * 2 * blk, (par + 1) * 2 * blk)
            o_ref[rows, cols] = jnp.where(ki == 0, sink, jnp.where(valid, bias, NEG_INF))


def _band_bias(rel_bias, sinks, n_kv):
    n_buckets, n_heads = rel_bias.shape
    pairs = n_heads // n_kv // 2
    blk = ATTN_BLOCK
    qi = jnp.arange(blk)[:, None]
    ki = jnp.arange(2 * blk)[None, :]
    bucket = _t5_bucket(jnp.maximum(qi + blk - ki, 0), n_buckets)
    smem = pl.BlockSpec(memory_space=pltpu.SMEM)
    return pl.pallas_call(
        functools.partial(_band_bias_kernel, pairs=pairs, n_buckets=n_buckets),
        grid=(n_kv,),
        in_specs=[smem, smem, pl.BlockSpec((blk, 2 * blk), lambda kv: (0, 0))],
        out_specs=pl.BlockSpec((None, pairs * blk, 4 * blk), lambda kv: (kv, 0, 0)),
        out_shape=jax.ShapeDtypeStruct((n_kv, pairs * blk, 4 * blk), F32),
        compiler_params=_params("parallel"),
        name="band_bias",
    )(rel_bias, sinks, bucket)


def _swap_lane_halves(v):
    u = pltpu.bitcast(v, jnp.uint32)
    return pltpu.bitcast(pltpu.roll(u, LANES // 2, axis=1), BF16)


def _mix_kernel(*refs, n_kv, pairs, scale, per_b):
    q_ref, kp_ref, kc_ref, vp_ref, vc_ref, bm_ref, rec_ref = refs[:7]
    ga_refs, gb_refs = refs[7:7 + n_kv], refs[7 + n_kv:7 + 2 * n_kv]
    x_ref, gate_ref, wl_ref, wa_ref, wo_ref, o_ref, att_ref, ya_ref, m_ref = refs[7 + 2 * n_kv:]
    blk = ATTN_BLOCK
    d = o_ref.shape[1]
    cc = d // n_kv
    fold = math.log2(scale).is_integer()
    lane = lax.broadcasted_iota(jnp.int32, (blk, LANES), 1)
    key = lax.broadcasted_iota(jnp.int32, (blk, LANES), 0)
    low = jnp.where(lane < LANES // 2, 1.0, 0.0).astype(BF16)
    high = jnp.where(lane < LANES // 2, 0.0, 1.0).astype(BF16)
    not_key0 = jnp.where(key == 0, 0.0, 1.0).astype(BF16)
    ones_blk = jnp.concatenate([low, low, high, high], axis=0)
    key_col = lax.broadcasted_iota(jnp.int32, (1, 4 * blk), 1) % (2 * blk)
    seq_start = pl.program_id(0) % per_b == 0
    start_mask = jnp.where(seq_start & (key_col >= 1) & (key_col < blk), NEG_INF, 0.0)

    def halves(t, kv):
        s = _swap_lane_halves(t)
        lo, hi = (t, s) if kv % 2 == 0 else (s, t)
        return lo * low, hi * high

    def window(prev, cur):
        return jnp.concatenate([prev[0] * not_key0, cur[0], prev[1] * not_key0, cur[1]], axis=0)

    for kv in range(n_kv):
        sl = slice((kv // 2) * LANES, (kv // 2 + 1) * LANES)
        kb = [halves(kp_ref[:, sl], kv)] + [halves(kc_ref[t * blk:(t + 1) * blk, sl], kv)
                                            for t in range(QUERY_BLOCKS)]
        vb = [halves(vp_ref[:, sl], kv)] + [halves(vc_ref[t * blk:(t + 1) * blk, sl], kv)
                                            for t in range(QUERY_BLOCKS)]
        for t in range(QUERY_BLOCKS):
            rows = slice(t * blk, (t + 1) * blk)
            kblk = window(kb[t], kb[t + 1])
            vaug = jnp.concatenate([window(vb[t], vb[t + 1]), ones_blk], axis=1)
            q = jnp.concatenate(
                [q_ref[rows, (kv * pairs + p) * LANES:(kv * pairs + p + 1) * LANES]
                 for p in range(pairs)], axis=0)
            if fold:
                q = q * scale
            s = lax.dot_general(q, kblk, (((1,), (1,)), ((), ())), preferred_element_type=F32)
            if not fold:
                s = s * scale
            s = s + bm_ref[kv]
            if t == 0:
                s = s + start_mask
            es = []
            for par in range(2):
                sp = s[:, par * 2 * blk:(par + 1) * 2 * blk]
                es.append(jnp.exp(sp - jnp.max(sp, axis=-1, keepdims=True)).astype(BF16))
            o = jnp.dot(jnp.concatenate(es, axis=1), vaug, preferred_element_type=F32)
            res = (o[:, :LANES] / o[:, LANES:]).astype(BF16)
            for p in range(pairs):
                att_ref[rows, (kv * pairs + p) * LANES:(kv * pairs + p + 1) * LANES] = (
                    res[p * blk:(p + 1) * blk])
        cols = slice(kv * cc, (kv + 1) * cc)
        ya = jnp.dot(rec_ref[...], wl_ref[:, cols], preferred_element_type=F32)
        ya_ref[:, cols] = _sigmoid(ga_refs[kv][...].astype(F32)) * ya

    for c in range(n_kv):
        cols = slice(c * cc, (c + 1) * cc)
        yb = jnp.dot(att_ref[...], wa_ref[:, cols], preferred_element_type=F32)
        gb = _sigmoid(gb_refs[c][...].astype(F32))
        m_ref[:, cols] = (ya_ref[:, cols] + gb * yb).astype(BF16)
    for c in range(n_kv):
        cols = slice(c * cc, (c + 1) * cc)
        y = jnp.dot(m_ref[...], wo_ref[:, cols], preferred_element_type=F32)
        o_ref[:, cols] = x_ref[:, cols] + gate_ref[:, cols] * y


def _mix(rest, rec, x2d, gate, band_bias, w_lru, w_att, w_out, seq, q_off, k_off, v_off, ga_off,
         gb_off, kv_w, n_heads):
    m, d = x2d.shape
    attn_w = w_att.shape[0]
    hd = attn_w // n_heads
    n_kv = kv_w // hd
    pairs = n_heads // n_kv // 2
    blk = ATTN_BLOCK
    tm = QUERY_BLOCKS * blk
    per_b = seq // tm
    cc = d // n_kv
    assert 2 * hd == LANES and n_heads == 2 * pairs * n_kv and kv_w % LANES == 0
    assert q_off % attn_w == 0 and k_off % kv_w == 0 and v_off % kv_w == 0
    assert seq % tm == 0 and cc % LANES == 0 and ga_off % cc == 0 and gb_off % cc == 0
    kb, vb = k_off // kv_w, v_off // kv_w
    prev = lambda col: (lambda i: (jnp.maximum(QUERY_BLOCKS * i - 1, 0), col))
    col_chunk = lambda off: [pl.BlockSpec((tm, cc), functools.partial(lambda c, i: (i, c),
                                                                      off // cc + c))
                             for c in range(n_kv)]
    resident = lambda a: pl.BlockSpec(a.shape, lambda i: (0,) * a.ndim,
                                      pipeline_mode=pl.Buffered(1))
    kernel = functools.partial(_mix_kernel, n_kv=n_kv, pairs=pairs, scale=hd ** -0.5, per_b=per_b)
    return pl.pallas_call(
        kernel,
        grid=(m // tm,),
        in_specs=[pl.BlockSpec((tm, attn_w), lambda i: (i, q_off // attn_w)),
                  pl.BlockSpec((blk, kv_w), prev(kb)),
                  pl.BlockSpec((tm, kv_w), lambda i: (i, kb)),
                  pl.BlockSpec((blk, kv_w), prev(vb)),
                  pl.BlockSpec((tm, kv_w), lambda i: (i, vb)),
                  resident(band_bias),
                  pl.BlockSpec((tm, rec.shape[1]), lambda i: (i, 0))]
                 + col_chunk(ga_off) + col_chunk(gb_off)
                 + [pl.BlockSpec((tm, d), lambda i: (i, 0)),
                    pl.BlockSpec((None, 1, d), lambda i: (i // per_b, 0, 0)),
                    resident(w_lru), resident(w_att), resident(w_out)],
        out_specs=pl.BlockSpec((tm, d), lambda i: (i, 0)),
        out_shape=jax.ShapeDtypeStruct((m, d), F32),
        scratch_shapes=[pltpu.VMEM((tm, attn_w), BF16), pltpu.VMEM((tm, d), F32),
                        pltpu.VMEM((tm, d), BF16)],
        compiler_params=_params("parallel"),
        name="mix",
    )(rest, rest, rest, rest, rest, band_bias, rec, *([rest] * (2 * n_kv)), x2d, gate,
      w_lru, w_att, w_out)


def _mlp_kernel(x_ref, sh_ref, sc_ref, gate_ref, g2_ref, gf_ref, w1_hbm, w2_hbm, o_ref,
                h_ref, ff_ref, w1_buf, w2_buf, sems, *, nf):
    i = pl.program_id(0)
    last_tile = pl.num_programs(0) - 1
    tm = x_ref.shape[0]
    tf = w1_buf.shape[2]
    rc = tm // ROW_CHUNKS

    def w1_copy(f):
        return pltpu.make_async_copy(w1_hbm.at[:, pl.ds(f * tf, tf)], w1_buf.at[f % 2],
                                     sems.at[0, f % 2])

    def w2_copy(f):
        return pltpu.make_async_copy(w2_hbm.at[pl.ds(f * tf, tf), :], w2_buf.at[f % 2],
                                     sems.at[1, f % 2])

    def up(rows, f):
        ff = jnp.dot(h_ref[rows, :], w1_buf[f % 2], preferred_element_type=F32)
        ff_ref[f % 2, rows, :] = jnp.square(jnp.maximum(ff, 0.0)).astype(BF16)

    def down(rows, f):
        return jnp.dot(ff_ref[f % 2, rows, :], w2_buf[f % 2], preferred_element_type=F32)

    @pl.when(i == 0)
    def _():
        w1_copy(0).start()

    for f in range(nf + 1):
        if f < nf:
            w1_copy(f).wait()
        if f >= 1:
            w2_copy(f - 1).wait()
        if f + 1 < nf:
            w1_copy(f + 1).start()
        if f < nf:
            w2_copy(f).start()
        if f == nf:
            w1_copy(0).start()

        if f == 0:
            for r in range(ROW_CHUNKS):
                rows = pl.ds(r * rc, rc)
                h_ref[rows, :] = _rms_mod(x_ref[rows, :], g2_ref[...], sh_ref[...],
                                          sc_ref[...]).astype(BF16)
                up(rows, 0)
        elif f < nf:
            part = down(slice(None), f - 1)
            if f == 1:
                o_ref[...] = part
            else:
                o_ref[...] += part
            up(slice(None), f)
        else:
            for r in range(ROW_CHUNKS):
                rows = pl.ds(r * rc, rc)
                acc = down(rows, nf - 1) if nf == 1 else o_ref[rows, :] + down(rows, nf - 1)
                x2 = x_ref[rows, :] + gate_ref[...] * acc
                var = jnp.mean(x2 * x2, axis=-1, keepdims=True)
                o_ref[rows, :] = x2 * lax.rsqrt(var + EPS) * gf_ref[...]

    @pl.when(i == last_tile)
    def _():
        w1_copy(0).wait()


def _mlp(x1, shift, scale, gate, g2, gf, w1, w2, seq):
    m, d = x1.shape
    dff = w1.shape[1]
    tm = _tile(seq, 512, SUBLANES * ROW_CHUNKS)
    tf = _tile(dff // 2, 1024, LANES)
    nf = dff // tf
    assert nf % 2 == 0
    per_b = seq // tm
    row = lambda i: (i // per_b, 0, 0)
    const = lambda i: (0, 0)
    hbm = pl.BlockSpec(memory_space=pl.ANY)
    return pl.pallas_call(
        functools.partial(_mlp_kernel, nf=nf),
        grid=(m // tm,),
        in_specs=[pl.BlockSpec((tm, d), lambda i: (i, 0)),
                  pl.BlockSpec((None, 1, d), row),
                  pl.BlockSpec((None, 1, d), row),
                  pl.BlockSpec((None, 1, d), row),
                  pl.BlockSpec((1, d), const),
                  pl.BlockSpec((1, d), const),
                  hbm, hbm],
        out_specs=pl.BlockSpec((tm, d), lambda i: (i, 0)),
        out_shape=jax.ShapeDtypeStruct((m, d), F32),
        scratch_shapes=[pltpu.VMEM((tm, d), BF16), pltpu.VMEM((2, tm, tf), BF16),
                        pltpu.VMEM((2, d, tf), BF16), pltpu.VMEM((2, tf, d), BF16),
                        pltpu.SemaphoreType.DMA((2, 2))],
        compiler_params=_params("arbitrary"),
        name="mlp",
    )(x1, shift, scale, gate, g2, gf, w1, w2)


def kernel(x, c, w_ada, b_ada, norm1_g, w_in, conv_w, conv_b, lru_wa, lru_ba, lru_wx, lru_bx,
           lru_lambda, w_lru_out, w_attn_out, attn_sinks, rel_bias, w_out, norm2_g, w_ff1, w_ff2,
           final_g):
    bsz, seq, d = x.shape
    depth = w_in.shape[0]
    lru_w = w_lru_out.shape[1]
    attn_w = w_attn_out.shape[1]
    kv_w = (w_in.shape[2] - 2 * lru_w - attn_w - 2 * d) // 2
    n_heads = attn_sinks.shape[1]
    n_kv = kv_w // (attn_w // n_heads)
    q_off = 0
    k_off = attn_w
    v_off = k_off + kv_w
    ga_off = v_off + kv_w
    gb_off = ga_off + d

    xs = x.reshape(bsz * seq, d)
    for l in range(depth):
        band_bias = _band_bias(rel_bias, attn_sinks[l], n_kv)
        mod = _adaln(c, w_ada[l], b_ada[l])
        shift1, scale1, gate1, shift2, scale2, gate2 = [
            t.reshape(bsz, 1, d) for t in jnp.split(mod, 6, axis=-1)]

        wcat = jnp.concatenate([lru_wa[l], lru_wx[l]], axis=-1).astype(BF16)
        rest, rec = _proj(xs, shift1, scale1, norm1_g[l].reshape(1, d),
                          w_in[l, :, :2 * lru_w].astype(BF16), w_in[l, :, 2 * lru_w:].astype(BF16),
                          wcat, conv_w[l], conv_b[l], lru_ba[l], lru_bx[l], lru_lambda[l], seq)
        x1 = _mix(rest, rec, xs, gate1, band_bias, w_lru_out[l].astype(BF16),
                  w_attn_out[l].astype(BF16), w_out[l].astype(BF16), seq, q_off, k_off, v_off,
                  ga_off, gb_off, kv_w, n_heads)
        if l != depth - 1:
            raise NotImplementedError("the fused final norm assumes a single layer")
        xs = _mlp(x1, shift2, scale2, gate2, norm2_g[l].reshape(1, d), final_g.reshape(1, d),
                  w_ff1[l].astype(BF16), w_ff2[l].astype(BF16), seq)
    return xs.reshape(bsz, seq, d)
```

```python
import functools
import math

import jax
import jax.numpy as jnp
from jax import lax
from jax.experimental import pallas as pl
from jax.experimental.pallas import tpu as pltpu

EPS = 1e-6
LRU_C = 8.0
LOG2E = 1.4426950408889634
ATTN_BLOCK = 128
NEG_INF = -1e30
MAX_DISTANCE = 128
LANES = 128
SUBLANES = 8
MXU_WIDTH = 256
VMEM_LIMIT = 56 * 1024 * 1024
ROW_CHUNKS = 2
COL_CHUNKS = 4
QUERY_BLOCKS = 2

F32 = jnp.float32
BF16 = jnp.bfloat16


def _tile(n, target, align):
    best = None
    t = align
    while t <= min(n, target):
        if n % t == 0:
            best = t
        t += align
    if best is None:
        raise ValueError(f"no tile for {n} (target {target}, align {align})")
    return best


def _params(*sem):
    return pltpu.CompilerParams(dimension_semantics=sem, vmem_limit_bytes=VMEM_LIMIT)


def _sigmoid(v):
    return 1.0 / (1.0 + jnp.exp2(v * -LOG2E))


def _rms_mod(x, g, shift, scale):
    var = jnp.mean(x * x, axis=-1, keepdims=True)
    y = x * lax.rsqrt(var + EPS) * g
    return y * (1.0 + scale) + shift


def _adaln_kernel(c_ref, w_ref, b_ref, o_ref):
    c = c_ref[...]
    act = (c * _sigmoid(c)).astype(BF16)
    o_ref[...] = jnp.dot(act, w_ref[...].astype(BF16), preferred_element_type=F32) + b_ref[...]


def _adaln(c, w, b):
    bsz, d = c.shape
    n = w.shape[1]
    tn = _tile(n, 1024, LANES)
    return pl.pallas_call(
        _adaln_kernel,
        grid=(n // tn,),
        in_specs=[pl.BlockSpec((bsz, d), lambda j: (0, 0)),
                  pl.BlockSpec((d, tn), lambda j: (0, j)),
                  pl.BlockSpec((1, tn), lambda j: (0, j))],
        out_specs=pl.BlockSpec((bsz, tn), lambda j: (0, j)),
        out_shape=jax.ShapeDtypeStruct((bsz, n), F32),
        compiler_params=_params("parallel"),
        name="adaln_mod",
    )(c, w, b.reshape(1, n))


def _gelu_tanh(v):
    k1 = -2.0 * math.sqrt(2.0 / math.pi) * LOG2E
    return v / (1.0 + jnp.exp2(v * (k1 + (k1 * 0.044715) * (v * v))))


def _rglru_tile(x, gate, seq_start, w_ref, cw_ref, cb_ref, ba_ref, bx_ref, lam_ref, o_ref,
                xtail, hcar, conv_k, other_mxu_work):
    tt, cw = x.shape
    groups = tt // SUBLANES

    assert conv_k - 1 < SUBLANES
    xs = jnp.concatenate([xtail[...], x], axis=0).reshape(groups + 1, SUBLANES, cw)
    xtail[...] = x[tt - SUBLANES:, :]
    first = lax.broadcasted_iota(jnp.int32, (groups + 1, SUBLANES, cw), 1) == 0
    xc = cb_ref[...] + cw_ref[conv_k - 1:conv_k, :] * x
    for k in range(conv_k - 2, -1, -1):
        rot = pltpu.roll(xs, 1, axis=1)
        xs = jnp.where(first, jnp.concatenate([rot[:1], rot[:-1]], axis=0), rot)
        xc = xc + cw_ref[k:k + 1, :] * xs[1:].reshape(tt, cw)

    lam = lam_ref[...]
    softplus_neg_lam = jnp.maximum(-lam, 0.0) + jnp.log(1.0 + jnp.exp(-jnp.abs(lam)))
    log2_a_coef = (-LRU_C * LOG2E) * softplus_neg_lam
    row0 = lax.broadcasted_iota(jnp.int32, (groups, SUBLANES, LANES), 1) == 0
    start_row = (lax.broadcasted_iota(jnp.int32, (SUBLANES, LANES), 0) == 0) & seq_start

    n_chunks = cw // LANES

    def gates(c):
        return jnp.dot(xc[:, c * LANES:(c + 1) * LANES].astype(BF16), w_ref[c],
                       preferred_element_type=F32)

    pieces = iter(other_mxu_work)

    def emit():
        piece = next(pieces, None)
        if piece is not None:
            piece()

    emit()
    zs = [gates(0)]
    for c in range(n_chunks):
        sl = slice(c * LANES, (c + 1) * LANES)
        if c + 1 < n_chunks:
            zs.append(gates(c + 1))
        emit()
        xh = xc[:, sl]
        z = zs[c]
        r = _sigmoid(z[:, :LANES] + ba_ref[:, sl])
        i = _sigmoid(z[:, LANES:] + bx_ref[:, sl])
        a = jnp.exp2(log2_a_coef[:, sl] * r)
        om = 1.0 - a * a
        mult = jnp.where(om > 0.0, om * lax.rsqrt(om), 0.0)
        mult = jnp.concatenate([jnp.where(start_row, 1.0, mult[:SUBLANES]), mult[SUBLANES:]],
                               axis=0)
        u3 = (mult * (i * xh)).reshape(groups, SUBLANES, LANES)
        a3 = a.reshape(groups, SUBLANES, LANES)

        az = jnp.where(row0, 0.0, a3)
        a_in = jnp.where(row0, a3, 0.0)
        az2 = az * pltpu.roll(az, 1, axis=1)
        az4 = az2 * pltpu.roll(az2, 2, axis=1)
        h_prev = hcar[0:1, sl]
        hs = []
        for gi in range(groups):
            if gi == groups // 2:
                emit()
            hg = u3[gi] + a_in[gi] * h_prev
            hg = hg + az[gi] * pltpu.roll(hg, 1, axis=0)
            hg = hg + az2[gi] * pltpu.roll(hg, 2, axis=0)
            hg = hg + az4[gi] * pltpu.roll(hg, 4, axis=0)
            hs.append(hg)
            h_prev = hg[SUBLANES - 1:SUBLANES, :]
        hcar[0:1, sl] = h_prev
        h = jnp.concatenate(hs, axis=0)
        o_ref[:, sl] = (h * _gelu_tanh(gate(sl))).astype(o_ref.dtype)
    for piece in pieces:
        piece()


def _proj_kernel(x_ref, sh_ref, sc_ref, g_ref, wx_ref, wg_ref, wr_ref, wcat_ref, cw_ref, cb_ref,
                 ba_ref, bx_ref, lam_ref, rest_ref, rec_ref, h_ref, lx_ref, gt_ref, xtail, hcar, *,
                 nl, per_b, conv_k):
    i = pl.program_id(0)
    j = pl.program_id(1)
    tm = x_ref.shape[0]
    rc = tm // ROW_CHUNKS
    cw = rec_ref.shape[1]
    per_step = wx_ref.shape[1] // cw

    def lru_cols(rows, h, base):
        yx = jnp.dot(h, wx_ref[...], preferred_element_type=F32).astype(BF16)
        yg = jnp.dot(h, wg_ref[...], preferred_element_type=F32).astype(BF16)
        for k in range(per_step):
            lx_ref[base + k, rows, :] = yx[:, k * cw:(k + 1) * cw]
            gt_ref[base + k, rows, :] = yg[:, k * cw:(k + 1) * cw]

    @pl.when(j == 0)
    def _():
        for r in range(ROW_CHUNKS):
            rows = pl.ds(r * rc, rc)
            h = _rms_mod(x_ref[rows, :], g_ref[...], sh_ref[...], sc_ref[...]).astype(BF16)
            h_ref[rows, :] = h
            lru_cols(rows, h, 0)

    @pl.when((j > 0) & (j < nl))
    def _():
        lru_cols(slice(None), h_ref[...], j * per_step)

    @pl.when(j >= nl)
    def _():
        c = j - nl
        seq_start = i % per_b == 0

        @pl.when(seq_start)
        def _():
            xtail[c] = jnp.zeros(xtail.shape[1:], F32)
            hcar[c] = jnp.zeros(hcar.shape[1:], F32)

        tr = rest_ref.shape[1]

        def rest_cols(lo):
            cols = slice(lo, min(lo + MXU_WIDTH, tr))
            rest_ref[:, cols] = jnp.dot(h_ref[...], wr_ref[:, cols],
                                        preferred_element_type=F32).astype(rest_ref.dtype)

        pieces = [functools.partial(rest_cols, lo) for lo in range(0, tr, MXU_WIDTH)]
        _rglru_tile(lx_ref[c].astype(F32), lambda sl: gt_ref[c, :, sl].astype(F32), seq_start,
                    wcat_ref, cw_ref, cb_ref, ba_ref, bx_ref, lam_ref, rec_ref,
                    xtail.at[c], hcar.at[c], conv_k, pieces)


def _proj(x2d, shift, scale, g, w_lru_cols, w_rest, wcat, conv_w, conv_b, ba, bx, lam, seq):
    m, d = x2d.shape
    lru_w = w_lru_cols.shape[1] // 2
    rest_w = w_rest.shape[1]
    conv_k = conv_w.shape[0]
    nr = max(n for n in (1, 2, 4) if lru_w % (n * LANES) == 0 and rest_w % (n * LANES) == 0)
    cw, tr = lru_w // nr, rest_w // nr
    tl = _tile(lru_w, 1024, cw)
    nl = lru_w // tl
    tm = _tile(seq, 512, SUBLANES * ROW_CHUNKS)
    per_b = seq // tm
    row = lambda i, j: (i // per_b, 0, 0)
    lstep = lambda j: jnp.minimum(j, nl - 1)
    chunk = lambda j: jnp.maximum(j - nl, 0)
    vec = pl.BlockSpec((1, cw), lambda i, j: (0, chunk(j)))
    kernel = functools.partial(_proj_kernel, nl=nl, per_b=per_b, conv_k=conv_k)
    return pl.pallas_call(
        kernel,
        grid=(m // tm, nl + nr),
        in_specs=[pl.BlockSpec((tm, d), lambda i, j: (i, 0)),
                  pl.BlockSpec((None, 1, d), row),
                  pl.BlockSpec((None, 1, d), row),
                  pl.BlockSpec((1, d), lambda i, j: (0, 0)),
                  pl.BlockSpec((d, tl), lambda i, j: (0, lstep(j))),
                  pl.BlockSpec((d, tl), lambda i, j: (0, nl + lstep(j))),
                  pl.BlockSpec((d, tr), lambda i, j: (0, chunk(j))),
                  pl.BlockSpec((cw // LANES, LANES, 2 * LANES), lambda i, j: (chunk(j), 0, 0)),
                  pl.BlockSpec((conv_k, cw), lambda i, j: (0, chunk(j))),
                  vec, vec, vec, vec],
        out_specs=[pl.BlockSpec((tm, tr), lambda i, j: (i, chunk(j))),
                   pl.BlockSpec((tm, cw), lambda i, j: (i, chunk(j)))],
        out_shape=[jax.ShapeDtypeStruct((m, rest_w), BF16),
                   jax.ShapeDtypeStruct((m, lru_w), BF16)],
        scratch_shapes=[pltpu.VMEM((tm, d), BF16),
                        pltpu.VMEM((nr, tm, cw), BF16), pltpu.VMEM((nr, tm, cw), BF16),
                        pltpu.VMEM((nr, SUBLANES, cw), F32), pltpu.VMEM((nr, SUBLANES, cw), F32)],
        compiler_params=_params("arbitrary", "arbitrary"),
        name="proj",
    )(x2d, shift, scale, g, w_lru_cols, w_lru_cols, w_rest, wcat, conv_w, conv_b.reshape(1, -1),
      ba.reshape(1, -1), bx.reshape(1, -1), lam.reshape(1, -1))


def _t5_bucket(rel, n_buckets):
    max_exact = n_buckets // 2
    relf = jnp.maximum(rel, 1).astype(F32)
    large = max_exact + (jnp.log(relf / max_exact) / math.log(MAX_DISTANCE / max_exact)
                         * (n_buckets - max_exact)).astype(jnp.int32)
    large = jnp.minimum(large, n_buckets - 1)
    return jnp.where(rel < max_exact, rel, large)


def _band_bias_kernel(rb_ref, sink_ref, bucket_ref, o_ref, *, pairs, n_buckets):
    kv = pl.program_id(0)
    blk = ATTN_BLOCK
    bucket = bucket_ref[...]
    qi = lax.broadcasted_iota(jnp.int32, (blk, 2 * blk), 0)
    ki = lax.broadcasted_iota(jnp.int32, (blk, 2 * blk), 1)
    rel = qi + blk - ki
    valid = (rel >= 0) & (rel < blk)
    for p in range(pairs):
        for par in range(2):
            h = (kv * pairs + p) * 2 + par
            bias = jnp.zeros((blk, 2 * blk), F32)
            for b in range(n_buckets):
                bias = jnp.where(bucket == b, rb_ref[b, h], bias)
            sink = sink_ref[h]
            rows, cols = slice(p * blk, (p + 1) * blk), slice(par * 2 * blk, (par + 1) * 2 * blk)
            o_ref[rows, cols] = jnp.where(ki == 0, sink, jnp.where(valid, bias, NEG_INF))


def _band_bias(rel_bias, sinks, n_kv):
    n_buckets, n_heads = rel_bias.shape
    pairs = n_heads // n_kv // 2
    blk = ATTN_BLOCK
    qi = jnp.arange(blk)[:, None]
    ki = jnp.arange(2 * blk)[None, :]
    bucket = _t5_bucket(jnp.maximum(qi + blk - ki, 0), n_buckets)
    smem = pl.BlockSpec(memory_space=pltpu.SMEM)
    return pl.pallas_call(
        functools.partial(_band_bias_kernel, pairs=pairs, n_buckets=n_buckets),
        grid=(n_kv,),
        in_specs=[smem, smem, pl.BlockSpec((blk, 2 * blk), lambda kv: (0, 0))],
        out_specs=pl.BlockSpec((None, pairs * blk, 4 * blk), lambda kv: (kv, 0, 0)),
        out_shape=jax.ShapeDtypeStruct((n_kv, pairs * blk, 4 * blk), F32),
        compiler_params=_params("parallel"),
        name="band_bias",
    )(rel_bias, sinks, bucket)


def _swap_lane_halves(v):
    u = pltpu.bitcast(v, jnp.uint32)
    return pltpu.bitcast(pltpu.roll(u, LANES // 2, axis=1), BF16)


def _mix_kernel(*refs, n_kv, pairs, scale, per_b):
    q_ref, kp_ref, kc_ref, vp_ref, vc_ref, bm_ref, rec_ref = refs[:7]
    ga_refs, gb_refs = refs[7:7 + n_kv], refs[7 + n_kv:7 + 2 * n_kv]
    x_ref, gate_ref, wl_ref, wa_ref, wo_ref, o_ref, att_ref, ya_ref, m_ref = refs[7 + 2 * n_kv:]
    blk = ATTN_BLOCK
    d = o_ref.shape[1]
    cc = d // n_kv
    fold = math.log2(scale).is_integer()
    lane = lax.broadcasted_iota(jnp.int32, (blk, LANES), 1)
    key = lax.broadcasted_iota(jnp.int32, (blk, LANES), 0)
    low = jnp.where(lane < LANES // 2, 1.0, 0.0).astype(BF16)
    high = jnp.where(lane < LANES // 2, 0.0, 1.0).astype(BF16)
    not_key0 = jnp.where(key == 0, 0.0, 1.0).astype(BF16)
    ones_blk = jnp.concatenate([low, low, high, high], axis=0)
    key_col = lax.broadcasted_iota(jnp.int32, (1, 4 * blk), 1) % (2 * blk)
    seq_start = pl.program_id(0) % per_b == 0
    start_mask = jnp.where(seq_start & (key_col >= 1) & (key_col < blk), NEG_INF, 0.0)

    def halves(t, kv):
        s = _swap_lane_halves(t)
        lo, hi = (t, s) if kv % 2 == 0 else (s, t)
        return lo * low, hi * high

    def window(prev, cur):
        return jnp.concatenate([prev[0] * not_key0, cur[0], prev[1] * not_key0, cur[1]], axis=0)

    for kv in range(n_kv):
        sl = slice((kv // 2) * LANES, (kv // 2 + 1) * LANES)
        kb = [halves(kp_ref[:, sl], kv)] + [halves(kc_ref[t * blk:(t + 1) * blk, sl], kv)
                                            for t in range(QUERY_BLOCKS)]
        vb = [halves(vp_ref[:, sl], kv)] + [halves(vc_ref[t * blk:(t + 1) * blk, sl], kv)
                                            for t in range(QUERY_BLOCKS)]
        for t in range(QUERY_BLOCKS):
            rows = slice(t * blk, (t + 1) * blk)
            kblk = window(kb[t], kb[t + 1])
            vaug = jnp.concatenate([window(vb[t], vb[t + 1]), ones_blk], axis=1)
            q = jnp.concatenate(
                [q_ref[rows, (kv * pairs + p) * LANES:(kv * pairs + p + 1) * LANES]
                 for p in range(pairs)], axis=0)
            if fold:
                q = q * scale
            s = lax.dot_general(q, kblk, (((1,), (1,)), ((), ())), preferred_element_type=F32)
            if not fold:
                s = s * scale
            s = s + bm_ref[kv]
            if t == 0:
                s = s + start_mask
            es = []
            for par in range(2):
                sp = s[:, par * 2 * blk:(par + 1) * 2 * blk]
                es.append(jnp.exp(sp - jnp.max(sp, axis=-1, keepdims=True)).astype(BF16))
            o = jnp.dot(jnp.concatenate(es, axis=1), vaug, preferred_element_type=F32)
            res = (o[:, :LANES] / o[:, LANES:]).astype(BF16)
            for p in range(pairs):
                att_ref[rows, (kv * pairs + p) * LANES:(kv * pairs + p + 1) * LANES] = (
                    res[p * blk:(p + 1) * blk])
        cols = slice(kv * cc, (kv + 1) * cc)
        ya = jnp.dot(rec_ref[...], wl_ref[:, cols], preferred_element_type=F32)
        ya_ref[:, cols] = _sigmoid(ga_refs[kv][...].astype(F32)) * ya

    for c in range(n_kv):
        cols = slice(c * cc, (c + 1) * cc)
        yb = jnp.dot(att_ref[...], wa_ref[:, cols], preferred_element_type=F32)
        gb = _sigmoid(gb_refs[c][...].astype(F32))
        m_ref[:, cols] = (ya_ref[:, cols] + gb * yb).astype(BF16)
    for c in range(n_kv):
        cols = slice(c * cc, (c + 1) * cc)
        y = jnp.dot(m_ref[...], wo_ref[:, cols], preferred_element_type=F32)
        o_ref[:, cols] = x_ref[:, cols] + gate_ref[:, cols] * y


def _mix(rest, rec, x2d, gate, band_bias, w_lru, w_att, w_out, seq, q_off, k_off, v_off, ga_off,
         gb_off, kv_w, n_heads):
    m, d = x2d.shape
    attn_w = w_att.shape[0]
    hd = attn_w // n_heads
    n_kv = kv_w // hd
    pairs = n_heads // n_kv // 2
    blk = ATTN_BLOCK
    tm = QUERY_BLOCKS * blk
    per_b = seq // tm
    cc = d // n_kv
    assert 2 * hd == LANES and n_heads == 2 * pairs * n_kv and kv_w % LANES == 0
    assert q_off % attn_w == 0 and k_off % kv_w == 0 and v_off % kv_w == 0
    assert seq % tm == 0 and cc % LANES == 0 and ga_off % cc == 0 and gb_off % cc == 0
    kb, vb = k_off // kv_w, v_off // kv_w
    prev = lambda col: (lambda i: (jnp.maximum(QUERY_BLOCKS * i - 1, 0), col))
    col_chunk = lambda off: [pl.BlockSpec((tm, cc), functools.partial(lambda c, i: (i, c),
                                                                      off // cc + c))
                             for c in range(n_kv)]
    resident = lambda a: pl.BlockSpec(a.shape, lambda i: (0,) * a.ndim,
                                      pipeline_mode=pl.Buffered(1))
    kernel = functools.partial(_mix_kernel, n_kv=n_kv, pairs=pairs, scale=hd ** -0.5, per_b=per_b)
    return pl.pallas_call(
        kernel,
        grid=(m // tm,),
        in_specs=[pl.BlockSpec((tm, attn_w), lambda i: (i, q_off // attn_w)),
                  pl.BlockSpec((blk, kv_w), prev(kb)),
                  pl.BlockSpec((tm, kv_w), lambda i: (i, kb)),
                  pl.BlockSpec((blk, kv_w), prev(vb)),
                  pl.BlockSpec((tm, kv_w), lambda i: (i, vb)),
                  resident(band_bias),
                  pl.BlockSpec((tm, rec.shape[1]), lambda i: (i, 0))]
                 + col_chunk(ga_off) + col_chunk(gb_off)
                 + [pl.BlockSpec((tm, d), lambda i: (i, 0)),
                    pl.BlockSpec((None, 1, d), lambda i: (i // per_b, 0, 0)),
                    resident(w_lru), resident(w_att), resident(w_out)],
        out_specs=pl.BlockSpec((tm, d), lambda i: (i, 0)),
        out_shape=jax.ShapeDtypeStruct((m, d), F32),
        scratch_shapes=[pltpu.VMEM((tm, attn_w), BF16), pltpu.VMEM((tm, d), F32),
                        pltpu.VMEM((tm, d), BF16)],
        compiler_params=_params("parallel"),
        name="mix",
    )(rest, rest, rest, rest, rest, band_bias, rec, *([rest] * (2 * n_kv)), x2d, gate,
      w_lru, w_att, w_out)


def _mlp_kernel(x_ref, sh_ref, sc_ref, gate_ref, g2_ref, gf_ref, w1_hbm, w2_hbm, o_ref,
                h_ref, ff_ref, w1_buf, w2_buf, sems, *, nf):
    i = pl.program_id(0)
    last_tile = pl.num_programs(0) - 1
    tm = x_ref.shape[0]
    tf = w1_buf.shape[2]
    rc = tm // ROW_CHUNKS

    def w1_copy(f):
        return pltpu.make_async_copy(w1_hbm.at[:, pl.ds(f * tf, tf)], w1_buf.at[f % 2],
                                     sems.at[0, f % 2])

    def w2_copy(f):
        return pltpu.make_async_copy(w2_hbm.at[pl.ds(f * tf, tf), :], w2_buf.at[f % 2],
                                     sems.at[1, f % 2])

    def up(rows, f):
        ff = jnp.dot(h_ref[rows, :], w1_buf[f % 2], preferred_element_type=F32)
        ff_ref[f % 2, rows, :] = jnp.square(jnp.maximum(ff, 0.0)).astype(BF16)

    def down(rows, f):
        return jnp.dot(ff_ref[f % 2, rows, :], w2_buf[f % 2], preferred_element_type=F32)

    @pl.when(i == 0)
    def _():
        w1_copy(0).start()

    for f in range(nf + 1):
        if f < nf:
            w1_copy(f).wait()
        if f >= 1:
            w2_copy(f - 1).wait()
        if f + 1 < nf:
            w1_copy(f + 1).start()
        if f < nf:
            w2_copy(f).start()
        if f == nf:
            w1_copy(0).start()

        if f == 0:
            for r in range(ROW_CHUNKS):
                rows = pl.ds(r * rc, rc)
                h_ref[rows, :] = _rms_mod(x_ref[rows, :], g2_ref[...], sh_ref[...],
                                          sc_ref[...]).astype(BF16)
                up(rows, 0)
        elif f < nf:
            d = o_ref.shape[1]
            cc = max(d // COL_CHUNKS, LANES)
            for c in range(0, d, cc):
                cols = slice(c, c + cc)
                part = jnp.dot(ff_ref[(f - 1) % 2], w2_buf[(f - 1) % 2, :, cols],
                               preferred_element_type=F32)
                if f == 1:
                    o_ref[:, cols] = part
                else:
                    o_ref[:, cols] += part
            up(slice(None), f)
        else:
            for r in range(ROW_CHUNKS):
                rows = pl.ds(r * rc, rc)
                acc = down(rows, nf - 1) if nf == 1 else o_ref[rows, :] + down(rows, nf - 1)
                x2 = x_ref[rows, :] + gate_ref[...] * acc
                var = jnp.mean(x2 * x2, axis=-1, keepdims=True)
                o_ref[rows, :] = x2 * lax.rsqrt(var + EPS) * gf_ref[...]

    @pl.when(i == last_tile)
    def _():
        w1_copy(0).wait()


def _mlp(x1, shift, scale, gate, g2, gf, w1, w2, seq):
    m, d = x1.shape
    dff = w1.shape[1]
    tm = _tile(seq, 1024, SUBLANES * ROW_CHUNKS)
    tf = _tile(dff // 2, 512, LANES)
    nf = dff // tf
    assert nf % 2 == 0
    per_b = seq // tm
    row = lambda i: (i // per_b, 0, 0)
    const = lambda i: (0, 0)
    hbm = pl.BlockSpec(memory_space=pl.ANY)
    return pl.pallas_call(
        functools.partial(_mlp_kernel, nf=nf),
        grid=(m // tm,),
        in_specs=[pl.BlockSpec((tm, d), lambda i: (i, 0)),
                  pl.BlockSpec((None, 1, d), row),
                  pl.BlockSpec((None, 1, d), row),
                  pl.BlockSpec((None, 1, d), row),
                  pl.BlockSpec((1, d), const),
                  pl.BlockSpec((1, d), const),
                  hbm, hbm],
        out_specs=pl.BlockSpec((tm, d), lambda i: (i, 0)),
        out_shape=jax.ShapeDtypeStruct((m, d), F32),
        scratch_shapes=[pltpu.VMEM((tm, d), BF16), pltpu.VMEM((2, tm, tf), BF16),
                        pltpu.VMEM((2, d, tf), BF16), pltpu.VMEM((2, tf, d), BF16),
                        pltpu.SemaphoreType.DMA((2, 2))],
        compiler_params=_params("arbitrary"),
        name="mlp",
    )(x1, shift, scale, gate, g2, gf, w1, w2)


def kernel(x, c, w_ada, b_ada, norm1_g, w_in, conv_w, conv_b, lru_wa, lru_ba, lru_wx, lru_bx,
           lru_lambda, w_lru_out, w_attn_out, attn_sinks, rel_bias, w_out, norm2_g, w_ff1, w_ff2,
           final_g):
    bsz, seq, d = x.shape
    depth = w_in.shape[0]
    lru_w = w_lru_out.shape[1]
    attn_w = w_attn_out.shape[1]
    kv_w = (w_in.shape[2] - 2 * lru_w - attn_w - 2 * d) // 2
    n_heads = attn_sinks.shape[1]
    n_kv = kv_w // (attn_w // n_heads)
    q_off = 0
    k_off = attn_w
    v_off = k_off + kv_w
    ga_off = v_off + kv_w
    gb_off = ga_off + d

    xs = x.reshape(bsz * seq, d)
    for l in range(depth):
        band_bias = _band_bias(rel_bias, attn_sinks[l], n_kv)
        mod = _adaln(c, w_ada[l], b_ada[l])
        shift1, scale1, gate1, shift2, scale2, gate2 = [
            t.reshape(bsz, 1, d) for t in jnp.split(mod, 6, axis=-1)]

        wcat = jnp.concatenate([lru_wa[l], lru_wx[l]], axis=-1).astype(BF16)
        rest, rec = _proj(xs, shift1, scale1, norm1_g[l].reshape(1, d),
                          w_in[l, :, :2 * lru_w].astype(BF16), w_in[l, :, 2 * lru_w:].astype(BF16),
                          wcat, conv_w[l], conv_b[l], lru_ba[l], lru_bx[l], lru_lambda[l], seq)
        x1 = _mix(rest, rec, xs, gate1, band_bias, w_lru_out[l].astype(BF16),
                  w_attn_out[l].astype(BF16), w_out[l].astype(BF16), seq, q_off, k_off, v_off,
                  ga_off, gb_off, kv_w, n_heads)
        if l != depth - 1:
            raise NotImplementedError("the fused final norm assumes a single layer")
        xs = _mlp(x1, shift2, scale2, gate2, norm2_g[l].reshape(1, d), final_g.reshape(1, d),
                  w_ff1[l].astype(BF16), w_ff2[l].astype(BF16), seq)
    return xs.reshape(bsz, seq, d)
```

```python
import functools
import math

import jax
import jax.numpy as jnp
from jax import lax
from jax.experimental import pallas as pl
from jax.experimental.pallas import tpu as pltpu

EPS = 1e-6
LRU_C = 8.0
LOG2E = 1.4426950408889634
ATTN_BLOCK = 128
NEG_INF = -1e30
MAX_DISTANCE = 128
LANES = 128
SUBLANES = 8
MXU_WIDTH = 256
VMEM_LIMIT = 56 * 1024 * 1024
ROW_CHUNKS = 2
COL_CHUNKS = 4
QUERY_BLOCKS = 2

F32 = jnp.float32
BF16 = jnp.bfloat16


def _tile(n, target, align):
    best = None
    t = align
    while t <= min(n, target):
        if n % t == 0:
            best = t
        t += align
    if best is None:
        raise ValueError(f"no tile for {n} (target {target}, align {align})")
    return best


def _params(*sem):
    return pltpu.CompilerParams(dimension_semantics=sem, vmem_limit_bytes=VMEM_LIMIT)


def _sigmoid(v):
    return 1.0 / (1.0 + jnp.exp2(v * -LOG2E))


def _rms_mod(x, g, shift, scale):
    var = jnp.mean(x * x, axis=-1, keepdims=True)
    y = x * lax.rsqrt(var + EPS) * g
    return y * (1.0 + scale) + shift


def _adaln_kernel(c_ref, w_ref, b_ref, o_ref):
    c = c_ref[...]
    act = (c * _sigmoid(c)).astype(BF16)
    o_ref[...] = jnp.dot(act, w_ref[...].astype(BF16), preferred_element_type=F32) + b_ref[...]


def _adaln(c, w, b):
    bsz, d = c.shape
    n = w.shape[1]
    tn = _tile(n, 1024, LANES)
    return pl.pallas_call(
        _adaln_kernel,
        grid=(n // tn,),
        in_specs=[pl.BlockSpec((bsz, d), lambda j: (0, 0)),
                  pl.BlockSpec((d, tn), lambda j: (0, j)),
                  pl.BlockSpec((1, tn), lambda j: (0, j))],
        out_specs=pl.BlockSpec((bsz, tn), lambda j: (0, j)),
        out_shape=jax.ShapeDtypeStruct((bsz, n), F32),
        compiler_params=_params("parallel"),
        name="adaln_mod",
    )(c, w, b.reshape(1, n))


def _gelu_tanh(v):
    k1 = -2.0 * math.sqrt(2.0 / math.pi) * LOG2E
    return v / (1.0 + jnp.exp2(v * (k1 + (k1 * 0.044715) * (v * v))))


def _rglru_tile(x, gate, seq_start, w_ref, cw_ref, cb_ref, ba_ref, bx_ref, lam_ref, o_ref,
                xtail, hcar, conv_k, other_mxu_work):
    tt, cw = x.shape
    groups = tt // SUBLANES

    assert conv_k - 1 < SUBLANES
    xs = jnp.concatenate([xtail[...], x], axis=0).reshape(groups + 1, SUBLANES, cw)
    xtail[...] = x[tt - SUBLANES:, :]
    first = lax.broadcasted_iota(jnp.int32, (groups + 1, SUBLANES, cw), 1) == 0
    xc = cb_ref[...] + cw_ref[conv_k - 1:conv_k, :] * x
    for k in range(conv_k - 2, -1, -1):
        rot = pltpu.roll(xs, 1, axis=1)
        xs = jnp.where(first, jnp.concatenate([rot[:1], rot[:-1]], axis=0), rot)
        xc = xc + cw_ref[k:k + 1, :] * xs[1:].reshape(tt, cw)

    lam = lam_ref[...]
    softplus_neg_lam = jnp.maximum(-lam, 0.0) + jnp.log(1.0 + jnp.exp(-jnp.abs(lam)))
    log2_a_coef = (-LRU_C * LOG2E) * softplus_neg_lam
    row0 = lax.broadcasted_iota(jnp.int32, (groups, SUBLANES, LANES), 1) == 0
    start_row = (lax.broadcasted_iota(jnp.int32, (SUBLANES, LANES), 0) == 0) & seq_start

    n_chunks = cw // LANES

    def gates(c):
        return jnp.dot(xc[:, c * LANES:(c + 1) * LANES].astype(BF16), w_ref[c],
                       preferred_element_type=F32)

    pieces = iter(other_mxu_work)

    def emit():
        piece = next(pieces, None)
        if piece is not None:
            piece()

    emit()
    zs = [gates(0)]
    for c in range(n_chunks):
        sl = slice(c * LANES, (c + 1) * LANES)
        if c + 1 < n_chunks:
            zs.append(gates(c + 1))
        emit()
        xh = xc[:, sl]
        z = zs[c]
        r = _sigmoid(z[:, :LANES] + ba_ref[:, sl])
        i = _sigmoid(z[:, LANES:] + bx_ref[:, sl])
        a = jnp.exp2(log2_a_coef[:, sl] * r)
        om = 1.0 - a * a
        mult = jnp.where(om > 0.0, om * lax.rsqrt(om), 0.0)
        mult = jnp.concatenate([jnp.where(start_row, 1.0, mult[:SUBLANES]), mult[SUBLANES:]],
                               axis=0)
        u3 = (mult * (i * xh)).reshape(groups, SUBLANES, LANES)
        a3 = a.reshape(groups, SUBLANES, LANES)

        az = jnp.where(row0, 0.0, a3)
        a_in = jnp.where(row0, a3, 0.0)
        az2 = az * pltpu.roll(az, 1, axis=1)
        az4 = az2 * pltpu.roll(az2, 2, axis=1)
        h_prev = hcar[0:1, sl]
        hs = []
        for gi in range(groups):
            if gi == groups // 2:
                emit()
            hg = u3[gi] + a_in[gi] * h_prev
            hg = hg + az[gi] * pltpu.roll(hg, 1, axis=0)
            hg = hg + az2[gi] * pltpu.roll(hg, 2, axis=0)
            hg = hg + az4[gi] * pltpu.roll(hg, 4, axis=0)
            hs.append(hg)
            h_prev = hg[SUBLANES - 1:SUBLANES, :]
        hcar[0:1, sl] = h_prev
        h = jnp.concatenate(hs, axis=0)
        o_ref[:, sl] = (h * _gelu_tanh(gate(sl))).astype(o_ref.dtype)
    for piece in pieces:
        piece()


def _proj_kernel(x_ref, sh_ref, sc_ref, g_ref, wx_ref, wg_ref, wr_ref, wcat_ref, cw_ref, cb_ref,
                 ba_ref, bx_ref, lam_ref, rest_ref, rec_ref, h_ref, lx_ref, gt_ref, xtail, hcar, *,
                 nl, per_b, conv_k):
    i = pl.program_id(0)
    j = pl.program_id(1)
    tm = x_ref.shape[0]
    rc = tm // ROW_CHUNKS
    cw = rec_ref.shape[1]
    per_step = wx_ref.shape[1] // cw

    def lru_cols(rows, h, base):
        yx = jnp.dot(h, wx_ref[...], preferred_element_type=F32).astype(BF16)
        yg = jnp.dot(h, wg_ref[...], preferred_element_type=F32).astype(BF16)
        for k in range(per_step):
            lx_ref[base + k, rows, :] = yx[:, k * cw:(k + 1) * cw]
            gt_ref[base + k, rows, :] = yg[:, k * cw:(k + 1) * cw]

    @pl.when(j == 0)
    def _():
        for r in range(ROW_CHUNKS):
            rows = pl.ds(r * rc, rc)
            h = _rms_mod(x_ref[rows, :], g_ref[...], sh_ref[...], sc_ref[...]).astype(BF16)
            h_ref[rows, :] = h
            lru_cols(rows, h, 0)

    @pl.when((j > 0) & (j < nl))
    def _():
        lru_cols(slice(None), h_ref[...], j * per_step)

    @pl.when(j >= nl)
    def _():
        c = j - nl
        seq_start = i % per_b == 0

        @pl.when(seq_start)
        def _():
            xtail[c] = jnp.zeros(xtail.shape[1:], F32)
            hcar[c] = jnp.zeros(hcar.shape[1:], F32)

        tr = rest_ref.shape[1]

        def rest_cols(lo):
            cols = slice(lo, min(lo + MXU_WIDTH, tr))
            rest_ref[:, cols] = jnp.dot(h_ref[...], wr_ref[:, cols],
                                        preferred_element_type=F32).astype(rest_ref.dtype)

        pieces = [functools.partial(rest_cols, lo) for lo in range(0, tr, MXU_WIDTH)]
        _rglru_tile(lx_ref[c].astype(F32), lambda sl: gt_ref[c, :, sl].astype(F32), seq_start,
                    wcat_ref, cw_ref, cb_ref, ba_ref, bx_ref, lam_ref, rec_ref,
                    xtail.at[c], hcar.at[c], conv_k, pieces)


def _proj(x2d, shift, scale, g, w_lru_cols, w_rest, wcat, conv_w, conv_b, ba, bx, lam, seq):
    m, d = x2d.shape
    lru_w = w_lru_cols.shape[1] // 2
    rest_w = w_rest.shape[1]
    conv_k = conv_w.shape[0]
    nr = max(n for n in (1, 2, 4) if lru_w % (n * LANES) == 0 and rest_w % (n * LANES) == 0)
    cw, tr = lru_w // nr, rest_w // nr
    tl = _tile(lru_w, 1024, cw)
    nl = lru_w // tl
    tm = _tile(seq, 512, SUBLANES * ROW_CHUNKS)
    per_b = seq // tm
    row = lambda i, j: (i // per_b, 0, 0)
    lstep = lambda j: jnp.minimum(j, nl - 1)
    chunk = lambda j: jnp.maximum(j - nl, 0)
    vec = pl.BlockSpec((1, cw), lambda i, j: (0, chunk(j)))
    kernel = functools.partial(_proj_kernel, nl=nl, per_b=per_b, conv_k=conv_k)
    return pl.pallas_call(
        kernel,
        grid=(m // tm, nl + nr),
        in_specs=[pl.BlockSpec((tm, d), lambda i, j: (i, 0)),
                  pl.BlockSpec((None, 1, d), row),
                  pl.BlockSpec((None, 1, d), row),
                  pl.BlockSpec((1, d), lambda i, j: (0, 0)),
                  pl.BlockSpec((d, tl), lambda i, j: (0, lstep(j))),
                  pl.BlockSpec((d, tl), lambda i, j: (0, nl + lstep(j))),
                  pl.BlockSpec((d, tr), lambda i, j: (0, chunk(j))),
                  pl.BlockSpec((cw // LANES, LANES, 2 * LANES), lambda i, j: (chunk(j), 0, 0)),
                  pl.BlockSpec((conv_k, cw), lambda i, j: (0, chunk(j))),
                  vec, vec, vec, vec],
        out_specs=[pl.BlockSpec((tm, tr), lambda i, j: (i, chunk(j))),
                   pl.BlockSpec((tm, cw), lambda i, j: (i, chunk(j)))],
        out_shape=[jax.ShapeDtypeStruct((m, rest_w), BF16),
                   jax.ShapeDtypeStruct((m, lru_w), BF16)],
        scratch_shapes=[pltpu.VMEM((tm, d), BF16),
                        pltpu.VMEM((nr, tm, cw), BF16), pltpu.VMEM((nr, tm, cw), BF16),
                        pltpu.VMEM((nr, SUBLANES, cw), F32), pltpu.VMEM((nr, SUBLANES, cw), F32)],
        compiler_params=_params("arbitrary", "arbitrary"),
        name="proj",
    )(x2d, shift, scale, g, w_lru_cols, w_lru_cols, w_rest, wcat, conv_w, conv_b.reshape(1, -1),
      ba.reshape(1, -1), bx.reshape(1, -1), lam.reshape(1, -1))


def _t5_bucket(rel, n_buckets):
    max_exact = n_buckets // 2
    relf = jnp.maximum(rel, 1).astype(F32)
    large = max_exact + (jnp.log(relf / max_exact) / math.log(MAX_DISTANCE / max_exact)
                         * (n_buckets - max_exact)).astype(jnp.int32)
    large = jnp.minimum(large, n_buckets - 1)
    return jnp.where(rel < max_exact, rel, large)


def _band_bias_kernel(rb_ref, sink_ref, bucket_ref, o_ref, *, pairs, n_buckets):
    kv = pl.program_id(0)
    blk = ATTN_BLOCK
    bucket = bucket_ref[...]
    qi = lax.broadcasted_iota(jnp.int32, (blk, 2 * blk), 0)
    ki = lax.broadcasted_iota(jnp.int32, (blk, 2 * blk), 1)
    rel = qi + blk - ki
    valid = (rel >= 0) & (rel < blk)
    for p in range(pairs):
        for par in range(2):
            h = (kv * pairs + p) * 2 + par
            bias = jnp.zeros((blk, 2 * blk), F32)
            for b in range(n_buckets):
                bias = jnp.where(bucket == b, rb_ref[b, h], bias)
            sink = sink_ref[h]
            rows, cols = slice(p * blk, (p + 1) * blk), slice(par * 2 * blk, (par + 1) * 2 * blk)
            o_ref[rows, cols] = jnp.where(ki == 0, sink, jnp.where(valid, bias, NEG_INF))


def _band_bias(rel_bias, sinks, n_kv):
    n_buckets, n_heads = rel_bias.shape
    pairs = n_heads // n_kv // 2
    blk = ATTN_BLOCK
    qi = jnp.arange(blk)[:, None]
    ki = jnp.arange(2 * blk)[None, :]
    bucket = _t5_bucket(jnp.maximum(qi + blk - ki, 0), n_buckets)
    smem = pl.BlockSpec(memory_space=pltpu.SMEM)
    return pl.pallas_call(
        functools.partial(_band_bias_kernel, pairs=pairs, n_buckets=n_buckets),
        grid=(n_kv,),
        in_specs=[smem, smem, pl.BlockSpec((blk, 2 * blk), lambda kv: (0, 0))],
        out_specs=pl.BlockSpec((None, pairs * blk, 4 * blk), lambda kv: (kv, 0, 0)),
        out_shape=jax.ShapeDtypeStruct((n_kv, pairs * blk, 4 * blk), F32),
        compiler_params=_params("parallel"),
        name="band_bias",
    )(rel_bias, sinks, bucket)


def _swap_lane_halves(v):
    u = pltpu.bitcast(v, jnp.uint32)
    return pltpu.bitcast(pltpu.roll(u, LANES // 2, axis=1), BF16)


def _mix_kernel(*refs, n_kv, pairs, scale, per_b):
    q_ref, kp_ref, kc_ref, vp_ref, vc_ref, bm_ref, rec_ref = refs[:7]
    ga_refs, gb_refs = refs[7:7 + n_kv], refs[7 + n_kv:7 + 2 * n_kv]
    x_ref, gate_ref, wl_ref, wa_ref, wo_ref, o_ref, att_ref, ya_ref, m_ref = refs[7 + 2 * n_kv:]
    blk = ATTN_BLOCK
    d = o_ref.shape[1]
    cc = d // n_kv
    fold = math.log2(scale).is_integer()
    lane = lax.broadcasted_iota(jnp.int32, (blk, LANES), 1)
    key = lax.broadcasted_iota(jnp.int32, (blk, LANES), 0)
    low = jnp.where(lane < LANES // 2, 1.0, 0.0).astype(BF16)
    high = jnp.where(lane < LANES // 2, 0.0, 1.0).astype(BF16)
    not_key0 = jnp.where(key == 0, 0.0, 1.0).astype(BF16)
    ones_blk = jnp.concatenate([low, low, high, high], axis=0)
    key_col = lax.broadcasted_iota(jnp.int32, (1, 4 * blk), 1) % (2 * blk)
    seq_start = pl.program_id(0) % per_b == 0
    start_mask = jnp.where(seq_start & (key_col >= 1) & (key_col < blk), NEG_INF, 0.0)

    def halves(t, kv):
        s = _swap_lane_halves(t)
        lo, hi = (t, s) if kv % 2 == 0 else (s, t)
        return lo * low, hi * high

    def window(prev, cur):
        return jnp.concatenate([prev[0] * not_key0, cur[0], prev[1] * not_key0, cur[1]], axis=0)

    for kv in range(n_kv):
        sl = slice((kv // 2) * LANES, (kv // 2 + 1) * LANES)
        kb = [halves(kp_ref[:, sl], kv)] + [halves(kc_ref[t * blk:(t + 1) * blk, sl], kv)
                                            for t in range(QUERY_BLOCKS)]
        vb = [halves(vp_ref[:, sl], kv)] + [halves(vc_ref[t * blk:(t + 1) * blk, sl], kv)
                                            for t in range(QUERY_BLOCKS)]
        for t in range(QUERY_BLOCKS):
            rows = slice(t * blk, (t + 1) * blk)
            kblk = window(kb[t], kb[t + 1])
            vaug = jnp.concatenate([window(vb[t], vb[t + 1]), ones_blk], axis=1)
            q = jnp.concatenate(
                [q_ref[rows, (kv * pairs + p) * LANES:(kv * pairs + p + 1) * LANES]
                 for p in range(pairs)], axis=0)
            if fold:
                q = q * scale
            s = lax.dot_general(q, kblk, (((1,), (1,)), ((), ())), preferred_element_type=F32)
            if not fold:
                s = s * scale
            s = s + bm_ref[kv]
            if t == 0:
                s = s + start_mask
            es = []
            for par in range(2):
                sp = s[:, par * 2 * blk:(par + 1) * 2 * blk]
                es.append(jnp.exp(sp - jnp.max(sp, axis=-1, keepdims=True)).astype(BF16))
            o = jnp.dot(jnp.concatenate(es, axis=1), vaug, preferred_element_type=F32)
            res = (o[:, :LANES] / o[:, LANES:]).astype(BF16)
            for p in range(pairs):
                att_ref[rows, (kv * pairs + p) * LANES:(kv * pairs + p + 1) * LANES] = (
                    res[p * blk:(p + 1) * blk])
        cols = slice(kv * cc, (kv + 1) * cc)
        ya = jnp.dot(rec_ref[...], wl_ref[:, cols], preferred_element_type=F32)
        ya_ref[:, cols] = _sigmoid(ga_refs[kv][...].astype(F32)) * ya

    for c in range(n_kv):
        cols = slice(c * cc, (c + 1) * cc)
        yb = jnp.dot(att_ref[...], wa_ref[:, cols], preferred_element_type=F32)
        gb = _sigmoid(gb_refs[c][...].astype(F32))
        m_ref[:, cols] = (ya_ref[:, cols] + gb * yb).astype(BF16)
    for c in range(n_kv):
        cols = slice(c * cc, (c + 1) * cc)
        y = jnp.dot(m_ref[...], wo_ref[:, cols], preferred_element_type=F32)
        o_ref[:, cols] = x_ref[:, cols] + gate_ref[:, cols] * y


def _mix(rest, rec, x2d, gate, band_bias, w_lru, w_att, w_out, seq, q_off, k_off, v_off, ga_off,
         gb_off, kv_w, n_heads):
    m, d = x2d.shape
    attn_w = w_att.shape[0]
    hd = attn_w // n_heads
    n_kv = kv_w // hd
    pairs = n_heads // n_kv // 2
    blk = ATTN_BLOCK
    tm = QUERY_BLOCKS * blk
    per_b = seq // tm
    cc = d // n_kv
    assert 2 * hd == LANES and n_heads == 2 * pairs * n_kv and kv_w % LANES == 0
    assert q_off % attn_w == 0 and k_off % kv_w == 0 and v_off % kv_w == 0
    assert seq % tm == 0 and cc % LANES == 0 and ga_off % cc == 0 and gb_off % cc == 0
    kb, vb = k_off // kv_w, v_off // kv_w
    prev = lambda col: (lambda i: (jnp.maximum(QUERY_BLOCKS * i - 1, 0), col))
    col_chunk = lambda off: [pl.BlockSpec((tm, cc), functools.partial(lambda c, i: (i, c),
                                                                      off // cc + c))
                             for c in range(n_kv)]
    resident = lambda a: pl.BlockSpec(a.shape, lambda i: (0,) * a.ndim,
                                      pipeline_mode=pl.Buffered(1))
    kernel = functools.partial(_mix_kernel, n_kv=n_kv, pairs=pairs, scale=hd ** -0.5, per_b=per_b)
    return pl.pallas_call(
        kernel,
        grid=(m // tm,),
        in_specs=[pl.BlockSpec((tm, attn_w), lambda i: (i, q_off // attn_w)),
                  pl.BlockSpec((blk, kv_w), prev(kb)),
                  pl.BlockSpec((tm, kv_w), lambda i: (i, kb)),
                  pl.BlockSpec((blk, kv_w), prev(vb)),
                  pl.BlockSpec((tm, kv_w), lambda i: (i, vb)),
                  resident(band_bias),
                  pl.BlockSpec((tm, rec.shape[1]), lambda i: (i, 0))]
                 + col_chunk(ga_off) + col_chunk(gb_off)
                 + [pl.BlockSpec((tm, d), lambda i: (i, 0)),
                    pl.BlockSpec((None, 1, d), lambda i: (i // per_b, 0, 0)),
                    resident(w_lru), resident(w_att), resident(w_out)],
        out_specs=pl.BlockSpec((tm, d), lambda i: (i, 0)),
        out_shape=jax.ShapeDtypeStruct((m, d), F32),
        scratch_shapes=[pltpu.VMEM((tm, attn_w), BF16), pltpu.VMEM((tm, d), F32),
                        pltpu.VMEM((tm, d), BF16)],
        compiler_params=_params("parallel"),
        name="mix",
    )(rest, rest, rest, rest, rest, band_bias, rec, *([rest] * (2 * n_kv)), x2d, gate,
      w_lru, w_att, w_out)


def _mlp_kernel(x_ref, sh_ref, sc_ref, gate_ref, g2_ref, gf_ref, w1_hbm, w2_hbm, o_ref,
                h_ref, ff_ref, w1_buf, w2_buf, sems, *, nf):
    i = pl.program_id(0)
    last_tile = pl.num_programs(0) - 1
    tm, d = x_ref.shape
    n1, _, tf = w1_buf.shape
    n2, _, tn = w2_buf.shape
    nd = d // tn
    rc = tm // ROW_CHUNKS
    assert n1 == 3 and n2 == 2 and nf >= 2 and nd >= 2

    def w1_copy(k):
        return pltpu.make_async_copy(w1_hbm.at[:, pl.ds(k * tf, tf)], w1_buf.at[k % n1],
                                     sems.at[0, k % n1])

    def w2_copy(g):
        return pltpu.make_async_copy(w2_hbm.at[:, pl.ds(g * tn, tn)], w2_buf.at[g % n2],
                                     sems.at[1, g % n2])

    @pl.when(i == 0)
    def _():
        w1_copy(0).start()
        w1_copy(1).start()

    for k in range(nf):
        w1_copy(k).wait()
        if k + 2 < nf:
            w1_copy(k + 2).start()
        if k >= nf - 2:
            w2_copy(k - (nf - 2)).start()
        if k == 0:
            for r in range(ROW_CHUNKS):
                rows = pl.ds(r * rc, rc)
                h = _rms_mod(x_ref[rows, :], g2_ref[...], sh_ref[...], sc_ref[...]).astype(BF16)
                h_ref[rows, :] = h
                ff = jnp.dot(h, w1_buf[0], preferred_element_type=F32)
                ff_ref[rows, 0:tf] = jnp.square(jnp.maximum(ff, 0.0)).astype(BF16)
        else:
            ff = jnp.dot(h_ref[...], w1_buf[k % n1], preferred_element_type=F32)
            ff_ref[:, k * tf:(k + 1) * tf] = jnp.square(jnp.maximum(ff, 0.0)).astype(BF16)

    ssq = jnp.zeros((tm, 1), F32)
    for g in range(nd):
        w2_copy(g).wait()
        if 1 <= g < nd - 1:
            w2_copy(g + 1).start()
        if g >= nd - 2:
            w1_copy(g - (nd - 2)).start()
        cols = slice(g * tn, (g + 1) * tn)
        y = jnp.dot(ff_ref[...], w2_buf[g % n2], preferred_element_type=F32)
        x2 = x_ref[:, cols] + gate_ref[:, cols] * y
        ssq = ssq + jnp.sum(x2 * x2, axis=-1, keepdims=True)
        o_ref[:, cols] = x2

    inv = lax.rsqrt(ssq * (1.0 / d) + EPS)
    for r in range(ROW_CHUNKS):
        rows = pl.ds(r * rc, rc)
        o_ref[rows, :] = o_ref[rows, :] * inv[r * rc:(r + 1) * rc] * gf_ref[...]

    @pl.when(i == last_tile)
    def _():
        w1_copy(0).wait()
        w1_copy(1).wait()


def _mlp(x1, shift, scale, gate, g2, gf, w1, w2, seq):
    m, d = x1.shape
    dff = w1.shape[1]
    tm = _tile(seq, 512, SUBLANES * ROW_CHUNKS)
    tf = _tile(dff // 2, 512, LANES)
    tn = _tile(d // 2, 2 * MXU_WIDTH, LANES)
    nf = dff // tf
    per_b = seq // tm
    row = lambda i: (i // per_b, 0, 0)
    const = lambda i: (0, 0)
    hbm = pl.BlockSpec(memory_space=pl.ANY)
    return pl.pallas_call(
        functools.partial(_mlp_kernel, nf=nf),
        grid=(m // tm,),
        in_specs=[pl.BlockSpec((tm, d), lambda i: (i, 0)),
                  pl.BlockSpec((None, 1, d), row),
                  pl.BlockSpec((None, 1, d), row),
                  pl.BlockSpec((None, 1, d), row),
                  pl.BlockSpec((1, d), const),
                  pl.BlockSpec((1, d), const),
                  hbm, hbm],
        out_specs=pl.BlockSpec((tm, d), lambda i: (i, 0)),
        out_shape=jax.ShapeDtypeStruct((m, d), F32),
        scratch_shapes=[pltpu.VMEM((tm, d), BF16), pltpu.VMEM((tm, dff), BF16),
                        pltpu.VMEM((3, d, tf), BF16), pltpu.VMEM((2, dff, tn), BF16),
                        pltpu.SemaphoreType.DMA((2, 3))],
        compiler_params=_params("arbitrary"),
        name="mlp",
    )(x1, shift, scale, gate, g2, gf, w1, w2)


def kernel(x, c, w_ada, b_ada, norm1_g, w_in, conv_w, conv_b, lru_wa, lru_ba, lru_wx, lru_bx,
           lru_lambda, w_lru_out, w_attn_out, attn_sinks, rel_bias, w_out, norm2_g, w_ff1, w_ff2,
           final_g):
    bsz, seq, d = x.shape
    depth = w_in.shape[0]
    lru_w = w_lru_out.shape[1]
    attn_w = w_attn_out.shape[1]
    kv_w = (w_in.shape[2] - 2 * lru_w - attn_w - 2 * d) // 2
    n_heads = attn_sinks.shape[1]
    n_kv = kv_w // (attn_w // n_heads)
    q_off = 0
    k_off = attn_w
    v_off = k_off + kv_w
    ga_off = v_off + kv_w
    gb_off = ga_off + d

    xs = x.reshape(bsz * seq, d)
    for l in range(depth):
        band_bias = _band_bias(rel_bias, attn_sinks[l], n_kv)
        mod = _adaln(c, w_ada[l], b_ada[l])
        shift1, scale1, gate1, shift2, scale2, gate2 = [
            t.reshape(bsz, 1, d) for t in jnp.split(mod, 6, axis=-1)]

        wcat = jnp.concatenate([lru_wa[l], lru_wx[l]], axis=-1).astype(BF16)
        rest, rec = _proj(xs, shift1, scale1, norm1_g[l].reshape(1, d),
                          w_in[l, :, :2 * lru_w].astype(BF16), w_in[l, :, 2 * lru_w:].astype(BF16),
                          wcat, conv_w[l], conv_b[l], lru_ba[l], lru_bx[l], lru_lambda[l], seq)
        x1 = _mix(rest, rec, xs, gate1, band_bias, w_lru_out[l].astype(BF16),
                  w_attn_out[l].astype(BF16), w_out[l].astype(BF16), seq, q_off, k_off, v_off,
                  ga_off, gb_off, kv_w, n_heads)
        if l != depth - 1:
            raise NotImplementedError("the fused final norm assumes a single layer")
        xs = _mlp(x1, shift2, scale2, gate2, norm2_g[l].reshape(1, d), final_g.reshape(1, d),
                  w_ff1[l].astype(BF16), w_ff2[l].astype(BF16), seq)
    return xs.reshape(bsz, seq, d)
```

```python
import functools
import math

import jax
import jax.numpy as jnp
from jax import lax
from jax.experimental import pallas as pl
from jax.experimental.pallas import tpu as pltpu

EPS = 1e-6
LRU_C = 8.0
LOG2E = 1.4426950408889634
ATTN_BLOCK = 128
NEG_INF = -1e30
MAX_DISTANCE = 128
LANES = 128
SUBLANES = 8
MXU_WIDTH = 256
VMEM_LIMIT = 56 * 1024 * 1024
ROW_CHUNKS = 2
QUERY_BLOCKS = 2

F32 = jnp.float32
BF16 = jnp.bfloat16


def _tile(n, target, align):
    best = None
    t = align
    while t <= min(n, target):
        if n % t == 0:
            best = t
        t += align
    if best is None:
        raise ValueError(f"no tile for {n} (target {target}, align {align})")
    return best


def _params(*sem):
    return pltpu.CompilerParams(dimension_semantics=sem, vmem_limit_bytes=VMEM_LIMIT)


def _sigmoid(v):
    return 1.0 / (1.0 + jnp.exp2(v * -LOG2E))


def _rms_mod(x, g, shift, scale):
    var = jnp.mean(x * x, axis=-1, keepdims=True)
    y = x * lax.rsqrt(var + EPS) * g
    return y * (1.0 + scale) + shift


def _adaln_kernel(c_ref, w_ref, b_ref, o_ref):
    c = c_ref[...]
    act = (c * _sigmoid(c)).astype(BF16)
    o_ref[...] = jnp.dot(act, w_ref[...].astype(BF16), preferred_element_type=F32) + b_ref[...]


def _adaln(c, w, b):
    bsz, d = c.shape
    n = w.shape[1]
    tn = _tile(n, 1024, LANES)
    return pl.pallas_call(
        _adaln_kernel,
        grid=(n // tn,),
        in_specs=[pl.BlockSpec((bsz, d), lambda j: (0, 0)),
                  pl.BlockSpec((d, tn), lambda j: (0, j)),
                  pl.BlockSpec((1, tn), lambda j: (0, j))],
        out_specs=pl.BlockSpec((bsz, tn), lambda j: (0, j)),
        out_shape=jax.ShapeDtypeStruct((bsz, n), F32),
        compiler_params=_params("parallel"),
        name="adaln_mod",
    )(c, w, b.reshape(1, n))


def _gelu_tanh(v):
    k1 = -2.0 * math.sqrt(2.0 / math.pi) * LOG2E
    return v / (1.0 + jnp.exp2(v * (k1 + (k1 * 0.044715) * (v * v))))


def _rglru_tile(lx_ref, slab0, gate, seq_start, w_ref, cw_ref, cb_ref, ba_ref, bx_ref, lam_ref,
                o_ref, xtail, hcar, hm_ref, conv_k, other_mxu_work):
    _, tt, cw = o_ref.shape
    n_chunks = cw // LANES
    lam = lam_ref[...]
    softplus_neg_lam = jnp.maximum(-lam, 0.0) + jnp.log(1.0 + jnp.exp(-jnp.abs(lam)))
    log2_a_coef = (-LRU_C * LOG2E) * softplus_neg_lam

    def conv(s):
        sl = slice(s * LANES, (s + 1) * LANES)
        x3 = lx_ref[slab0 + s].reshape(tt, SUBLANES, LANES)
        xs = jnp.concatenate([xtail[slab0 + s], x3], axis=0)
        xtail[slab0 + s] = xs[tt:]
        xc = cb_ref[:, sl] + cw_ref[conv_k - 1:conv_k, sl] * x3
        for k in range(conv_k - 1):
            xc = xc + cw_ref[k:k + 1, sl] * xs[k:k + tt]
        return xc

    def gates(s, xc):
        return jnp.dot(xc.reshape(tt * SUBLANES, LANES).astype(BF16), w_ref[s],
                       preferred_element_type=F32)

    pieces = iter(other_mxu_work)

    def emit():
        piece = next(pieces, None)
        if piece is not None:
            piece()

    emit()
    xcs = [conv(0)]
    zs = [gates(0, xcs[0])]
    for s in range(n_chunks):
        sl = slice(s * LANES, (s + 1) * LANES)
        if s + 1 < n_chunks:
            xcs.append(conv(s + 1))
            zs.append(gates(s + 1, xcs[s + 1]))
        emit()
        xh, z = xcs[s], zs[s]
        r = _sigmoid(z[:, :LANES] + ba_ref[:, sl]).reshape(tt, SUBLANES, LANES)
        i = _sigmoid(z[:, LANES:] + bx_ref[:, sl]).reshape(tt, SUBLANES, LANES)
        a = jnp.exp2(log2_a_coef[:, sl] * r)
        om = 1.0 - a * a
        mult = jnp.where(om > 0.0, om * lax.rsqrt(om), 0.0)
        mult = jnp.concatenate([jnp.where(seq_start, 1.0, mult[:1]), mult[1:]], axis=0)
        u = mult * (i * xh)
        h = hcar[slab0 + s]
        for t in range(tt):
            if t == tt // 2:
                emit()
            h = a[t] * h + u[t]
            hm_ref[s, t * SUBLANES:(t + 1) * SUBLANES, :] = h
        hcar[slab0 + s] = h
        for b in range(SUBLANES):
            hb = hm_ref[s, pl.ds(b, tt, stride=SUBLANES), :]
            o_ref[b, :, sl] = (hb * _gelu_tanh(gate(b, sl))).astype(o_ref.dtype)
    for piece in pieces:
        piece()


def _proj_kernel(x_ref, sh_ref, sc_ref, g_ref, wx_ref, wg_ref, wr_ref, wcat_ref, cw_ref, cb_ref,
                 ba_ref, bx_ref, lam_ref, rest_ref, rec_ref, h_ref, lx_ref, gt_ref, xtail, hcar,
                 hm_ref, *, nl, conv_k):
    i = pl.program_id(0)
    j = pl.program_id(1)
    bsz, tt, d = x_ref.shape
    cw = rec_ref.shape[2]
    tl = wx_ref.shape[1]
    bc = bsz // ROW_CHUNKS

    def lru_cols(b0, nb, h, step):
        yx = jnp.dot(h, wx_ref[...], preferred_element_type=F32)
        yg = jnp.dot(h, wg_ref[...], preferred_element_type=F32).astype(BF16)
        for k in range(tl // LANES):
            for bb in range(nb):
                lx_ref[step * (tl // LANES) + k, pl.ds(b0 + bb, tt, stride=SUBLANES), :] = (
                    yx[bb * tt:(bb + 1) * tt, k * LANES:(k + 1) * LANES])
        for k in range(tl // cw):
            gt_ref[step * (tl // cw) + k, b0 * tt:(b0 + nb) * tt, :] = yg[:, k * cw:(k + 1) * cw]

    @pl.when(j == 0)
    def _():
        @pl.when(i == 0)
        def _():
            xtail[...] = jnp.zeros_like(xtail)
            hcar[...] = jnp.zeros_like(hcar)

        for r in range(ROW_CHUNKS):
            x = x_ref[r * bc:(r + 1) * bc]
            var = jnp.mean(x * x, axis=-1, keepdims=True)
            y = x * lax.rsqrt(var + EPS) * g_ref[...]
            seqs = slice(r * bc, (r + 1) * bc)
            h = (y * (1.0 + sc_ref[seqs]) + sh_ref[seqs]).reshape(bc * tt, d).astype(BF16)
            h_ref[r * bc * tt:(r + 1) * bc * tt, :] = h
            lru_cols(r * bc, bc, h, 0)

    @pl.when((j > 0) & (j < nl))
    def _():
        lru_cols(0, bsz, h_ref[...], j)

    @pl.when(j >= nl)
    def _():
        c = j - nl
        tr = rest_ref.shape[2]

        def rest_cols(lo):
            hi = min(lo + MXU_WIDTH, tr)
            y = jnp.dot(h_ref[...], wr_ref[:, lo:hi], preferred_element_type=F32)
            rest_ref[:, :, lo:hi] = y.astype(rest_ref.dtype).reshape(bsz, tt, hi - lo)

        pieces = [functools.partial(rest_cols, lo) for lo in range(0, tr, MXU_WIDTH)]
        _rglru_tile(lx_ref, c * (cw // LANES),
                    lambda b, sl: gt_ref[c, b * tt:(b + 1) * tt, sl].astype(F32), i == 0,
                    wcat_ref, cw_ref, cb_ref, ba_ref, bx_ref, lam_ref, rec_ref,
                    xtail, hcar, hm_ref, conv_k, pieces)


def _proj(x, shift, scale, g, w_lru_cols, w_rest, wcat, conv_w, conv_b, ba, bx, lam):
    bsz, seq, d = x.shape
    assert bsz == SUBLANES
    lru_w = w_lru_cols.shape[1] // 2
    rest_w = w_rest.shape[1]
    conv_k = conv_w.shape[0]
    nr = max(n for n in (1, 2, 4) if lru_w % (n * LANES) == 0 and rest_w % (n * LANES) == 0)
    cw, tr = lru_w // nr, rest_w // nr
    tl = _tile(lru_w, 1024, cw)
    nl = lru_w // tl
    tt = _tile(seq, 64, SUBLANES)
    tm = bsz * tt
    lstep = lambda j: jnp.minimum(j, nl - 1)
    chunk = lambda j: jnp.maximum(j - nl, 0)
    full = lambda shape: pl.BlockSpec(shape, lambda i, j: (0,) * len(shape))
    vec = pl.BlockSpec((1, cw), lambda i, j: (0, chunk(j)))
    slabs = lru_w // LANES
    kernel = functools.partial(_proj_kernel, nl=nl, conv_k=conv_k)
    rest, rec = pl.pallas_call(
        kernel,
        grid=(seq // tt, nl + nr),
        in_specs=[pl.BlockSpec((bsz, tt, d), lambda i, j: (0, i, 0)),
                  full((bsz, 1, d)), full((bsz, 1, d)), full((1, d)),
                  pl.BlockSpec((d, tl), lambda i, j: (0, lstep(j))),
                  pl.BlockSpec((d, tl), lambda i, j: (0, nl + lstep(j))),
                  pl.BlockSpec((d, tr), lambda i, j: (0, chunk(j))),
                  pl.BlockSpec((cw // LANES, LANES, 2 * LANES), lambda i, j: (chunk(j), 0, 0)),
                  pl.BlockSpec((conv_k, cw), lambda i, j: (0, chunk(j))),
                  vec, vec, vec, vec],
        out_specs=[pl.BlockSpec((bsz, tt, tr), lambda i, j: (0, i, chunk(j))),
                   pl.BlockSpec((bsz, tt, cw), lambda i, j: (0, i, chunk(j)))],
        out_shape=[jax.ShapeDtypeStruct((bsz, seq, rest_w), BF16),
                   jax.ShapeDtypeStruct((bsz, seq, lru_w), BF16)],
        scratch_shapes=[pltpu.VMEM((tm, d), BF16),
                        pltpu.VMEM((slabs, tm, LANES), F32), pltpu.VMEM((nr, tm, cw), BF16),
                        pltpu.VMEM((slabs, conv_k - 1, SUBLANES, LANES), F32),
                        pltpu.VMEM((slabs, SUBLANES, LANES), F32),
                        pltpu.VMEM((cw // LANES, tm, LANES), F32)],
        compiler_params=_params("arbitrary", "arbitrary"),
        name="proj",
    )(x, shift, scale, g, w_lru_cols, w_lru_cols, w_rest, wcat, conv_w, conv_b.reshape(1, -1),
      ba.reshape(1, -1), bx.reshape(1, -1), lam.reshape(1, -1))
    return rest.reshape(bsz * seq, rest_w), rec.reshape(bsz * seq, lru_w)


def _t5_bucket(rel, n_buckets):
    max_exact = n_buckets // 2
    relf = jnp.maximum(rel, 1).astype(F32)
    large = max_exact + (jnp.log(relf / max_exact) / math.log(MAX_DISTANCE / max_exact)
                         * (n_buckets - max_exact)).astype(jnp.int32)
    large = jnp.minimum(large, n_buckets - 1)
    return jnp.where(rel < max_exact, rel, large)


def _band_bias_kernel(rb_ref, sink_ref, bucket_ref, o_ref, *, pairs, n_buckets):
    kv = pl.program_id(0)
    blk = ATTN_BLOCK
    bucket = bucket_ref[...]
    qi = lax.broadcasted_iota(jnp.int32, (blk, 2 * blk), 0)
    ki = lax.broadcasted_iota(jnp.int32, (blk, 2 * blk), 1)
    rel = qi + blk - ki
    valid = (rel >= 0) & (rel < blk)
    for p in range(pairs):
        for par in range(2):
            h = (kv * pairs + p) * 2 + par
            bias = jnp.zeros((blk, 2 * blk), F32)
            for b in range(n_buckets):
                bias = jnp.where(bucket == b, rb_ref[b, h], bias)
            sink = sink_ref[h]
            rows, cols = slice(p * blk, (p + 1) * blk), slice(par * 2 * blk, (par + 1) * 2 * blk)
            o_ref[rows, cols] = jnp.where(ki == 0, sink, jnp.where(valid, bias, NEG_INF))


def _band_bias(rel_bias, sinks, n_kv):
    n_buckets, n_heads = rel_bias.shape
    pairs = n_heads // n_kv // 2
    blk = ATTN_BLOCK
    qi = jnp.arange(blk)[:, None]
    ki = jnp.arange(2 * blk)[None, :]
    bucket = _t5_bucket(jnp.maximum(qi + blk - ki, 0), n_buckets)
    smem = pl.BlockSpec(memory_space=pltpu.SMEM)
    return pl.pallas_call(
        functools.partial(_band_bias_kernel, pairs=pairs, n_buckets=n_buckets),
        grid=(n_kv,),
        in_specs=[smem, smem, pl.BlockSpec((blk, 2 * blk), lambda kv: (0, 0))],
        out_specs=pl.BlockSpec((None, pairs * blk, 4 * blk), lambda kv: (kv, 0, 0)),
        out_shape=jax.ShapeDtypeStruct((n_kv, pairs * blk, 4 * blk), F32),
        compiler_params=_params("parallel"),
        name="band_bias",
    )(rel_bias, sinks, bucket)


def _swap_lane_halves(v):
    u = pltpu.bitcast(v, jnp.uint32)
    return pltpu.bitcast(pltpu.roll(u, LANES // 2, axis=1), BF16)


def _mix_kernel(*refs, n_kv, pairs, scale, per_b):
    q_ref, kp_ref, kc_ref, vp_ref, vc_ref, bm_ref, rec_ref = refs[:7]
    ga_refs, gb_refs = refs[7:7 + n_kv], refs[7 + n_kv:7 + 2 * n_kv]
    x_ref, gate_ref, wl_ref, wa_ref, wo_ref, o_ref, att_ref, ya_ref, m_ref = refs[7 + 2 * n_kv:]
    blk = ATTN_BLOCK
    d = o_ref.shape[1]
    cc = d // n_kv
    fold = math.log2(scale).is_integer()
    lane = lax.broadcasted_iota(jnp.int32, (blk, LANES), 1)
    key = lax.broadcasted_iota(jnp.int32, (blk, LANES), 0)
    low = jnp.where(lane < LANES // 2, 1.0, 0.0).astype(BF16)
    high = jnp.where(lane < LANES // 2, 0.0, 1.0).astype(BF16)
    not_key0 = jnp.where(key == 0, 0.0, 1.0).astype(BF16)
    ones_blk = jnp.concatenate([low, low, high, high], axis=0)
    key_col = lax.broadcasted_iota(jnp.int32, (1, 4 * blk), 1) % (2 * blk)
    seq_start = pl.program_id(0) % per_b == 0
    start_mask = jnp.where(seq_start & (key_col >= 1) & (key_col < blk), NEG_INF, 0.0)

    def halves(t, kv):
        s = _swap_lane_halves(t)
        lo, hi = (t, s) if kv % 2 == 0 else (s, t)
        return lo * low, hi * high

    def window(prev, cur):
        return jnp.concatenate([prev[0] * not_key0, cur[0], prev[1] * not_key0, cur[1]], axis=0)

    for kv in range(n_kv):
        sl = slice((kv // 2) * LANES, (kv // 2 + 1) * LANES)
        kb = [halves(kp_ref[:, sl], kv)] + [halves(kc_ref[t * blk:(t + 1) * blk, sl], kv)
                                            for t in range(QUERY_BLOCKS)]
        vb = [halves(vp_ref[:, sl], kv)] + [halves(vc_ref[t * blk:(t + 1) * blk, sl], kv)
                                            for t in range(QUERY_BLOCKS)]
        for t in range(QUERY_BLOCKS):
            rows = slice(t * blk, (t + 1) * blk)
            kblk = window(kb[t], kb[t + 1])
            vaug = jnp.concatenate([window(vb[t], vb[t + 1]), ones_blk], axis=1)
            q = jnp.concatenate(
                [q_ref[rows, (kv * pairs + p) * LANES:(kv * pairs + p + 1) * LANES]
                 for p in range(pairs)], axis=0)
            if fold:
                q = q * scale
            s = lax.dot_general(q, kblk, (((1,), (1,)), ((), ())), preferred_element_type=F32)
            if not fold:
                s = s * scale
            s = s + bm_ref[kv]
            if t == 0:
                s = s + start_mask
            es = []
            for par in range(2):
                sp = s[:, par * 2 * blk:(par + 1) * 2 * blk]
                es.append(jnp.exp(sp - jnp.max(sp, axis=-1, keepdims=True)).astype(BF16))
            o = jnp.dot(jnp.concatenate(es, axis=1), vaug, preferred_element_type=F32)
            res = (o[:, :LANES] / o[:, LANES:]).astype(BF16)
            for p in range(pairs):
                att_ref[rows, (kv * pairs + p) * LANES:(kv * pairs + p + 1) * LANES] = (
                    res[p * blk:(p + 1) * blk])
        cols = slice(kv * cc, (kv + 1) * cc)
        ya = jnp.dot(rec_ref[...], wl_ref[:, cols], preferred_element_type=F32)
        ya_ref[:, cols] = _sigmoid(ga_refs[kv][...].astype(F32)) * ya

    for c in range(n_kv):
        cols = slice(c * cc, (c + 1) * cc)
        yb = jnp.dot(att_ref[...], wa_ref[:, cols], preferred_element_type=F32)
        gb = _sigmoid(gb_refs[c][...].astype(F32))
        m_ref[:, cols] = (ya_ref[:, cols] + gb * yb).astype(BF16)
    for c in range(n_kv):
        cols = slice(c * cc, (c + 1) * cc)
        y = jnp.dot(m_ref[...], wo_ref[:, cols], preferred_element_type=F32)
        o_ref[:, cols] = x_ref[:, cols] + gate_ref[:, cols] * y


def _mix(rest, rec, x2d, gate, band_bias, w_lru, w_att, w_out, seq, q_off, k_off, v_off, ga_off,
         gb_off, kv_w, n_heads):
    m, d = x2d.shape
    attn_w = w_att.shape[0]
    hd = attn_w // n_heads
    n_kv = kv_w // hd
    pairs = n_heads // n_kv // 2
    blk = ATTN_BLOCK
    tm = QUERY_BLOCKS * blk
    per_b = seq // tm
    cc = d // n_kv
    assert 2 * hd == LANES and n_heads == 2 * pairs * n_kv and kv_w % LANES == 0
    assert q_off % attn_w == 0 and k_off % kv_w == 0 and v_off % kv_w == 0
    assert seq % tm == 0 and cc % LANES == 0 and ga_off % cc == 0 and gb_off % cc == 0
    kb, vb = k_off // kv_w, v_off // kv_w
    prev = lambda col: (lambda i: (jnp.maximum(QUERY_BLOCKS * i - 1, 0), col))
    col_chunk = lambda off: [pl.BlockSpec((tm, cc), functools.partial(lambda c, i: (i, c),
                                                                      off // cc + c))
                             for c in range(n_kv)]
    resident = lambda a: pl.BlockSpec(a.shape, lambda i: (0,) * a.ndim,
                                      pipeline_mode=pl.Buffered(1))
    kernel = functools.partial(_mix_kernel, n_kv=n_kv, pairs=pairs, scale=hd ** -0.5, per_b=per_b)
    return pl.pallas_call(
        kernel,
        grid=(m // tm,),
        in_specs=[pl.BlockSpec((tm, attn_w), lambda i: (i, q_off // attn_w)),
                  pl.BlockSpec((blk, kv_w), prev(kb)),
                  pl.BlockSpec((tm, kv_w), lambda i: (i, kb)),
                  pl.BlockSpec((blk, kv_w), prev(vb)),
                  pl.BlockSpec((tm, kv_w), lambda i: (i, vb)),
                  resident(band_bias),
                  pl.BlockSpec((tm, rec.shape[1]), lambda i: (i, 0))]
                 + col_chunk(ga_off) + col_chunk(gb_off)
                 + [pl.BlockSpec((tm, d), lambda i: (i, 0)),
                    pl.BlockSpec((None, 1, d), lambda i: (i // per_b, 0, 0)),
                    resident(w_lru), resident(w_att), resident(w_out)],
        out_specs=pl.BlockSpec((tm, d), lambda i: (i, 0)),
        out_shape=jax.ShapeDtypeStruct((m, d), F32),
        scratch_shapes=[pltpu.VMEM((tm, attn_w), BF16), pltpu.VMEM((tm, d), F32),
                        pltpu.VMEM((tm, d), BF16)],
        compiler_params=_params("parallel"),
        name="mix",
    )(rest, rest, rest, rest, rest, band_bias, rec, *([rest] * (2 * n_kv)), x2d, gate,
      w_lru, w_att, w_out)


def _mlp_kernel(x_ref, sh_ref, sc_ref, gate_ref, g2_ref, gf_ref, w1_ref, w2_ref, o_ref,
                h_ref, ff_ref, *, nf):
    f = pl.program_id(1)
    rc = x_ref.shape[0] // ROW_CHUNKS

    def up(rows, slot):
        ff = jnp.dot(h_ref[rows, :], w1_ref[...], preferred_element_type=F32)
        ff_ref[slot, rows, :] = jnp.square(jnp.maximum(ff, 0.0)).astype(BF16)

    def down(rows, slot):
        return jnp.dot(ff_ref[slot, rows, :], w2_ref[...], preferred_element_type=F32)

    @pl.when(f == 0)
    def _():
        o_ref[...] = jnp.zeros_like(o_ref)
        for r in range(ROW_CHUNKS):
            rows = pl.ds(r * rc, rc)
            h_ref[rows, :] = _rms_mod(x_ref[rows, :], g2_ref[...], sh_ref[...],
                                      sc_ref[...]).astype(BF16)
            up(rows, 0)

    @pl.when((f > 0) & (f < nf))
    def _():
        o_ref[...] += down(slice(None), (f - 1) % 2)
        up(slice(None), f % 2)

    @pl.when(f == nf)
    def _():
        for r in range(ROW_CHUNKS):
            rows = pl.ds(r * rc, rc)
            x2 = x_ref[rows, :] + gate_ref[...] * (o_ref[rows, :] + down(rows, (nf - 1) % 2))
            var = jnp.mean(x2 * x2, axis=-1, keepdims=True)
            o_ref[rows, :] = x2 * lax.rsqrt(var + EPS) * gf_ref[...]


def _mlp(x1, shift, scale, gate, g2, gf, w1, w2, seq):
    m, d = x1.shape
    dff = w1.shape[1]
    tm = _tile(seq, 512, SUBLANES * ROW_CHUNKS)
    tf = _tile(dff, 1024, LANES)
    nf = dff // tf
    per_b = seq // tm
    row = lambda i, f: (i // per_b, 0, 0)
    const = lambda i, f: (0, 0)
    return pl.pallas_call(
        functools.partial(_mlp_kernel, nf=nf),
        grid=(m // tm, nf + 1),
        in_specs=[pl.BlockSpec((tm, d), lambda i, f: (i, 0)),
                  pl.BlockSpec((None, 1, d), row),
                  pl.BlockSpec((None, 1, d), row),
                  pl.BlockSpec((None, 1, d), row),
                  pl.BlockSpec((1, d), const),
                  pl.BlockSpec((1, d), const),
                  pl.BlockSpec((d, tf), lambda i, f: (0, jnp.minimum(f, nf - 1))),
                  pl.BlockSpec((tf, d), lambda i, f: (jnp.maximum(f - 1, 0), 0))],
        out_specs=pl.BlockSpec((tm, d), lambda i, f: (i, 0)),
        out_shape=jax.ShapeDtypeStruct((m, d), F32),
        scratch_shapes=[pltpu.VMEM((tm, d), BF16), pltpu.VMEM((2, tm, tf), BF16)],
        compiler_params=_params("parallel", "arbitrary"),
        name="mlp",
    )(x1, shift, scale, gate, g2, gf, w1, w2)


def kernel(x, c, w_ada, b_ada, norm1_g, w_in, conv_w, conv_b, lru_wa, lru_ba, lru_wx, lru_bx,
           lru_lambda, w_lru_out, w_attn_out, attn_sinks, rel_bias, w_out, norm2_g, w_ff1, w_ff2,
           final_g):
    bsz, seq, d = x.shape
    depth = w_in.shape[0]
    lru_w = w_lru_out.shape[1]
    attn_w = w_attn_out.shape[1]
    kv_w = (w_in.shape[2] - 2 * lru_w - attn_w - 2 * d) // 2
    n_heads = attn_sinks.shape[1]
    n_kv = kv_w // (attn_w // n_heads)
    q_off = 0
    k_off = attn_w
    v_off = k_off + kv_w
    ga_off = v_off + kv_w
    gb_off = ga_off + d

    xs = x.reshape(bsz * seq, d)
    for l in range(depth):
        band_bias = _band_bias(rel_bias, attn_sinks[l], n_kv)
        mod = _adaln(c, w_ada[l], b_ada[l])
        shift1, scale1, gate1, shift2, scale2, gate2 = [
            t.reshape(bsz, 1, d) for t in jnp.split(mod, 6, axis=-1)]

        wcat = jnp.concatenate([lru_wa[l], lru_wx[l]], axis=-1).astype(BF16)
        rest, rec = _proj(xs.reshape(bsz, seq, d), shift1, scale1, norm1_g[l].reshape(1, d),
                          w_in[l, :, :2 * lru_w].astype(BF16), w_in[l, :, 2 * lru_w:].astype(BF16),
                          wcat, conv_w[l], conv_b[l], lru_ba[l], lru_bx[l], lru_lambda[l])
        x1 = _mix(rest, rec, xs, gate1, band_bias, w_lru_out[l].astype(BF16),
                  w_attn_out[l].astype(BF16), w_out[l].astype(BF16), seq, q_off, k_off, v_off,
                  ga_off, gb_off, kv_w, n_heads)
        if l != depth - 1:
            raise NotImplementedError("the fused final norm assumes a single layer")
        xs = _mlp(x1, shift2, scale2, gate2, norm2_g[l].reshape(1, d), final_g.reshape(1, d),
                  w_ff1[l].astype(BF16), w_ff2[l].astype(BF16), seq)
    return xs.reshape(bsz, seq, d)
```

```python
import functools
import math

import jax
import jax.numpy as jnp
from jax import lax
from jax.experimental import pallas as pl
from jax.experimental.pallas import tpu as pltpu

EPS = 1e-6
LRU_C = 8.0
LOG2E = 1.4426950408889634
ATTN_BLOCK = 128
NEG_INF = -1e30
MAX_DISTANCE = 128
LANES = 128
SUBLANES = 8
MXU_WIDTH = 256
VMEM_LIMIT = 56 * 1024 * 1024
ROW_CHUNKS = 2
IN_PROJ_ROW_CHUNKS = 4
QUERY_BLOCKS = 2

F32 = jnp.float32
BF16 = jnp.bfloat16


def _tile(n, target, align):
    best = None
    t = align
    while t <= min(n, target):
        if n % t == 0:
            best = t
        t += align
    if best is None:
        raise ValueError(f"no tile for {n} (target {target}, align {align})")
    return best


def _params(*sem):
    return pltpu.CompilerParams(dimension_semantics=sem, vmem_limit_bytes=VMEM_LIMIT)


def _sigmoid(v):
    return 1.0 / (1.0 + jnp.exp2(v * -LOG2E))


def _rms_mod(x, g, shift, scale):
    var = jnp.mean(x * x, axis=-1, keepdims=True)
    y = x * lax.rsqrt(var + EPS) * g
    return y * (1.0 + scale) + shift


def _adaln_kernel(c_ref, w_ref, b_ref, o_ref):
    c = c_ref[...]
    act = (c * _sigmoid(c)).astype(BF16)
    o_ref[...] = jnp.dot(act, w_ref[...].astype(BF16), preferred_element_type=F32) + b_ref[...]


def _adaln(c, w, b):
    bsz, d = c.shape
    n = w.shape[1]
    tn = _tile(n, 1024, LANES)
    return pl.pallas_call(
        _adaln_kernel,
        grid=(n // tn,),
        in_specs=[pl.BlockSpec((bsz, d), lambda j: (0, 0)),
                  pl.BlockSpec((d, tn), lambda j: (0, j)),
                  pl.BlockSpec((1, tn), lambda j: (0, j))],
        out_specs=pl.BlockSpec((bsz, tn), lambda j: (0, j)),
        out_shape=jax.ShapeDtypeStruct((bsz, n), F32),
        compiler_params=_params("parallel"),
        name="adaln_mod",
    )(c, w, b.reshape(1, n))


def _gelu_tanh(v):
    k1 = -2.0 * math.sqrt(2.0 / math.pi) * LOG2E
    return v / (1.0 + jnp.exp2(v * (k1 + (k1 * 0.044715) * (v * v))))


def _rglru_tile(lx_ref, slab0, gate, seq_start, w_ref, cw_ref, cb_ref, ba_ref, bx_ref, lam_ref,
                o_ref, xtail, hcar, hm_ref, conv_k, other_mxu_work):
    _, tt, cw = o_ref.shape
    n_chunks = cw // LANES
    lam = lam_ref[...]
    softplus_neg_lam = jnp.maximum(-lam, 0.0) + jnp.log(1.0 + jnp.exp(-jnp.abs(lam)))
    log2_a_coef = (-LRU_C * LOG2E) * softplus_neg_lam

    def conv(s):
        sl = slice(s * LANES, (s + 1) * LANES)
        x3 = lx_ref[slab0 + s].reshape(tt, SUBLANES, LANES)
        xs = jnp.concatenate([xtail[slab0 + s], x3], axis=0)
        xtail[slab0 + s] = xs[tt:]
        xc = cb_ref[:, sl] + cw_ref[conv_k - 1:conv_k, sl] * x3
        for k in range(conv_k - 1):
            xc = xc + cw_ref[k:k + 1, sl] * xs[k:k + tt]
        return xc

    def gates(s, xc):
        return jnp.dot(xc.reshape(tt * SUBLANES, LANES).astype(BF16), w_ref[s],
                       preferred_element_type=F32)

    pieces = iter(other_mxu_work)

    def emit():
        piece = next(pieces, None)
        if piece is not None:
            piece()

    emit()
    xcs = [conv(0)]
    zs = [gates(0, xcs[0])]
    for s in range(n_chunks):
        sl = slice(s * LANES, (s + 1) * LANES)
        if s + 1 < n_chunks:
            xcs.append(conv(s + 1))
            zs.append(gates(s + 1, xcs[s + 1]))
        emit()
        xh, z = xcs[s], zs[s]
        r = _sigmoid(z[:, :LANES] + ba_ref[:, sl]).reshape(tt, SUBLANES, LANES)
        i = _sigmoid(z[:, LANES:] + bx_ref[:, sl]).reshape(tt, SUBLANES, LANES)
        a = jnp.exp2(log2_a_coef[:, sl] * r)
        om = 1.0 - a * a
        mult = jnp.where(om > 0.0, om * lax.rsqrt(om), 0.0)
        mult = jnp.concatenate([jnp.where(seq_start, 1.0, mult[:1]), mult[1:]], axis=0)
        u = mult * (i * xh)
        h = hcar[slab0 + s]
        for t in range(tt):
            if t == tt // 2:
                emit()
            h = a[t] * h + u[t]
            hm_ref[s, t * SUBLANES:(t + 1) * SUBLANES, :] = h
        hcar[slab0 + s] = h
        for b in range(SUBLANES):
            hb = hm_ref[s, pl.ds(b, tt, stride=SUBLANES), :]
            o_ref[b, :, sl] = (hb * _gelu_tanh(gate(b, sl))).astype(o_ref.dtype)
    for piece in pieces:
        piece()


def _rglru_kernel(x_ref, gate_ref, w_ref, cw_ref, cb_ref, ba_ref, bx_ref, lam_ref, o_ref,
                  lx_ref, xtail, hcar, hm_ref, *, conv_k):
    bsz, tt, cw = x_ref.shape

    @pl.when(pl.program_id(1) == 0)
    def _():
        xtail[...] = jnp.zeros_like(xtail)
        hcar[...] = jnp.zeros_like(hcar)

    for b in range(bsz):
        xb = x_ref[b].astype(F32)
        for k in range(cw // LANES):
            lx_ref[k, pl.ds(b, tt, stride=SUBLANES), :] = xb[:, k * LANES:(k + 1) * LANES]

    _rglru_tile(lx_ref, 0, lambda b, sl: gate_ref[b, :, sl].astype(F32), pl.program_id(1) == 0,
                w_ref, cw_ref, cb_ref, ba_ref, bx_ref, lam_ref, o_ref, xtail, hcar, hm_ref,
                conv_k, ())


def _rglru(proj, wcat, conv_w, conv_b, ba, bx, lam, bsz, seq, lru_w):
    assert bsz == SUBLANES
    n = proj.shape[1]
    conv_k = conv_w.shape[0]
    tt = _tile(seq, 128, SUBLANES)
    cw = _tile(lru_w, 512, LANES)
    slabs = cw // LANES
    vec = pl.BlockSpec((1, cw), lambda c, t: (0, c))
    rec = pl.pallas_call(
        functools.partial(_rglru_kernel, conv_k=conv_k),
        grid=(lru_w // cw, seq // tt),
        in_specs=[pl.BlockSpec((bsz, tt, cw), lambda c, t: (0, t, c)),
                  pl.BlockSpec((bsz, tt, cw), lambda c, t: (0, t, lru_w // cw + c)),
                  pl.BlockSpec((slabs, LANES, 2 * LANES), lambda c, t: (c, 0, 0)),
                  pl.BlockSpec((conv_k, cw), lambda c, t: (0, c)),
                  vec, vec, vec, vec],
        out_specs=pl.BlockSpec((bsz, tt, cw), lambda c, t: (0, t, c)),
        out_shape=jax.ShapeDtypeStruct((bsz, seq, lru_w), BF16),
        scratch_shapes=[pltpu.VMEM((slabs, bsz * tt, LANES), F32),
                        pltpu.VMEM((slabs, conv_k - 1, SUBLANES, LANES), F32),
                        pltpu.VMEM((slabs, SUBLANES, LANES), F32),
                        pltpu.VMEM((slabs, bsz * tt, LANES), F32)],
        compiler_params=_params("parallel", "arbitrary"),
        name="rglru",
    )(proj.reshape(bsz, seq, n), proj.reshape(bsz, seq, n), wcat, conv_w, conv_b.reshape(1, -1),
      ba.reshape(1, -1), bx.reshape(1, -1), lam.reshape(1, -1))
    return rec.reshape(bsz * seq, lru_w)


def _in_proj_kernel(x_ref, sh_ref, sc_ref, g_ref, w_ref, o_ref, h_ref):
    j = pl.program_id(1)
    rc = x_ref.shape[0] // IN_PROJ_ROW_CHUNKS

    @pl.when(j == 0)
    def _():
        for r in range(IN_PROJ_ROW_CHUNKS):
            rows = pl.ds(r * rc, rc)
            h = _rms_mod(x_ref[rows, :], g_ref[...], sh_ref[...], sc_ref[...]).astype(BF16)
            h_ref[rows, :] = h
            o_ref[rows, :] = jnp.dot(h, w_ref[...], preferred_element_type=F32).astype(o_ref.dtype)

    @pl.when(j > 0)
    def _():
        o_ref[...] = jnp.dot(h_ref[...], w_ref[...],
                             preferred_element_type=F32).astype(o_ref.dtype)


def _in_proj(x2d, shift, scale, g, w, seq):
    m, d = x2d.shape
    n = w.shape[1]
    tm = _tile(seq, 1024, SUBLANES * IN_PROJ_ROW_CHUNKS)
    tn = _tile(n, 1536, LANES)
    per_b = seq // tm
    row = lambda i, j: (i // per_b, 0, 0)
    return pl.pallas_call(
        _in_proj_kernel,
        grid=(m // tm, n // tn),
        in_specs=[pl.BlockSpec((tm, d), lambda i, j: (i, 0)),
                  pl.BlockSpec((None, 1, d), row),
                  pl.BlockSpec((None, 1, d), row),
                  pl.BlockSpec((1, d), lambda i, j: (0, 0)),
                  pl.BlockSpec((d, tn), lambda i, j: (0, j))],
        out_specs=pl.BlockSpec((tm, tn), lambda i, j: (i, j)),
        out_shape=jax.ShapeDtypeStruct((m, n), BF16),
        scratch_shapes=[pltpu.VMEM((tm, d), BF16)],
        compiler_params=_params("parallel", "arbitrary"),
        name="in_proj",
    )(x2d, shift, scale, g, w)


def _t5_bucket(rel, n_buckets):
    max_exact = n_buckets // 2
    relf = jnp.maximum(rel, 1).astype(F32)
    large = max_exact + (jnp.log(relf / max_exact) / math.log(MAX_DISTANCE / max_exact)
                         * (n_buckets - max_exact)).astype(jnp.int32)
    large = jnp.minimum(large, n_buckets - 1)
    return jnp.where(rel < max_exact, rel, large)


def _band_bias_kernel(rb_ref, sink_ref, bucket_ref, o_ref, *, pairs, n_buckets):
    kv = pl.program_id(0)
    blk = ATTN_BLOCK
    bucket = bucket_ref[...]
    qi = lax.broadcasted_iota(jnp.int32, (blk, 2 * blk), 0)
    ki = lax.broadcasted_iota(jnp.int32, (blk, 2 * blk), 1)
    rel = qi + blk - ki
    valid = (rel >= 0) & (rel < blk)
    for p in range(pairs):
        for par in range(2):
            h = (kv * pairs + p) * 2 + par
            bias = jnp.zeros((blk, 2 * blk), F32)
            for b in range(n_buckets):
                bias = jnp.where(bucket == b, rb_ref[b, h], bias)
            sink = sink_ref[h]
            rows, cols = slice(p * blk, (p + 1) * blk), slice(par * 2 * blk, (par + 1) * 2 * blk)
            o_ref[rows, cols] = jnp.where(ki == 0, sink, jnp.where(valid, bias, NEG_INF))


def _band_bias(rel_bias, sinks, n_kv):
    n_buckets, n_heads = rel_bias.shape
    pairs = n_heads // n_kv // 2
    blk = ATTN_BLOCK
    qi = jnp.arange(blk)[:, None]
    ki = jnp.arange(2 * blk)[None, :]
    bucket = _t5_bucket(jnp.maximum(qi + blk - ki, 0), n_buckets)
    smem = pl.BlockSpec(memory_space=pltpu.SMEM)
    return pl.pallas_call(
        functools.partial(_band_bias_kernel, pairs=pairs, n_buckets=n_buckets),
        grid=(n_kv,),
        in_specs=[smem, smem, pl.BlockSpec((blk, 2 * blk), lambda kv: (0, 0))],
        out_specs=pl.BlockSpec((None, pairs * blk, 4 * blk), lambda kv: (kv, 0, 0)),
        out_shape=jax.ShapeDtypeStruct((n_kv, pairs * blk, 4 * blk), F32),
        compiler_params=_params("parallel"),
        name="band_bias",
    )(rel_bias, sinks, bucket)


def _swap_lane_halves(v):
    u = pltpu.bitcast(v, jnp.uint32)
    return pltpu.bitcast(pltpu.roll(u, LANES // 2, axis=1), BF16)


def _mix_kernel(*refs, n_kv, pairs, scale, per_b):
    q_ref, kp_ref, kc_ref, vp_ref, vc_ref, bm_ref, rec_ref = refs[:7]
    ga_refs, gb_refs = refs[7:7 + n_kv], refs[7 + n_kv:7 + 2 * n_kv]
    x_ref, gate_ref, wl_ref, wa_ref, wo_ref, o_ref, att_ref, ya_ref, m_ref = refs[7 + 2 * n_kv:]
    blk = ATTN_BLOCK
    d = o_ref.shape[1]
    cc = d // n_kv
    fold = math.log2(scale).is_integer()
    lane = lax.broadcasted_iota(jnp.int32, (blk, LANES), 1)
    key = lax.broadcasted_iota(jnp.int32, (blk, LANES), 0)
    low = jnp.where(lane < LANES // 2, 1.0, 0.0).astype(BF16)
    high = jnp.where(lane < LANES // 2, 0.0, 1.0).astype(BF16)
    not_key0 = jnp.where(key == 0, 0.0, 1.0).astype(BF16)
    ones_blk = jnp.concatenate([low, low, high, high], axis=0)
    key_col = lax.broadcasted_iota(jnp.int32, (1, 4 * blk), 1) % (2 * blk)
    seq_start = pl.program_id(0) % per_b == 0
    start_mask = jnp.where(seq_start & (key_col >= 1) & (key_col < blk), NEG_INF, 0.0)

    def halves(t, kv):
        s = _swap_lane_halves(t)
        lo, hi = (t, s) if kv % 2 == 0 else (s, t)
        return lo * low, hi * high

    def window(prev, cur):
        return jnp.concatenate([prev[0] * not_key0, cur[0], prev[1] * not_key0, cur[1]], axis=0)

    for kv in range(n_kv):
        sl = slice((kv // 2) * LANES, (kv // 2 + 1) * LANES)
        kb = [halves(kp_ref[:, sl], kv)] + [halves(kc_ref[t * blk:(t + 1) * blk, sl], kv)
                                            for t in range(QUERY_BLOCKS)]
        vb = [halves(vp_ref[:, sl], kv)] + [halves(vc_ref[t * blk:(t + 1) * blk, sl], kv)
                                            for t in range(QUERY_BLOCKS)]
        for t in range(QUERY_BLOCKS):
            rows = slice(t * blk, (t + 1) * blk)
            kblk = window(kb[t], kb[t + 1])
            vaug = jnp.concatenate([window(vb[t], vb[t + 1]), ones_blk], axis=1)
            q = jnp.concatenate(
                [q_ref[rows, (kv * pairs + p) * LANES:(kv * pairs + p + 1) * LANES]
                 for p in range(pairs)], axis=0)
            if fold:
                q = q * scale
            s = lax.dot_general(q, kblk, (((1,), (1,)), ((), ())), preferred_element_type=F32)
            if not fold:
                s = s * scale
            s = s + bm_ref[kv]
            if t == 0:
                s = s + start_mask
            es = []
            for par in range(2):
                sp = s[:, par * 2 * blk:(par + 1) * 2 * blk]
                es.append(jnp.exp(sp - jnp.max(sp, axis=-1, keepdims=True)).astype(BF16))
            o = jnp.dot(jnp.concatenate(es, axis=1), vaug, preferred_element_type=F32)
            res = (o[:, :LANES] / o[:, LANES:]).astype(BF16)
            for p in range(pairs):
                att_ref[rows, (kv * pairs + p) * LANES:(kv * pairs + p + 1) * LANES] = (
                    res[p * blk:(p + 1) * blk])
        cols = slice(kv * cc, (kv + 1) * cc)
        ya = jnp.dot(rec_ref[...], wl_ref[:, cols], preferred_element_type=F32)
        ya_ref[:, cols] = _sigmoid(ga_refs[kv][...].astype(F32)) * ya

    for c in range(n_kv):
        cols = slice(c * cc, (c + 1) * cc)
        yb = jnp.dot(att_ref[...], wa_ref[:, cols], preferred_element_type=F32)
        gb = _sigmoid(gb_refs[c][...].astype(F32))
        m_ref[:, cols] = (ya_ref[:, cols] + gb * yb).astype(BF16)
    for c in range(n_kv):
        cols = slice(c * cc, (c + 1) * cc)
        y = jnp.dot(m_ref[...], wo_ref[:, cols], preferred_element_type=F32)
        o_ref[:, cols] = x_ref[:, cols] + gate_ref[:, cols] * y


def _mix(rest, rec, x2d, gate, band_bias, w_lru, w_att, w_out, seq, q_off, k_off, v_off, ga_off,
         gb_off, kv_w, n_heads):
    m, d = x2d.shape
    attn_w = w_att.shape[0]
    hd = attn_w // n_heads
    n_kv = kv_w // hd
    pairs = n_heads // n_kv // 2
    blk = ATTN_BLOCK
    tm = QUERY_BLOCKS * blk
    per_b = seq // tm
    cc = d // n_kv
    assert 2 * hd == LANES and n_heads == 2 * pairs * n_kv and kv_w % LANES == 0
    assert q_off % attn_w == 0 and k_off % kv_w == 0 and v_off % kv_w == 0
    assert seq % tm == 0 and cc % LANES == 0 and ga_off % cc == 0 and gb_off % cc == 0
    kb, vb = k_off // kv_w, v_off // kv_w
    prev = lambda col: (lambda i: (jnp.maximum(QUERY_BLOCKS * i - 1, 0), col))
    col_chunk = lambda off: [pl.BlockSpec((tm, cc), functools.partial(lambda c, i: (i, c),
                                                                      off // cc + c))
                             for c in range(n_kv)]
    resident = lambda a: pl.BlockSpec(a.shape, lambda i: (0,) * a.ndim,
                                      pipeline_mode=pl.Buffered(1))
    kernel = functools.partial(_mix_kernel, n_kv=n_kv, pairs=pairs, scale=hd ** -0.5, per_b=per_b)
    return pl.pallas_call(
        kernel,
        grid=(m // tm,),
        in_specs=[pl.BlockSpec((tm, attn_w), lambda i: (i, q_off // attn_w)),
                  pl.BlockSpec((blk, kv_w), prev(kb)),
                  pl.BlockSpec((tm, kv_w), lambda i: (i, kb)),
                  pl.BlockSpec((blk, kv_w), prev(vb)),
                  pl.BlockSpec((tm, kv_w), lambda i: (i, vb)),
                  resident(band_bias),
                  pl.BlockSpec((tm, rec.shape[1]), lambda i: (i, 0))]
                 + col_chunk(ga_off) + col_chunk(gb_off)
                 + [pl.BlockSpec((tm, d), lambda i: (i, 0)),
                    pl.BlockSpec((None, 1, d), lambda i: (i // per_b, 0, 0)),
                    resident(w_lru), resident(w_att), resident(w_out)],
        out_specs=pl.BlockSpec((tm, d), lambda i: (i, 0)),
        out_shape=jax.ShapeDtypeStruct((m, d), F32),
        scratch_shapes=[pltpu.VMEM((tm, attn_w), BF16), pltpu.VMEM((tm, d), F32),
                        pltpu.VMEM((tm, d), BF16)],
        compiler_params=_params("parallel"),
        name="mix",
    )(rest, rest, rest, rest, rest, band_bias, rec, *([rest] * (2 * n_kv)), x2d, gate,
      w_lru, w_att, w_out)


def _mlp_kernel(x_ref, sh_ref, sc_ref, gate_ref, g2_ref, gf_ref, w1_ref, w2_ref, o_ref,
                h_ref, ff_ref, *, nf):
    f = pl.program_id(1)
    rc = x_ref.shape[0] // ROW_CHUNKS

    def up(rows, slot):
        ff = jnp.dot(h_ref[rows, :], w1_ref[...], preferred_element_type=F32)
        ff_ref[slot, rows, :] = jnp.square(jnp.maximum(ff, 0.0)).astype(BF16)

    def down(rows, slot):
        return jnp.dot(ff_ref[slot, rows, :], w2_ref[...], preferred_element_type=F32)

    @pl.when(f == 0)
    def _():
        o_ref[...] = jnp.zeros_like(o_ref)
        for r in range(ROW_CHUNKS):
            rows = pl.ds(r * rc, rc)
            h_ref[rows, :] = _rms_mod(x_ref[rows, :], g2_ref[...], sh_ref[...],
                                      sc_ref[...]).astype(BF16)
            up(rows, 0)

    @pl.when((f > 0) & (f < nf))
    def _():
        o_ref[...] += down(slice(None), (f - 1) % 2)
        up(slice(None), f % 2)

    @pl.when(f == nf)
    def _():
        for r in range(ROW_CHUNKS):
            rows = pl.ds(r * rc, rc)
            x2 = x_ref[rows, :] + gate_ref[...] * (o_ref[rows, :] + down(rows, (nf - 1) % 2))
            var = jnp.mean(x2 * x2, axis=-1, keepdims=True)
            o_ref[rows, :] = x2 * lax.rsqrt(var + EPS) * gf_ref[...]


def _mlp(x1, shift, scale, gate, g2, gf, w1, w2, seq):
    m, d = x1.shape
    dff = w1.shape[1]
    tm = _tile(seq, 512, SUBLANES * ROW_CHUNKS)
    tf = _tile(dff, 1024, LANES)
    nf = dff // tf
    per_b = seq // tm
    row = lambda i, f: (i // per_b, 0, 0)
    const = lambda i, f: (0, 0)
    return pl.pallas_call(
        functools.partial(_mlp_kernel, nf=nf),
        grid=(m // tm, nf + 1),
        in_specs=[pl.BlockSpec((tm, d), lambda i, f: (i, 0)),
                  pl.BlockSpec((None, 1, d), row),
                  pl.BlockSpec((None, 1, d), row),
                  pl.BlockSpec((None, 1, d), row),
                  pl.BlockSpec((1, d), const),
                  pl.BlockSpec((1, d), const),
                  pl.BlockSpec((d, tf), lambda i, f: (0, jnp.minimum(f, nf - 1))),
                  pl.BlockSpec((tf, d), lambda i, f: (jnp.maximum(f - 1, 0), 0))],
        out_specs=pl.BlockSpec((tm, d), lambda i, f: (i, 0)),
        out_shape=jax.ShapeDtypeStruct((m, d), F32),
        scratch_shapes=[pltpu.VMEM((tm, d), BF16), pltpu.VMEM((2, tm, tf), BF16)],
        compiler_params=_params("parallel", "arbitrary"),
        name="mlp",
    )(x1, shift, scale, gate, g2, gf, w1, w2)


def kernel(x, c, w_ada, b_ada, norm1_g, w_in, conv_w, conv_b, lru_wa, lru_ba, lru_wx, lru_bx,
           lru_lambda, w_lru_out, w_attn_out, attn_sinks, rel_bias, w_out, norm2_g, w_ff1, w_ff2,
           final_g):
    bsz, seq, d = x.shape
    depth = w_in.shape[0]
    lru_w = w_lru_out.shape[1]
    attn_w = w_attn_out.shape[1]
    kv_w = (w_in.shape[2] - 2 * lru_w - attn_w - 2 * d) // 2
    n_heads = attn_sinks.shape[1]
    n_kv = kv_w // (attn_w // n_heads)
    q_off = 2 * lru_w
    k_off = q_off + attn_w
    v_off = k_off + kv_w
    ga_off = v_off + kv_w
    gb_off = ga_off + d

    xs = x.reshape(bsz * seq, d)
    for l in range(depth):
        band_bias = _band_bias(rel_bias, attn_sinks[l], n_kv)
        mod = _adaln(c, w_ada[l], b_ada[l])
        shift1, scale1, gate1, shift2, scale2, gate2 = [
            t.reshape(bsz, 1, d) for t in jnp.split(mod, 6, axis=-1)]

        rest = _in_proj(xs, shift1, scale1, norm1_g[l].reshape(1, d), w_in[l].astype(BF16), seq)
        wcat = jnp.concatenate([lru_wa[l], lru_wx[l]], axis=-1).astype(BF16)
        rec = _rglru(rest, wcat, conv_w[l], conv_b[l], lru_ba[l], lru_bx[l], lru_lambda[l],
                     bsz, seq, lru_w)
        x1 = _mix(rest, rec, xs, gate1, band_bias, w_lru_out[l].astype(BF16),
                  w_attn_out[l].astype(BF16), w_out[l].astype(BF16), seq, q_off, k_off, v_off,
                  ga_off, gb_off, kv_w, n_heads)
        if l != depth - 1:
            raise NotImplementedError("the fused final norm assumes a single layer")
        xs = _mlp(x1, shift2, scale2, gate2, norm2_g[l].reshape(1, d), final_g.reshape(1, d),
                  w_ff1[l].astype(BF16), w_ff2[l].astype(BF16), seq)
    return xs.reshape(bsz, seq, d)
```

```python
import functools
import math

import jax
import jax.numpy as jnp
from jax import lax
from jax.experimental import pallas as pl
from jax.experimental.pallas import tpu as pltpu

EPS = 1e-6
LRU_C = 8.0
LOG2E = 1.4426950408889634
ATTN_BLOCK = 128
NEG_INF = -1e30
MAX_DISTANCE = 128
LANES = 128
SUBLANES = 8
MXU_WIDTH = 256
VMEM_LIMIT = 56 * 1024 * 1024
ROW_CHUNKS = 2
IN_PROJ_ROW_CHUNKS = 4
QUERY_BLOCKS = 2

F32 = jnp.float32
BF16 = jnp.bfloat16


def _tile(n, target, align):
    best = None
    t = align
    while t <= min(n, target):
        if n % t == 0:
            best = t
        t += align
    if best is None:
        raise ValueError(f"no tile for {n} (target {target}, align {align})")
    return best


def _params(*sem):
    return pltpu.CompilerParams(dimension_semantics=sem, vmem_limit_bytes=VMEM_LIMIT)


def _sigmoid(v):
    return 1.0 / (1.0 + jnp.exp2(v * -LOG2E))


def _rms_mod(x, g, shift, scale):
    var = jnp.mean(x * x, axis=-1, keepdims=True)
    y = x * lax.rsqrt(var + EPS) * g
    return y * (1.0 + scale) + shift


def _adaln_kernel(c_ref, w_ref, b_ref, o_ref):
    c = c_ref[...]
    act = (c * _sigmoid(c)).astype(BF16)
    o_ref[...] = jnp.dot(act, w_ref[...].astype(BF16), preferred_element_type=F32) + b_ref[...]


def _adaln(c, w, b):
    bsz, d = c.shape
    n = w.shape[1]
    tn = _tile(n, 1024, LANES)
    return pl.pallas_call(
        _adaln_kernel,
        grid=(n // tn,),
        in_specs=[pl.BlockSpec((bsz, d), lambda j: (0, 0)),
                  pl.BlockSpec((d, tn), lambda j: (0, j)),
                  pl.BlockSpec((1, tn), lambda j: (0, j))],
        out_specs=pl.BlockSpec((bsz, tn), lambda j: (0, j)),
        out_shape=jax.ShapeDtypeStruct((bsz, n), F32),
        compiler_params=_params("parallel"),
        name="adaln_mod",
    )(c, w, b.reshape(1, n))


def _gelu_tanh(v):
    k1 = -2.0 * math.sqrt(2.0 / math.pi) * LOG2E
    return v / (1.0 + jnp.exp2(v * (k1 + (k1 * 0.044715) * (v * v))))


def _rglru_tile(lx_ref, slab0, gate, seq_start, w_ref, cw_ref, cb_ref, ba_ref, bx_ref, lam_ref,
                o_ref, xtail, hcar, hm_ref, conv_k, other_mxu_work):
    _, tt, cw = o_ref.shape
    n_chunks = cw // LANES
    lam = lam_ref[...]
    softplus_neg_lam = jnp.maximum(-lam, 0.0) + jnp.log(1.0 + jnp.exp(-jnp.abs(lam)))
    log2_a_coef = (-LRU_C * LOG2E) * softplus_neg_lam

    def conv(s):
        sl = slice(s * LANES, (s + 1) * LANES)
        x3 = lx_ref[slab0 + s].reshape(tt, SUBLANES, LANES)
        xs = jnp.concatenate([xtail[slab0 + s], x3], axis=0)
        xtail[slab0 + s] = xs[tt:]
        xc = cb_ref[:, sl] + cw_ref[conv_k - 1:conv_k, sl] * x3
        for k in range(conv_k - 1):
            xc = xc + cw_ref[k:k + 1, sl] * xs[k:k + tt]
        return xc

    def gates(s, xc):
        return jnp.dot(xc.reshape(tt * SUBLANES, LANES).astype(BF16), w_ref[s],
                       preferred_element_type=F32)

    pieces = iter(other_mxu_work)

    def emit():
        piece = next(pieces, None)
        if piece is not None:
            piece()

    emit()
    xcs = [conv(0)]
    zs = [gates(0, xcs[0])]
    for s in range(n_chunks):
        sl = slice(s * LANES, (s + 1) * LANES)
        if s + 1 < n_chunks:
            xcs.append(conv(s + 1))
            zs.append(gates(s + 1, xcs[s + 1]))
        emit()
        xh, z = xcs[s], zs[s]
        r = _sigmoid(z[:, :LANES] + ba_ref[:, sl]).reshape(tt, SUBLANES, LANES)
        i = _sigmoid(z[:, LANES:] + bx_ref[:, sl]).reshape(tt, SUBLANES, LANES)
        a = jnp.exp2(log2_a_coef[:, sl] * r)
        om = 1.0 - a * a
        mult = jnp.where(om > 0.0, om * lax.rsqrt(om), 0.0)
        mult = jnp.concatenate([jnp.where(seq_start, 1.0, mult[:1]), mult[1:]], axis=0)
        u = mult * (i * xh)
        h = hcar[slab0 + s]
        for t in range(tt):
            if t == tt // 2:
                emit()
            h = a[t] * h + u[t]
            hm_ref[s, t * SUBLANES:(t + 1) * SUBLANES, :] = h
        hcar[slab0 + s] = h
        for b in range(SUBLANES):
            hb = hm_ref[s, pl.ds(b, tt, stride=SUBLANES), :]
            o_ref[b, :, sl] = (hb * _gelu_tanh(gate(b, sl))).astype(o_ref.dtype)
    for piece in pieces:
        piece()


def _rglru_kernel(x_ref, gate_ref, w_ref, cw_ref, cb_ref, ba_ref, bx_ref, lam_ref, o_ref,
                  lx_ref, xtail, hcar, hm_ref, *, conv_k):
    bsz, tt, cw = x_ref.shape

    @pl.when(pl.program_id(1) == 0)
    def _():
        xtail[...] = jnp.zeros_like(xtail)
        hcar[...] = jnp.zeros_like(hcar)

    for b in range(bsz):
        xb = x_ref[b].astype(F32)
        for k in range(cw // LANES):
            lx_ref[k, pl.ds(b, tt, stride=SUBLANES), :] = xb[:, k * LANES:(k + 1) * LANES]

    _rglru_tile(lx_ref, 0, lambda b, sl: gate_ref[b, :, sl].astype(F32), pl.program_id(1) == 0,
                w_ref, cw_ref, cb_ref, ba_ref, bx_ref, lam_ref, o_ref, xtail, hcar, hm_ref,
                conv_k, ())


def _rglru(proj, wcat, conv_w, conv_b, ba, bx, lam, bsz, seq, lru_w):
    assert bsz == SUBLANES
    n = proj.shape[1]
    conv_k = conv_w.shape[0]
    tt = _tile(seq, 128, SUBLANES)
    cw = _tile(lru_w, 512, LANES)
    slabs = cw // LANES
    vec = pl.BlockSpec((1, cw), lambda c, t: (0, c))
    rec = pl.pallas_call(
        functools.partial(_rglru_kernel, conv_k=conv_k),
        grid=(lru_w // cw, seq // tt),
        in_specs=[pl.BlockSpec((bsz, tt, cw), lambda c, t: (0, t, c)),
                  pl.BlockSpec((bsz, tt, cw), lambda c, t: (0, t, lru_w // cw + c)),
                  pl.BlockSpec((slabs, LANES, 2 * LANES), lambda c, t: (c, 0, 0)),
                  pl.BlockSpec((conv_k, cw), lambda c, t: (0, c)),
                  vec, vec, vec, vec],
        out_specs=pl.BlockSpec((bsz, tt, cw), lambda c, t: (0, t, c)),
        out_shape=jax.ShapeDtypeStruct((bsz, seq, lru_w), BF16),
        scratch_shapes=[pltpu.VMEM((slabs, bsz * tt, LANES), F32),
                        pltpu.VMEM((slabs, conv_k - 1, SUBLANES, LANES), F32),
                        pltpu.VMEM((slabs, SUBLANES, LANES), F32),
                        pltpu.VMEM((slabs, bsz * tt, LANES), F32)],
        compiler_params=_params("parallel", "arbitrary"),
        name="rglru",
    )(proj.reshape(bsz, seq, n), proj.reshape(bsz, seq, n), wcat, conv_w, conv_b.reshape(1, -1),
      ba.reshape(1, -1), bx.reshape(1, -1), lam.reshape(1, -1))
    return rec.reshape(bsz * seq, lru_w)


def _in_proj_kernel(x_ref, sh_ref, sc_ref, g_ref, w_ref, o_ref, h_ref):
    j = pl.program_id(1)
    rc = x_ref.shape[0] // IN_PROJ_ROW_CHUNKS

    @pl.when(j == 0)
    def _():
        for r in range(IN_PROJ_ROW_CHUNKS):
            rows = pl.ds(r * rc, rc)
            h = _rms_mod(x_ref[rows, :], g_ref[...], sh_ref[...], sc_ref[...]).astype(BF16)
            h_ref[rows, :] = h
            o_ref[rows, :] = jnp.dot(h, w_ref[...], preferred_element_type=F32).astype(o_ref.dtype)

    @pl.when(j > 0)
    def _():
        o_ref[...] = jnp.dot(h_ref[...], w_ref[...],
                             preferred_element_type=F32).astype(o_ref.dtype)


def _in_proj(x2d, shift, scale, g, w, seq):
    m, d = x2d.shape
    n = w.shape[1]
    tm = _tile(seq, 1024, SUBLANES * IN_PROJ_ROW_CHUNKS)
    tn = _tile(n, 1536, LANES)
    per_b = seq // tm
    row = lambda i, j: (i // per_b, 0, 0)
    return pl.pallas_call(
        _in_proj_kernel,
        grid=(m // tm, n // tn),
        in_specs=[pl.BlockSpec((tm, d), lambda i, j: (i, 0)),
                  pl.BlockSpec((None, 1, d), row),
                  pl.BlockSpec((None, 1, d), row),
                  pl.BlockSpec((1, d), lambda i, j: (0, 0)),
                  pl.BlockSpec((None, d, tn), lambda i, j: (j, 0, 0))],
        out_specs=pl.BlockSpec((tm, tn), lambda i, j: (i, j)),
        out_shape=jax.ShapeDtypeStruct((m, n), BF16),
        scratch_shapes=[pltpu.VMEM((tm, d), BF16)],
        compiler_params=_params("parallel", "arbitrary"),
        name="in_proj",
    )(x2d, shift, scale, g, w.reshape(d, n // tn, tn).transpose(1, 0, 2))


def _t5_bucket(rel, n_buckets):
    max_exact = n_buckets // 2
    relf = jnp.maximum(rel, 1).astype(F32)
    large = max_exact + (jnp.log(relf / max_exact) / math.log(MAX_DISTANCE / max_exact)
                         * (n_buckets - max_exact)).astype(jnp.int32)
    large = jnp.minimum(large, n_buckets - 1)
    return jnp.where(rel < max_exact, rel, large)


def _band_bias_kernel(rb_ref, sink_ref, bucket_ref, o_ref, *, pairs, n_buckets):
    kv = pl.program_id(0)
    blk = ATTN_BLOCK
    bucket = bucket_ref[...]
    qi = lax.broadcasted_iota(jnp.int32, (blk, 2 * blk), 0)
    ki = lax.broadcasted_iota(jnp.int32, (blk, 2 * blk), 1)
    rel = qi + blk - ki
    valid = (rel >= 0) & (rel < blk)
    for p in range(pairs):
        for par in range(2):
            h = (kv * pairs + p) * 2 + par
            bias = jnp.zeros((blk, 2 * blk), F32)
            for b in range(n_buckets):
                bias = jnp.where(bucket == b, rb_ref[b, h], bias)
            sink = sink_ref[h]
            rows, cols = slice(p * blk, (p + 1) * blk), slice(par * 2 * blk, (par + 1) * 2 * blk)
            o_ref[rows, cols] = jnp.where(ki == 0, sink, jnp.where(valid, bias, NEG_INF))


def _band_bias(rel_bias, sinks, n_kv):
    n_buckets, n_heads = rel_bias.shape
    pairs = n_heads // n_kv // 2
    blk = ATTN_BLOCK
    qi = jnp.arange(blk)[:, None]
    ki = jnp.arange(2 * blk)[None, :]
    bucket = _t5_bucket(jnp.maximum(qi + blk - ki, 0), n_buckets)
    smem = pl.BlockSpec(memory_space=pltpu.SMEM)
    return pl.pallas_call(
        functools.partial(_band_bias_kernel, pairs=pairs, n_buckets=n_buckets),
        grid=(n_kv,),
        in_specs=[smem, smem, pl.BlockSpec((blk, 2 * blk), lambda kv: (0, 0))],
        out_specs=pl.BlockSpec((None, pairs * blk, 4 * blk), lambda kv: (kv, 0, 0)),
        out_shape=jax.ShapeDtypeStruct((n_kv, pairs * blk, 4 * blk), F32),
        compiler_params=_params("parallel"),
        name="band_bias",
    )(rel_bias, sinks, bucket)


def _swap_lane_halves(v):
    u = pltpu.bitcast(v, jnp.uint32)
    return pltpu.bitcast(pltpu.roll(u, LANES // 2, axis=1), BF16)


def _mix_kernel(*refs, n_kv, pairs, scale, per_b):
    q_ref, kp_ref, kc_ref, vp_ref, vc_ref, bm_ref, rec_ref = refs[:7]
    ga_refs, gb_refs = refs[7:7 + n_kv], refs[7 + n_kv:7 + 2 * n_kv]
    x_ref, gate_ref, wl_ref, wa_ref, wo_ref, o_ref, att_ref, ya_ref, m_ref = refs[7 + 2 * n_kv:]
    blk = ATTN_BLOCK
    d = o_ref.shape[1]
    cc = d // n_kv
    fold = math.log2(scale).is_integer()
    lane = lax.broadcasted_iota(jnp.int32, (blk, LANES), 1)
    key = lax.broadcasted_iota(jnp.int32, (blk, LANES), 0)
    low = jnp.where(lane < LANES // 2, 1.0, 0.0).astype(BF16)
    high = jnp.where(lane < LANES // 2, 0.0, 1.0).astype(BF16)
    not_key0 = jnp.where(key == 0, 0.0, 1.0).astype(BF16)
    ones_blk = jnp.concatenate([low, low, high, high], axis=0)
    key_col = lax.broadcasted_iota(jnp.int32, (1, 4 * blk), 1) % (2 * blk)
    seq_start = pl.program_id(0) % per_b == 0
    start_mask = jnp.where(seq_start & (key_col >= 1) & (key_col < blk), NEG_INF, 0.0)

    def halves(t, kv):
        s = _swap_lane_halves(t)
        lo, hi = (t, s) if kv % 2 == 0 else (s, t)
        return lo * low, hi * high

    def window(prev, cur):
        return jnp.concatenate([prev[0] * not_key0, cur[0], prev[1] * not_key0, cur[1]], axis=0)

    for kv in range(n_kv):
        sl = slice((kv // 2) * LANES, (kv // 2 + 1) * LANES)
        kb = [halves(kp_ref[:, sl], kv)] + [halves(kc_ref[t * blk:(t + 1) * blk, sl], kv)
                                            for t in range(QUERY_BLOCKS)]
        vb = [halves(vp_ref[:, sl], kv)] + [halves(vc_ref[t * blk:(t + 1) * blk, sl], kv)
                                            for t in range(QUERY_BLOCKS)]
        for t in range(QUERY_BLOCKS):
            rows = slice(t * blk, (t + 1) * blk)
            kblk = window(kb[t], kb[t + 1])
            vaug = jnp.concatenate([window(vb[t], vb[t + 1]), ones_blk], axis=1)
            q = jnp.concatenate(
                [q_ref[rows, (kv * pairs + p) * LANES:(kv * pairs + p + 1) * LANES]
                 for p in range(pairs)], axis=0)
            if fold:
                q = q * scale
            s = lax.dot_general(q, kblk, (((1,), (1,)), ((), ())), preferred_element_type=F32)
            if not fold:
                s = s * scale
            s = s + bm_ref[kv]
            if t == 0:
                s = s + start_mask
            es = []
            for par in range(2):
                sp = s[:, par * 2 * blk:(par + 1) * 2 * blk]
                es.append(jnp.exp(sp - jnp.max(sp, axis=-1, keepdims=True)).astype(BF16))
            o = jnp.dot(jnp.concatenate(es, axis=1), vaug, preferred_element_type=F32)
            res = (o[:, :LANES] / o[:, LANES:]).astype(BF16)
            for p in range(pairs):
                att_ref[rows, (kv * pairs + p) * LANES:(kv * pairs + p + 1) * LANES] = (
                    res[p * blk:(p + 1) * blk])
        cols = slice(kv * cc, (kv + 1) * cc)
        ya = jnp.dot(rec_ref[...], wl_ref[:, cols], preferred_element_type=F32)
        ya_ref[:, cols] = _sigmoid(ga_refs[kv][...].astype(F32)) * ya

    for c in range(n_kv):
        cols = slice(c * cc, (c + 1) * cc)
        yb = jnp.dot(att_ref[...], wa_ref[:, cols], preferred_element_type=F32)
        gb = _sigmoid(gb_refs[c][...].astype(F32))
        m_ref[:, cols] = (ya_ref[:, cols] + gb * yb).astype(BF16)
    for c in range(n_kv):
        cols = slice(c * cc, (c + 1) * cc)
        y = jnp.dot(m_ref[...], wo_ref[:, cols], preferred_element_type=F32)
        o_ref[:, cols] = x_ref[:, cols] + gate_ref[:, cols] * y


def _mix(rest, rec, x2d, gate, band_bias, w_lru, w_att, w_out, seq, q_off, k_off, v_off, ga_off,
         gb_off, kv_w, n_heads):
    m, d = x2d.shape
    attn_w = w_att.shape[0]
    hd = attn_w // n_heads
    n_kv = kv_w // hd
    pairs = n_heads // n_kv // 2
    blk = ATTN_BLOCK
    tm = QUERY_BLOCKS * blk
    per_b = seq // tm
    cc = d // n_kv
    assert 2 * hd == LANES and n_heads == 2 * pairs * n_kv and kv_w % LANES == 0
    assert q_off % attn_w == 0 and k_off % kv_w == 0 and v_off % kv_w == 0
    assert seq % tm == 0 and cc % LANES == 0 and ga_off % cc == 0 and gb_off % cc == 0
    kb, vb = k_off // kv_w, v_off // kv_w
    prev = lambda col: (lambda i: (jnp.maximum(QUERY_BLOCKS * i - 1, 0), col))
    col_chunk = lambda off: [pl.BlockSpec((tm, cc), functools.partial(lambda c, i: (i, c),
                                                                      off // cc + c))
                             for c in range(n_kv)]
    resident = lambda a: pl.BlockSpec(a.shape, lambda i: (0,) * a.ndim,
                                      pipeline_mode=pl.Buffered(1))
    kernel = functools.partial(_mix_kernel, n_kv=n_kv, pairs=pairs, scale=hd ** -0.5, per_b=per_b)
    return pl.pallas_call(
        kernel,
        grid=(m // tm,),
        in_specs=[pl.BlockSpec((tm, attn_w), lambda i: (i, q_off // attn_w)),
                  pl.BlockSpec((blk, kv_w), prev(kb)),
                  pl.BlockSpec((tm, kv_w), lambda i: (i, kb)),
                  pl.BlockSpec((blk, kv_w), prev(vb)),
                  pl.BlockSpec((tm, kv_w), lambda i: (i, vb)),
                  resident(band_bias),
                  pl.BlockSpec((tm, rec.shape[1]), lambda i: (i, 0))]
                 + col_chunk(ga_off) + col_chunk(gb_off)
                 + [pl.BlockSpec((tm, d), lambda i: (i, 0)),
                    pl.BlockSpec((None, 1, d), lambda i: (i // per_b, 0, 0)),
                    resident(w_lru), resident(w_att), resident(w_out)],
        out_specs=pl.BlockSpec((tm, d), lambda i: (i, 0)),
        out_shape=jax.ShapeDtypeStruct((m, d), F32),
        scratch_shapes=[pltpu.VMEM((tm, attn_w), BF16), pltpu.VMEM((tm, d), F32),
                        pltpu.VMEM((tm, d), BF16)],
        compiler_params=_params("parallel"),
        name="mix",
    )(rest, rest, rest, rest, rest, band_bias, rec, *([rest] * (2 * n_kv)), x2d, gate,
      w_lru, w_att, w_out)


def _mlp_kernel(x_ref, sh_ref, sc_ref, gate_ref, g2_ref, gf_ref, w1_ref, w2_ref, o_ref,
                h_ref, ff_ref, *, nf):
    f = pl.program_id(1)
    rc = x_ref.shape[0] // ROW_CHUNKS

    def up(rows, slot):
        ff = jnp.dot(h_ref[rows, :], w1_ref[...], preferred_element_type=F32)
        ff_ref[slot, rows, :] = jnp.square(jnp.maximum(ff, 0.0)).astype(BF16)

    def down(rows, slot):
        return jnp.dot(ff_ref[slot, rows, :], w2_ref[...], preferred_element_type=F32)

    @pl.when(f == 0)
    def _():
        o_ref[...] = jnp.zeros_like(o_ref)
        for r in range(ROW_CHUNKS):
            rows = pl.ds(r * rc, rc)
            h_ref[rows, :] = _rms_mod(x_ref[rows, :], g2_ref[...], sh_ref[...],
                                      sc_ref[...]).astype(BF16)
            up(rows, 0)

    @pl.when((f > 0) & (f < nf))
    def _():
        o_ref[...] += down(slice(None), (f - 1) % 2)
        up(slice(None), f % 2)

    @pl.when(f == nf)
    def _():
        for r in range(ROW_CHUNKS):
            rows = pl.ds(r * rc, rc)
            x2 = x_ref[rows, :] + gate_ref[...] * (o_ref[rows, :] + down(rows, (nf - 1) % 2))
            var = jnp.mean(x2 * x2, axis=-1, keepdims=True)
            o_ref[rows, :] = x2 * lax.rsqrt(var + EPS) * gf_ref[...]


def _mlp(x1, shift, scale, gate, g2, gf, w1, w2, seq):
    m, d = x1.shape
    dff = w1.shape[1]
    tm = _tile(seq, 512, SUBLANES * ROW_CHUNKS)
    tf = _tile(dff, 1024, LANES)
    nf = dff // tf
    per_b = seq // tm
    row = lambda i, f: (i // per_b, 0, 0)
    const = lambda i, f: (0, 0)
    return pl.pallas_call(
        functools.partial(_mlp_kernel, nf=nf),
        grid=(m // tm, nf + 1),
        in_specs=[pl.BlockSpec((tm, d), lambda i, f: (i, 0)),
                  pl.BlockSpec((None, 1, d), row),
                  pl.BlockSpec((None, 1, d), row),
                  pl.BlockSpec((None, 1, d), row),
                  pl.BlockSpec((1, d), const),
                  pl.BlockSpec((1, d), const),
                  pl.BlockSpec((None, d, tf), lambda i, f: (jnp.minimum(f, nf - 1), 0, 0)),
                  pl.BlockSpec((tf, d), lambda i, f: (jnp.maximum(f - 1, 0), 0))],
        out_specs=pl.BlockSpec((tm, d), lambda i, f: (i, 0)),
        out_shape=jax.ShapeDtypeStruct((m, d), F32),
        scratch_shapes=[pltpu.VMEM((tm, d), BF16), pltpu.VMEM((2, tm, tf), BF16)],
        compiler_params=_params("parallel", "arbitrary"),
        name="mlp",
    )(x1, shift, scale, gate, g2, gf, w1.reshape(d, nf, tf).transpose(1, 0, 2), w2)


def kernel(x, c, w_ada, b_ada, norm1_g, w_in, conv_w, conv_b, lru_wa, lru_ba, lru_wx, lru_bx,
           lru_lambda, w_lru_out, w_attn_out, attn_sinks, rel_bias, w_out, norm2_g, w_ff1, w_ff2,
           final_g):
    bsz, seq, d = x.shape
    depth = w_in.shape[0]
    lru_w = w_lru_out.shape[1]
    attn_w = w_attn_out.shape[1]
    kv_w = (w_in.shape[2] - 2 * lru_w - attn_w - 2 * d) // 2
    n_heads = attn_sinks.shape[1]
    n_kv = kv_w // (attn_w // n_heads)
    q_off = 2 * lru_w
    k_off = q_off + attn_w
    v_off = k_off + kv_w
    ga_off = v_off + kv_w
    gb_off = ga_off + d

    xs = x.reshape(bsz * seq, d)
    for l in range(depth):
        band_bias = _band_bias(rel_bias, attn_sinks[l], n_kv)
        mod = _adaln(c, w_ada[l], b_ada[l])
        shift1, scale1, gate1, shift2, scale2, gate2 = [
            t.reshape(bsz, 1, d) for t in jnp.split(mod, 6, axis=-1)]

        rest = _in_proj(xs, shift1, scale1, norm1_g[l].reshape(1, d), w_in[l].astype(BF16), seq)
        wcat = jnp.concatenate([lru_wa[l], lru_wx[l]], axis=-1).astype(BF16)
        rec = _rglru(rest, wcat, conv_w[l], conv_b[l], lru_ba[l], lru_bx[l], lru_lambda[l],
                     bsz, seq, lru_w)
        x1 = _mix(rest, rec, xs, gate1, band_bias, w_lru_out[l].astype(BF16),
                  w_attn_out[l].astype(BF16), w_out[l].astype(BF16), seq, q_off, k_off, v_off,
                  ga_off, gb_off, kv_w, n_heads)
        if l != depth - 1:
            raise NotImplementedError("the fused final norm assumes a single layer")
        xs = _mlp(x1, shift2, scale2, gate2, norm2_g[l].reshape(1, d), final_g.reshape(1, d),
                  w_ff1[l].astype(BF16), w_ff2[l].astype(BF16), seq)
    return xs.reshape(bsz, seq, d)
```

```python
import functools
import math

import jax
import jax.numpy as jnp
from jax import lax
from jax.experimental import pallas as pl
from jax.experimental.pallas import tpu as pltpu

EPS = 1e-6
LRU_C = 8.0
LOG2E = 1.4426950408889634
ATTN_BLOCK = 128
NEG_INF = -1e30
MAX_DISTANCE = 128
LANES = 128
SUBLANES = 8
VMEM_LIMIT = 56 * 1024 * 1024
IN_PROJ_ROWS, IN_PROJ_COLS = 1024, 1792
RGLRU_STEPS, RGLRU_CHANNELS = 256, 512
MLP_ROWS, MLP_FF_CHUNK = 512, 1024
ADALN_COLS = 1024
ROW_CHUNKS = 2
IN_PROJ_ROW_CHUNKS = 4
QUERY_BLOCKS = 2

F32 = jnp.float32
BF16 = jnp.bfloat16


def _tile(n, target, align):
    best = None
    t = align
    while t <= min(n, target):
        if n % t == 0:
            best = t
        t += align
    if best is None:
        raise ValueError(f"no tile for {n} (target {target}, align {align})")
    return best


def _params(*sem):
    return pltpu.CompilerParams(dimension_semantics=sem, vmem_limit_bytes=VMEM_LIMIT)


def _sigmoid(v):
    return 1.0 / (1.0 + jnp.exp2(v * -LOG2E))


def _rms_mod(x, g, shift, scale):
    var = jnp.mean(x * x, axis=-1, keepdims=True)
    y = x * lax.rsqrt(var + EPS) * g
    return y * (1.0 + scale) + shift


def _adaln_kernel(c_ref, w_ref, b_ref, o_ref):
    c = c_ref[...]
    act = (c * _sigmoid(c)).astype(BF16)
    o_ref[...] = jnp.dot(act, w_ref[...].astype(BF16), preferred_element_type=F32) + b_ref[...]


def _adaln(c, w, b):
    bsz, d = c.shape
    n = w.shape[1]
    tn = _tile(n, ADALN_COLS, LANES)
    return pl.pallas_call(
        _adaln_kernel,
        grid=(n // tn,),
        in_specs=[pl.BlockSpec((bsz, d), lambda j: (0, 0)),
                  pl.BlockSpec((d, tn), lambda j: (0, j)),
                  pl.BlockSpec((1, tn), lambda j: (0, j))],
        out_specs=pl.BlockSpec((bsz, tn), lambda j: (0, j)),
        out_shape=jax.ShapeDtypeStruct((bsz, n), F32),
        compiler_params=_params("parallel"),
        name="adaln_mod",
    )(c, w, b.reshape(1, n))


def _gelu_tanh(v):
    k1 = -2.0 * math.sqrt(2.0 / math.pi) * LOG2E
    return v / (1.0 + jnp.exp2(v * (k1 + (k1 * 0.044715) * (v * v))))


def _rglru_kernel(x_ref, gate_ref, w_ref, cw_ref, cb_ref, ba_ref, bx_ref, lam_ref, o_ref,
                  lx_ref, xtail, hcar, hm_ref, *, conv_k):
    bsz, tt, cw = x_ref.shape
    slabs = cw // LANES
    seq_start = pl.program_id(1) == 0

    @pl.when(seq_start)
    def _():
        xtail[...] = jnp.zeros_like(xtail)
        hcar[...] = jnp.zeros_like(hcar)

    for b in range(bsz):
        xb = x_ref[b].astype(F32)
        for s in range(slabs):
            lx_ref[s, pl.ds(b, tt, stride=SUBLANES), :] = xb[:, s * LANES:(s + 1) * LANES]

    lam = lam_ref[...]
    softplus_neg_lam = jnp.maximum(-lam, 0.0) + jnp.log(1.0 + jnp.exp(-jnp.abs(lam)))
    log2_a_coef = (-LRU_C * LOG2E) * softplus_neg_lam

    def conv_and_gates(s):
        sl = slice(s * LANES, (s + 1) * LANES)
        x3 = lx_ref[s].reshape(tt, SUBLANES, LANES)
        xs = jnp.concatenate([xtail[s], x3], axis=0)
        xtail[s] = xs[tt:]
        xc = cb_ref[:, sl] + cw_ref[conv_k - 1:conv_k, sl] * x3
        for k in range(conv_k - 1):
            xc = xc + cw_ref[k:k + 1, sl] * xs[k:k + tt]
        z = jnp.dot(xc.reshape(tt * SUBLANES, LANES).astype(BF16), w_ref[s],
                    preferred_element_type=F32)
        return xc, z

    ahead = conv_and_gates(0)
    for s in range(slabs):
        sl = slice(s * LANES, (s + 1) * LANES)
        xc, z = ahead
        if s + 1 < slabs:
            ahead = conv_and_gates(s + 1)
        r = _sigmoid(z[:, :LANES] + ba_ref[:, sl]).reshape(tt, SUBLANES, LANES)
        i = _sigmoid(z[:, LANES:] + bx_ref[:, sl]).reshape(tt, SUBLANES, LANES)
        a = jnp.exp2(log2_a_coef[:, sl] * r)
        om = 1.0 - a * a
        mult = jnp.where(om > 0.0, om * lax.rsqrt(om), 0.0)
        mult = jnp.concatenate([jnp.where(seq_start, 1.0, mult[:1]), mult[1:]], axis=0)
        u = mult * (i * xc)
        h = hcar[s]
        for t in range(tt):
            h = a[t] * h + u[t]
            hm_ref[s, t * SUBLANES:(t + 1) * SUBLANES, :] = h
        hcar[s] = h
        for b in range(bsz):
            hb = hm_ref[s, pl.ds(b, tt, stride=SUBLANES), :]
            gate = gate_ref[b, :, sl].astype(F32)
            o_ref[b, :, sl] = (hb * _gelu_tanh(gate)).astype(o_ref.dtype)


def _rglru(proj, wcat, conv_w, conv_b, ba, bx, lam, bsz, seq, lru_w):
    assert bsz == SUBLANES
    n = proj.shape[1]
    conv_k = conv_w.shape[0]
    tt = _tile(seq, RGLRU_STEPS, SUBLANES)
    cw = _tile(lru_w, RGLRU_CHANNELS, LANES)
    slabs = cw // LANES
    vec = pl.BlockSpec((1, cw), lambda c, t: (0, c))
    rec = pl.pallas_call(
        functools.partial(_rglru_kernel, conv_k=conv_k),
        grid=(lru_w // cw, seq // tt),
        in_specs=[pl.BlockSpec((bsz, tt, cw), lambda c, t: (0, t, c)),
                  pl.BlockSpec((bsz, tt, cw), lambda c, t: (0, t, lru_w // cw + c)),
                  pl.BlockSpec((slabs, LANES, 2 * LANES), lambda c, t: (c, 0, 0)),
                  pl.BlockSpec((conv_k, cw), lambda c, t: (0, c)),
                  vec, vec, vec, vec],
        out_specs=pl.BlockSpec((bsz, tt, cw), lambda c, t: (0, t, c)),
        out_shape=jax.ShapeDtypeStruct((bsz, seq, lru_w), BF16),
        scratch_shapes=[pltpu.VMEM((slabs, bsz * tt, LANES), F32),
                        pltpu.VMEM((slabs, conv_k - 1, SUBLANES, LANES), F32),
                        pltpu.VMEM((slabs, SUBLANES, LANES), F32),
                        pltpu.VMEM((slabs, bsz * tt, LANES), F32)],
        compiler_params=_params("parallel", "arbitrary"),
        name="rglru",
    )(proj.reshape(bsz, seq, n), proj.reshape(bsz, seq, n), wcat, conv_w, conv_b.reshape(1, -1),
      ba.reshape(1, -1), bx.reshape(1, -1), lam.reshape(1, -1))
    return rec.reshape(bsz * seq, lru_w)


def _in_proj_kernel(x_ref, sh_ref, sc_ref, g_ref, w_ref, o_ref, h_ref):
    j = pl.program_id(1)
    rc = x_ref.shape[0] // IN_PROJ_ROW_CHUNKS

    @pl.when(j == 0)
    def _():
        for r in range(IN_PROJ_ROW_CHUNKS):
            rows = pl.ds(r * rc, rc)
            h = _rms_mod(x_ref[rows, :], g_ref[...], sh_ref[...], sc_ref[...]).astype(BF16)
            h_ref[rows, :] = h
            o_ref[rows, :] = jnp.dot(h, w_ref[...], preferred_element_type=F32).astype(o_ref.dtype)

    @pl.when(j > 0)
    def _():
        o_ref[...] = jnp.dot(h_ref[...], w_ref[...],
                             preferred_element_type=F32).astype(o_ref.dtype)


def _in_proj(x2d, shift, scale, g, w, seq):
    m, d = x2d.shape
    n = w.shape[1]
    tm = _tile(seq, IN_PROJ_ROWS, SUBLANES * IN_PROJ_ROW_CHUNKS)
    tn = _tile(n, IN_PROJ_COLS, LANES)
    per_b = seq // tm
    row = lambda i, j: (i // per_b, 0, 0)
    return pl.pallas_call(
        _in_proj_kernel,
        grid=(m // tm, n // tn),
        in_specs=[pl.BlockSpec((tm, d), lambda i, j: (i, 0)),
                  pl.BlockSpec((None, 1, d), row),
                  pl.BlockSpec((None, 1, d), row),
                  pl.BlockSpec((1, d), lambda i, j: (0, 0)),
                  pl.BlockSpec((d, tn), lambda i, j: (0, j))],
        out_specs=pl.BlockSpec((tm, tn), lambda i, j: (i, j)),
        out_shape=jax.ShapeDtypeStruct((m, n), BF16),
        scratch_shapes=[pltpu.VMEM((tm, d), BF16)],
        compiler_params=_params("parallel", "arbitrary"),
        name="in_proj",
    )(x2d, shift, scale, g, w)


def _t5_bucket(rel, n_buckets):
    max_exact = n_buckets // 2
    relf = jnp.maximum(rel, 1).astype(F32)
    large = max_exact + (jnp.log(relf / max_exact) / math.log(MAX_DISTANCE / max_exact)
                         * (n_buckets - max_exact)).astype(jnp.int32)
    large = jnp.minimum(large, n_buckets - 1)
    return jnp.where(rel < max_exact, rel, large)


def _band_bias_kernel(rb_ref, sink_ref, bucket_ref, o_ref, *, pairs, n_buckets):
    kv = pl.program_id(0)
    blk = ATTN_BLOCK
    bucket = bucket_ref[...]
    qi = lax.broadcasted_iota(jnp.int32, (blk, 2 * blk), 0)
    ki = lax.broadcasted_iota(jnp.int32, (blk, 2 * blk), 1)
    rel = qi + blk - ki
    valid = (rel >= 0) & (rel < blk)
    for p in range(pairs):
        for par in range(2):
            h = (kv * pairs + p) * 2 + par
            bias = jnp.zeros((blk, 2 * blk), F32)
            for b in range(n_buckets):
                bias = jnp.where(bucket == b, rb_ref[b, h], bias)
            sink = sink_ref[h]
            rows, cols = slice(p * blk, (p + 1) * blk), slice(par * 2 * blk, (par + 1) * 2 * blk)
            o_ref[rows, cols] = jnp.where(ki == 0, sink, jnp.where(valid, bias, NEG_INF))


def _band_bias(rel_bias, sinks, n_kv):
    n_buckets, n_heads = rel_bias.shape
    pairs = n_heads // n_kv // 2
    blk = ATTN_BLOCK
    qi = jnp.arange(blk)[:, None]
    ki = jnp.arange(2 * blk)[None, :]
    bucket = _t5_bucket(jnp.maximum(qi + blk - ki, 0), n_buckets)
    smem = pl.BlockSpec(memory_space=pltpu.SMEM)
    return pl.pallas_call(
        functools.partial(_band_bias_kernel, pairs=pairs, n_buckets=n_buckets),
        grid=(n_kv,),
        in_specs=[smem, smem, pl.BlockSpec((blk, 2 * blk), lambda kv: (0, 0))],
        out_specs=pl.BlockSpec((None, pairs * blk, 4 * blk), lambda kv: (kv, 0, 0)),
        out_shape=jax.ShapeDtypeStruct((n_kv, pairs * blk, 4 * blk), F32),
        compiler_params=_params("parallel"),
        name="band_bias",
    )(rel_bias, sinks, bucket)


def _swap_lane_halves(v):
    u = pltpu.bitcast(v, jnp.uint32)
    return pltpu.bitcast(pltpu.roll(u, LANES // 2, axis=1), BF16)


def _mix_kernel(*refs, n_kv, pairs, scale, per_b):
    q_ref, kp_ref, kc_ref, vp_ref, vc_ref, bm_ref, rec_ref = refs[:7]
    ga_refs, gb_refs = refs[7:7 + n_kv], refs[7 + n_kv:7 + 2 * n_kv]
    x_ref, gate_ref, wl_ref, wa_ref, wo_ref, o_ref, att_ref, ya_ref, m_ref = refs[7 + 2 * n_kv:]
    blk = ATTN_BLOCK
    d = o_ref.shape[1]
    cc = d // n_kv
    fold = math.log2(scale).is_integer()
    lane = lax.broadcasted_iota(jnp.int32, (blk, LANES), 1)
    key = lax.broadcasted_iota(jnp.int32, (blk, LANES), 0)
    low = jnp.where(lane < LANES // 2, 1.0, 0.0).astype(BF16)
    high = jnp.where(lane < LANES // 2, 0.0, 1.0).astype(BF16)
    not_key0 = jnp.where(key == 0, 0.0, 1.0).astype(BF16)
    ones_blk = jnp.concatenate([low, low, high, high], axis=0)
    key_col = lax.broadcasted_iota(jnp.int32, (1, 4 * blk), 1) % (2 * blk)
    seq_start = pl.program_id(0) % per_b == 0
    start_mask = jnp.where(seq_start & (key_col >= 1) & (key_col < blk), NEG_INF, 0.0)

    def halves(t, kv):
        s = _swap_lane_halves(t)
        lo, hi = (t, s) if kv % 2 == 0 else (s, t)
        return lo * low, hi * high

    def window(prev, cur):
        return jnp.concatenate([prev[0] * not_key0, cur[0], prev[1] * not_key0, cur[1]], axis=0)

    for kv in range(n_kv):
        sl = slice((kv // 2) * LANES, (kv // 2 + 1) * LANES)
        kb = [halves(kp_ref[:, sl], kv)] + [halves(kc_ref[t * blk:(t + 1) * blk, sl], kv)
                                            for t in range(QUERY_BLOCKS)]
        vb = [halves(vp_ref[:, sl], kv)] + [halves(vc_ref[t * blk:(t + 1) * blk, sl], kv)
                                            for t in range(QUERY_BLOCKS)]
        for t in range(QUERY_BLOCKS):
            rows = slice(t * blk, (t + 1) * blk)
            kblk = window(kb[t], kb[t + 1])
            vaug = jnp.concatenate([window(vb[t], vb[t + 1]), ones_blk], axis=1)
            q = jnp.concatenate(
                [q_ref[rows, (kv * pairs + p) * LANES:(kv * pairs + p + 1) * LANES]
                 for p in range(pairs)], axis=0)
            if fold:
                q = q * scale
            s = lax.dot_general(q, kblk, (((1,), (1,)), ((), ())), preferred_element_type=F32)
            if not fold:
                s = s * scale
            s = s + bm_ref[kv]
            if t == 0:
                s = s + start_mask
            es = []
            for par in range(2):
                sp = s[:, par * 2 * blk:(par + 1) * 2 * blk]
                es.append(jnp.exp(sp - jnp.max(sp, axis=-1, keepdims=True)).astype(BF16))
            o = jnp.dot(jnp.concatenate(es, axis=1), vaug, preferred_element_type=F32)
            res = (o[:, :LANES] / o[:, LANES:]).astype(BF16)
            for p in range(pairs):
                att_ref[rows, (kv * pairs + p) * LANES:(kv * pairs + p + 1) * LANES] = (
                    res[p * blk:(p + 1) * blk])
        cols = slice(kv * cc, (kv + 1) * cc)
        ya = jnp.dot(rec_ref[...], wl_ref[:, cols], preferred_element_type=F32)
        ya_ref[:, cols] = _sigmoid(ga_refs[kv][...].astype(F32)) * ya

    for c in range(n_kv):
        cols = slice(c * cc, (c + 1) * cc)
        yb = jnp.dot(att_ref[...], wa_ref[:, cols], preferred_element_type=F32)
        gb = _sigmoid(gb_refs[c][...].astype(F32))
        m_ref[:, cols] = (ya_ref[:, cols] + gb * yb).astype(BF16)
    for c in range(n_kv):
        cols = slice(c * cc, (c + 1) * cc)
        y = jnp.dot(m_ref[...], wo_ref[:, cols], preferred_element_type=F32)
        o_ref[:, cols] = x_ref[:, cols] + gate_ref[:, cols] * y


def _mix(rest, rec, x2d, gate, band_bias, w_lru, w_att, w_out, seq, q_off, k_off, v_off, ga_off,
         gb_off, kv_w, n_heads):
    m, d = x2d.shape
    attn_w = w_att.shape[0]
    hd = attn_w // n_heads
    n_kv = kv_w // hd
    pairs = n_heads // n_kv // 2
    blk = ATTN_BLOCK
    tm = QUERY_BLOCKS * blk
    per_b = seq // tm
    cc = d // n_kv
    assert 2 * hd == LANES and n_heads == 2 * pairs * n_kv and kv_w % LANES == 0
    assert q_off % attn_w == 0 and k_off % kv_w == 0 and v_off % kv_w == 0
    assert seq % tm == 0 and cc % LANES == 0 and ga_off % cc == 0 and gb_off % cc == 0
    kb, vb = k_off // kv_w, v_off // kv_w
    prev = lambda col: (lambda i: (jnp.maximum(QUERY_BLOCKS * i - 1, 0), col))
    col_chunk = lambda off: [pl.BlockSpec((tm, cc), functools.partial(lambda c, i: (i, c),
                                                                      off // cc + c))
                             for c in range(n_kv)]
    resident = lambda a: pl.BlockSpec(a.shape, lambda i: (0,) * a.ndim,
                                      pipeline_mode=pl.Buffered(1))
    kernel = functools.partial(_mix_kernel, n_kv=n_kv, pairs=pairs, scale=hd ** -0.5, per_b=per_b)
    return pl.pallas_call(
        kernel,
        grid=(m // tm,),
        in_specs=[pl.BlockSpec((tm, attn_w), lambda i: (i, q_off // attn_w)),
                  pl.BlockSpec((blk, kv_w), prev(kb)),
                  pl.BlockSpec((tm, kv_w), lambda i: (i, kb)),
                  pl.BlockSpec((blk, kv_w), prev(vb)),
                  pl.BlockSpec((tm, kv_w), lambda i: (i, vb)),
                  resident(band_bias),
                  pl.BlockSpec((tm, rec.shape[1]), lambda i: (i, 0))]
                 + col_chunk(ga_off) + col_chunk(gb_off)
                 + [pl.BlockSpec((tm, d), lambda i: (i, 0)),
                    pl.BlockSpec((None, 1, d), lambda i: (i // per_b, 0, 0)),
                    resident(w_lru), resident(w_att), resident(w_out)],
        out_specs=pl.BlockSpec((tm, d), lambda i: (i, 0)),
        out_shape=jax.ShapeDtypeStruct((m, d), F32),
        scratch_shapes=[pltpu.VMEM((tm, attn_w), BF16), pltpu.VMEM((tm, d), F32),
                        pltpu.VMEM((tm, d), BF16)],
        compiler_params=_params("parallel"),
        name="mix",
    )(rest, rest, rest, rest, rest, band_bias, rec, *([rest] * (2 * n_kv)), x2d, gate,
      w_lru, w_att, w_out)


def _mlp_kernel(x_ref, sh_ref, sc_ref, gate_ref, g2_ref, gf_ref, w1_ref, w2_ref, o_ref,
                h_ref, ff_ref, *, nf):
    f = pl.program_id(1)
    rc = x_ref.shape[0] // ROW_CHUNKS

    def up(rows, slot):
        ff = jnp.dot(h_ref[rows, :], w1_ref[...], preferred_element_type=F32)
        ff_ref[slot, rows, :] = jnp.square(jnp.maximum(ff, 0.0)).astype(BF16)

    def down(rows, slot):
        return jnp.dot(ff_ref[slot, rows, :], w2_ref[...], preferred_element_type=F32)

    @pl.when(f == 0)
    def _():
        o_ref[...] = jnp.zeros_like(o_ref)
        for r in range(ROW_CHUNKS):
            rows = pl.ds(r * rc, rc)
            h_ref[rows, :] = _rms_mod(x_ref[rows, :], g2_ref[...], sh_ref[...],
                                      sc_ref[...]).astype(BF16)
            up(rows, 0)

    @pl.when((f > 0) & (f < nf))
    def _():
        o_ref[...] += down(slice(None), (f - 1) % 2)
        up(slice(None), f % 2)

    @pl.when(f == nf)
    def _():
        for r in range(ROW_CHUNKS):
            rows = pl.ds(r * rc, rc)
            x2 = x_ref[rows, :] + gate_ref[...] * (o_ref[rows, :] + down(rows, (nf - 1) % 2))
            var = jnp.mean(x2 * x2, axis=-1, keepdims=True)
            o_ref[rows, :] = x2 * lax.rsqrt(var + EPS) * gf_ref[...]


def _mlp(x1, shift, scale, gate, g2, gf, w1, w2, seq):
    m, d = x1.shape
    dff = w1.shape[1]
    tm = _tile(seq, MLP_ROWS, SUBLANES * ROW_CHUNKS)
    tf = _tile(dff, MLP_FF_CHUNK, LANES)
    nf = dff // tf
    per_b = seq // tm
    row = lambda i, f: (i // per_b, 0, 0)
    const = lambda i, f: (0, 0)
    return pl.pallas_call(
        functools.partial(_mlp_kernel, nf=nf),
        grid=(m // tm, nf + 1),
        in_specs=[pl.BlockSpec((tm, d), lambda i, f: (i, 0)),
                  pl.BlockSpec((None, 1, d), row),
                  pl.BlockSpec((None, 1, d), row),
                  pl.BlockSpec((None, 1, d), row),
                  pl.BlockSpec((1, d), const),
                  pl.BlockSpec((1, d), const),
                  pl.BlockSpec((d, tf), lambda i, f: (0, jnp.minimum(f, nf - 1))),
                  pl.BlockSpec((tf, d), lambda i, f: (jnp.maximum(f - 1, 0), 0))],
        out_specs=pl.BlockSpec((tm, d), lambda i, f: (i, 0)),
        out_shape=jax.ShapeDtypeStruct((m, d), F32),
        scratch_shapes=[pltpu.VMEM((tm, d), BF16), pltpu.VMEM((2, tm, tf), BF16)],
        compiler_params=_params("parallel", "arbitrary"),
        name="mlp",
    )(x1, shift, scale, gate, g2, gf, w1, w2)


def kernel(x, c, w_ada, b_ada, norm1_g, w_in, conv_w, conv_b, lru_wa, lru_ba, lru_wx, lru_bx,
           lru_lambda, w_lru_out, w_attn_out, attn_sinks, rel_bias, w_out, norm2_g, w_ff1, w_ff2,
           final_g):
    bsz, seq, d = x.shape
    depth = w_in.shape[0]
    lru_w = w_lru_out.shape[1]
    attn_w = w_attn_out.shape[1]
    kv_w = (w_in.shape[2] - 2 * lru_w - attn_w - 2 * d) // 2
    n_heads = attn_sinks.shape[1]
    n_kv = kv_w // (attn_w // n_heads)
    q_off = 2 * lru_w
    k_off = q_off + attn_w
    v_off = k_off + kv_w
    ga_off = v_off + kv_w
    gb_off = ga_off + d

    xs = x.reshape(bsz * seq, d)
    for l in range(depth):
        band_bias = _band_bias(rel_bias, attn_sinks[l], n_kv)
        mod = _adaln(c, w_ada[l], b_ada[l])
        shift1, scale1, gate1, shift2, scale2, gate2 = [
            t.reshape(bsz, 1, d) for t in jnp.split(mod, 6, axis=-1)]

        rest = _in_proj(xs, shift1, scale1, norm1_g[l].reshape(1, d), w_in[l].astype(BF16), seq)
        wcat = jnp.concatenate([lru_wa[l], lru_wx[l]], axis=-1).astype(BF16)
        rec = _rglru(rest, wcat, conv_w[l], conv_b[l], lru_ba[l], lru_bx[l], lru_lambda[l],
                     bsz, seq, lru_w)
        x1 = _mix(rest, rec, xs, gate1, band_bias, w_lru_out[l].astype(BF16),
                  w_attn_out[l].astype(BF16), w_out[l].astype(BF16), seq, q_off, k_off, v_off,
                  ga_off, gb_off, kv_w, n_heads)
        if l != depth - 1:
            raise NotImplementedError("the fused final norm assumes a single layer")
        xs = _mlp(x1, shift2, scale2, gate2, norm2_g[l].reshape(1, d), final_g.reshape(1, d),
                  w_ff1[l].astype(BF16), w_ff2[l].astype(BF16), seq)
    return xs.reshape(bsz, seq, d)
```

```python
import functools
import math

import jax
import jax.numpy as jnp
from jax import lax
from jax.experimental import pallas as pl
from jax.experimental.pallas import tpu as pltpu

EPS = 1e-6
LRU_C = 8.0
LOG2E = 1.4426950408889634
ATTN_BLOCK = 128
NEG_INF = -1e30
MAX_DISTANCE = 128
LANES = 128
SUBLANES = 8
VMEM_LIMIT = 56 * 1024 * 1024
IN_PROJ_ROWS, IN_PROJ_COLS = 1024, 1792
RGLRU_STEPS, RGLRU_CHANNELS = 256, 512
MLP_ROWS, MLP_FF_CHUNK = 512, 1024
MLP_UP_COLS, MLP_DOWN_K = 1024, 2048
COL_CHUNKS = 4
ADALN_COLS = 1024
ROW_CHUNKS = 2
IN_PROJ_ROW_CHUNKS = 4
QUERY_BLOCKS = 2

F32 = jnp.float32
BF16 = jnp.bfloat16


def _tile(n, target, align):
    best = None
    t = align
    while t <= min(n, target):
        if n % t == 0:
            best = t
        t += align
    if best is None:
        raise ValueError(f"no tile for {n} (target {target}, align {align})")
    return best


def _params(*sem):
    return pltpu.CompilerParams(dimension_semantics=sem, vmem_limit_bytes=VMEM_LIMIT)


def _sigmoid(v):
    return 1.0 / (1.0 + jnp.exp2(v * -LOG2E))


def _rms_mod(x, g, shift, scale):
    var = jnp.mean(x * x, axis=-1, keepdims=True)
    y = x * lax.rsqrt(var + EPS) * g
    return y * (1.0 + scale) + shift


def _adaln_kernel(c_ref, w_ref, b_ref, o_ref):
    c = c_ref[...]
    act = (c * _sigmoid(c)).astype(BF16)
    o_ref[...] = jnp.dot(act, w_ref[...].astype(BF16), preferred_element_type=F32) + b_ref[...]


def _adaln(c, w, b):
    bsz, d = c.shape
    n = w.shape[1]
    tn = _tile(n, ADALN_COLS, LANES)
    return pl.pallas_call(
        _adaln_kernel,
        grid=(n // tn,),
        in_specs=[pl.BlockSpec((bsz, d), lambda j: (0, 0)),
                  pl.BlockSpec((d, tn), lambda j: (0, j)),
                  pl.BlockSpec((1, tn), lambda j: (0, j))],
        out_specs=pl.BlockSpec((bsz, tn), lambda j: (0, j)),
        out_shape=jax.ShapeDtypeStruct((bsz, n), F32),
        compiler_params=_params("parallel"),
        name="adaln_mod",
    )(c, w, b.reshape(1, n))


def _gelu_tanh(v):
    k1 = -2.0 * math.sqrt(2.0 / math.pi) * LOG2E
    return v / (1.0 + jnp.exp2(v * (k1 + (k1 * 0.044715) * (v * v))))


def _rglru_kernel(x_ref, gate_ref, w_ref, cw_ref, cb_ref, ba_ref, bx_ref, lam_ref, o_ref,
                  lx_ref, xtail, hcar, hm_ref, *, conv_k):
    bsz, tt, cw = x_ref.shape
    slabs = cw // LANES
    seq_start = pl.program_id(1) == 0

    @pl.when(seq_start)
    def _():
        xtail[...] = jnp.zeros_like(xtail)
        hcar[...] = jnp.zeros_like(hcar)

    for b in range(bsz):
        xb = x_ref[b].astype(F32)
        for s in range(slabs):
            lx_ref[s, pl.ds(b, tt, stride=SUBLANES), :] = xb[:, s * LANES:(s + 1) * LANES]

    lam = lam_ref[...]
    softplus_neg_lam = jnp.maximum(-lam, 0.0) + jnp.log(1.0 + jnp.exp(-jnp.abs(lam)))
    log2_a_coef = (-LRU_C * LOG2E) * softplus_neg_lam

    def conv_and_gates(s):
        sl = slice(s * LANES, (s + 1) * LANES)
        x3 = lx_ref[s].reshape(tt, SUBLANES, LANES)
        xs = jnp.concatenate([xtail[s], x3], axis=0)
        xtail[s] = xs[tt:]
        xc = cb_ref[:, sl] + cw_ref[conv_k - 1:conv_k, sl] * x3
        for k in range(conv_k - 1):
            xc = xc + cw_ref[k:k + 1, sl] * xs[k:k + tt]
        z = jnp.dot(xc.reshape(tt * SUBLANES, LANES).astype(BF16), w_ref[s],
                    preferred_element_type=F32)
        return xc, z

    ahead = conv_and_gates(0)
    for s in range(slabs):
        sl = slice(s * LANES, (s + 1) * LANES)
        xc, z = ahead
        if s + 1 < slabs:
            ahead = conv_and_gates(s + 1)
        r = _sigmoid(z[:, :LANES] + ba_ref[:, sl]).reshape(tt, SUBLANES, LANES)
        i = _sigmoid(z[:, LANES:] + bx_ref[:, sl]).reshape(tt, SUBLANES, LANES)
        a = jnp.exp2(log2_a_coef[:, sl] * r)
        om = 1.0 - a * a
        mult = jnp.where(om > 0.0, om * lax.rsqrt(om), 0.0)
        mult = jnp.concatenate([jnp.where(seq_start, 1.0, mult[:1]), mult[1:]], axis=0)
        u = mult * (i * xc)
        h = hcar[s]
        for t in range(tt):
            h = a[t] * h + u[t]
            hm_ref[s, t * SUBLANES:(t + 1) * SUBLANES, :] = h
        hcar[s] = h
        for b in range(bsz):
            hb = hm_ref[s, pl.ds(b, tt, stride=SUBLANES), :]
            gate = gate_ref[b, :, sl].astype(F32)
            o_ref[b, :, sl] = (hb * _gelu_tanh(gate)).astype(o_ref.dtype)


def _rglru(proj, wcat, conv_w, conv_b, ba, bx, lam, bsz, seq, lru_w):
    assert bsz == SUBLANES
    n = proj.shape[1]
    conv_k = conv_w.shape[0]
    tt = _tile(seq, RGLRU_STEPS, SUBLANES)
    cw = _tile(lru_w, RGLRU_CHANNELS, LANES)
    slabs = cw // LANES
    vec = pl.BlockSpec((1, cw), lambda c, t: (0, c))
    rec = pl.pallas_call(
        functools.partial(_rglru_kernel, conv_k=conv_k),
        grid=(lru_w // cw, seq // tt),
        in_specs=[pl.BlockSpec((bsz, tt, cw), lambda c, t: (0, t, c)),
                  pl.BlockSpec((bsz, tt, cw), lambda c, t: (0, t, lru_w // cw + c)),
                  pl.BlockSpec((slabs, LANES, 2 * LANES), lambda c, t: (c, 0, 0)),
                  pl.BlockSpec((conv_k, cw), lambda c, t: (0, c)),
                  vec, vec, vec, vec],
        out_specs=pl.BlockSpec((bsz, tt, cw), lambda c, t: (0, t, c)),
        out_shape=jax.ShapeDtypeStruct((bsz, seq, lru_w), BF16),
        scratch_shapes=[pltpu.VMEM((slabs, bsz * tt, LANES), F32),
                        pltpu.VMEM((slabs, conv_k - 1, SUBLANES, LANES), F32),
                        pltpu.VMEM((slabs, SUBLANES, LANES), F32),
                        pltpu.VMEM((slabs, bsz * tt, LANES), F32)],
        compiler_params=_params("parallel", "arbitrary"),
        name="rglru",
    )(proj.reshape(bsz, seq, n), proj.reshape(bsz, seq, n), wcat, conv_w, conv_b.reshape(1, -1),
      ba.reshape(1, -1), bx.reshape(1, -1), lam.reshape(1, -1))
    return rec.reshape(bsz * seq, lru_w)


def _in_proj_kernel(x_ref, sh_ref, sc_ref, g_ref, w_ref, o_ref, h_ref, *, sq_relu):
    j = pl.program_id(1)
    rc = x_ref.shape[0] // IN_PROJ_ROW_CHUNKS

    def project(h):
        y = jnp.dot(h, w_ref[...], preferred_element_type=F32)
        if sq_relu:
            y = jnp.square(jnp.maximum(y, 0.0))
        return y.astype(o_ref.dtype)

    @pl.when(j == 0)
    def _():
        for r in range(IN_PROJ_ROW_CHUNKS):
            rows = pl.ds(r * rc, rc)
            h = _rms_mod(x_ref[rows, :], g_ref[...], sh_ref[...], sc_ref[...]).astype(BF16)
            h_ref[rows, :] = h
            o_ref[rows, :] = project(h)

    @pl.when(j > 0)
    def _():
        o_ref[...] = project(h_ref[...])


def _in_proj(x2d, shift, scale, g, w, seq, tn_target, name, sq_relu=False):
    m, d = x2d.shape
    n = w.shape[1]
    tm = _tile(seq, IN_PROJ_ROWS, SUBLANES * IN_PROJ_ROW_CHUNKS)
    tn = _tile(n, tn_target, LANES)
    per_b = seq // tm
    row = lambda i, j: (i // per_b, 0, 0)
    return pl.pallas_call(
        functools.partial(_in_proj_kernel, sq_relu=sq_relu),
        grid=(m // tm, n // tn),
        in_specs=[pl.BlockSpec((tm, d), lambda i, j: (i, 0)),
                  pl.BlockSpec((None, 1, d), row),
                  pl.BlockSpec((None, 1, d), row),
                  pl.BlockSpec((1, d), lambda i, j: (0, 0)),
                  pl.BlockSpec((d, tn), lambda i, j: (0, j))],
        out_specs=pl.BlockSpec((tm, tn), lambda i, j: (i, j)),
        out_shape=jax.ShapeDtypeStruct((m, n), BF16),
        scratch_shapes=[pltpu.VMEM((tm, d), BF16)],
        compiler_params=_params("parallel", "arbitrary"),
        name=name,
    )(x2d, shift, scale, g, w)


def _t5_bucket(rel, n_buckets):
    max_exact = n_buckets // 2
    relf = jnp.maximum(rel, 1).astype(F32)
    large = max_exact + (jnp.log(relf / max_exact) / math.log(MAX_DISTANCE / max_exact)
                         * (n_buckets - max_exact)).astype(jnp.int32)
    large = jnp.minimum(large, n_buckets - 1)
    return jnp.where(rel < max_exact, rel, large)


def _band_bias_kernel(rb_ref, sink_ref, bucket_ref, o_ref, *, pairs, n_buckets):
    kv = pl.program_id(0)
    blk = ATTN_BLOCK
    bucket = bucket_ref[...]
    qi = lax.broadcasted_iota(jnp.int32, (blk, 2 * blk), 0)
    ki = lax.broadcasted_iota(jnp.int32, (blk, 2 * blk), 1)
    rel = qi + blk - ki
    valid = (rel >= 0) & (rel < blk)
    for p in range(pairs):
        for par in range(2):
            h = (kv * pairs + p) * 2 + par
            bias = jnp.zeros((blk, 2 * blk), F32)
            for b in range(n_buckets):
                bias = jnp.where(bucket == b, rb_ref[b, h], bias)
            sink = sink_ref[h]
            rows, cols = slice(p * blk, (p + 1) * blk), slice(par * 2 * blk, (par + 1) * 2 * blk)
            o_ref[rows, cols] = jnp.where(ki == 0, sink, jnp.where(valid, bias, NEG_INF))


def _band_bias(rel_bias, sinks, n_kv):
    n_buckets, n_heads = rel_bias.shape
    pairs = n_heads // n_kv // 2
    blk = ATTN_BLOCK
    qi = jnp.arange(blk)[:, None]
    ki = jnp.arange(2 * blk)[None, :]
    bucket = _t5_bucket(jnp.maximum(qi + blk - ki, 0), n_buckets)
    smem = pl.BlockSpec(memory_space=pltpu.SMEM)
    return pl.pallas_call(
        functools.partial(_band_bias_kernel, pairs=pairs, n_buckets=n_buckets),
        grid=(n_kv,),
        in_specs=[smem, smem, pl.BlockSpec((blk, 2 * blk), lambda kv: (0, 0))],
        out_specs=pl.BlockSpec((None, pairs * blk, 4 * blk), lambda kv: (kv, 0, 0)),
        out_shape=jax.ShapeDtypeStruct((n_kv, pairs * blk, 4 * blk), F32),
        compiler_params=_params("parallel"),
        name="band_bias",
    )(rel_bias, sinks, bucket)


def _swap_lane_halves(v):
    u = pltpu.bitcast(v, jnp.uint32)
    return pltpu.bitcast(pltpu.roll(u, LANES // 2, axis=1), BF16)


def _mix_kernel(*refs, n_kv, pairs, scale, per_b):
    q_ref, kp_ref, kc_ref, vp_ref, vc_ref, bm_ref, rec_ref = refs[:7]
    ga_refs, gb_refs = refs[7:7 + n_kv], refs[7 + n_kv:7 + 2 * n_kv]
    x_ref, gate_ref, wl_ref, wa_ref, wo_ref, o_ref, att_ref, ya_ref, m_ref = refs[7 + 2 * n_kv:]
    blk = ATTN_BLOCK
    d = o_ref.shape[1]
    cc = d // n_kv
    fold = math.log2(scale).is_integer()
    lane = lax.broadcasted_iota(jnp.int32, (blk, LANES), 1)
    key = lax.broadcasted_iota(jnp.int32, (blk, LANES), 0)
    low = jnp.where(lane < LANES // 2, 1.0, 0.0).astype(BF16)
    high = jnp.where(lane < LANES // 2, 0.0, 1.0).astype(BF16)
    not_key0 = jnp.where(key == 0, 0.0, 1.0).astype(BF16)
    ones_blk = jnp.concatenate([low, low, high, high], axis=0)
    key_col = lax.broadcasted_iota(jnp.int32, (1, 4 * blk), 1) % (2 * blk)
    seq_start = pl.program_id(0) % per_b == 0
    start_mask = jnp.where(seq_start & (key_col >= 1) & (key_col < blk), NEG_INF, 0.0)

    def halves(t, kv):
        s = _swap_lane_halves(t)
        lo, hi = (t, s) if kv % 2 == 0 else (s, t)
        return lo * low, hi * high

    def window(prev, cur):
        return jnp.concatenate([prev[0] * not_key0, cur[0], prev[1] * not_key0, cur[1]], axis=0)

    for kv in range(n_kv):
        sl = slice((kv // 2) * LANES, (kv // 2 + 1) * LANES)
        kb = [halves(kp_ref[:, sl], kv)] + [halves(kc_ref[t * blk:(t + 1) * blk, sl], kv)
                                            for t in range(QUERY_BLOCKS)]
        vb = [halves(vp_ref[:, sl], kv)] + [halves(vc_ref[t * blk:(t + 1) * blk, sl], kv)
                                            for t in range(QUERY_BLOCKS)]
        for t in range(QUERY_BLOCKS):
            rows = slice(t * blk, (t + 1) * blk)
            kblk = window(kb[t], kb[t + 1])
            vaug = jnp.concatenate([window(vb[t], vb[t + 1]), ones_blk], axis=1)
            q = jnp.concatenate(
                [q_ref[rows, (kv * pairs + p) * LANES:(kv * pairs + p + 1) * LANES]
                 for p in range(pairs)], axis=0)
            if fold:
                q = q * scale
            s = lax.dot_general(q, kblk, (((1,), (1,)), ((), ())), preferred_element_type=F32)
            if not fold:
                s = s * scale
            s = s + bm_ref[kv]
            if t == 0:
                s = s + start_mask
            es = []
            for par in range(2):
                sp = s[:, par * 2 * blk:(par + 1) * 2 * blk]
                es.append(jnp.exp(sp - jnp.max(sp, axis=-1, keepdims=True)).astype(BF16))
            o = jnp.dot(jnp.concatenate(es, axis=1), vaug, preferred_element_type=F32)
            res = (o[:, :LANES] / o[:, LANES:]).astype(BF16)
            for p in range(pairs):
                att_ref[rows, (kv * pairs + p) * LANES:(kv * pairs + p + 1) * LANES] = (
                    res[p * blk:(p + 1) * blk])
        cols = slice(kv * cc, (kv + 1) * cc)
        ya = jnp.dot(rec_ref[...], wl_ref[:, cols], preferred_element_type=F32)
        ya_ref[:, cols] = _sigmoid(ga_refs[kv][...].astype(F32)) * ya

    for c in range(n_kv):
        cols = slice(c * cc, (c + 1) * cc)
        yb = jnp.dot(att_ref[...], wa_ref[:, cols], preferred_element_type=F32)
        gb = _sigmoid(gb_refs[c][...].astype(F32))
        m_ref[:, cols] = (ya_ref[:, cols] + gb * yb).astype(BF16)
    for c in range(n_kv):
        cols = slice(c * cc, (c + 1) * cc)
        y = jnp.dot(m_ref[...], wo_ref[:, cols], preferred_element_type=F32)
        o_ref[:, cols] = x_ref[:, cols] + gate_ref[:, cols] * y


def _mix(rest, rec, x2d, gate, band_bias, w_lru, w_att, w_out, seq, q_off, k_off, v_off, ga_off,
         gb_off, kv_w, n_heads):
    m, d = x2d.shape
    attn_w = w_att.shape[0]
    hd = attn_w // n_heads
    n_kv = kv_w // hd
    pairs = n_heads // n_kv // 2
    blk = ATTN_BLOCK
    tm = QUERY_BLOCKS * blk
    per_b = seq // tm
    cc = d // n_kv
    assert 2 * hd == LANES and n_heads == 2 * pairs * n_kv and kv_w % LANES == 0
    assert q_off % attn_w == 0 and k_off % kv_w == 0 and v_off % kv_w == 0
    assert seq % tm == 0 and cc % LANES == 0 and ga_off % cc == 0 and gb_off % cc == 0
    kb, vb = k_off // kv_w, v_off // kv_w
    prev = lambda col: (lambda i: (jnp.maximum(QUERY_BLOCKS * i - 1, 0), col))
    col_chunk = lambda off: [pl.BlockSpec((tm, cc), functools.partial(lambda c, i: (i, c),
                                                                      off // cc + c))
                             for c in range(n_kv)]
    resident = lambda a: pl.BlockSpec(a.shape, lambda i: (0,) * a.ndim,
                                      pipeline_mode=pl.Buffered(1))
    kernel = functools.partial(_mix_kernel, n_kv=n_kv, pairs=pairs, scale=hd ** -0.5, per_b=per_b)
    return pl.pallas_call(
        kernel,
        grid=(m // tm,),
        in_specs=[pl.BlockSpec((tm, attn_w), lambda i: (i, q_off // attn_w)),
                  pl.BlockSpec((blk, kv_w), prev(kb)),
                  pl.BlockSpec((tm, kv_w), lambda i: (i, kb)),
                  pl.BlockSpec((blk, kv_w), prev(vb)),
                  pl.BlockSpec((tm, kv_w), lambda i: (i, vb)),
                  resident(band_bias),
                  pl.BlockSpec((tm, rec.shape[1]), lambda i: (i, 0))]
                 + col_chunk(ga_off) + col_chunk(gb_off)
                 + [pl.BlockSpec((tm, d), lambda i: (i, 0)),
                    pl.BlockSpec((None, 1, d), lambda i: (i // per_b, 0, 0)),
                    resident(w_lru), resident(w_att), resident(w_out)],
        out_specs=pl.BlockSpec((tm, d), lambda i: (i, 0)),
        out_shape=jax.ShapeDtypeStruct((m, d), F32),
        scratch_shapes=[pltpu.VMEM((tm, attn_w), BF16), pltpu.VMEM((tm, d), F32),
                        pltpu.VMEM((tm, d), BF16)],
        compiler_params=_params("parallel"),
        name="mix",
    )(rest, rest, rest, rest, rest, band_bias, rec, *([rest] * (2 * n_kv)), x2d, gate,
      w_lru, w_att, w_out)


def _mlp_kernel(x_ref, sh_ref, sc_ref, gate_ref, g2_ref, gf_ref, w1_ref, w2_ref, o_ref,
                h_ref, ff_ref, *, nf):
    f = pl.program_id(1)
    rc = x_ref.shape[0] // ROW_CHUNKS

    def up(rows, slot):
        ff = jnp.dot(h_ref[rows, :], w1_ref[...], preferred_element_type=F32)
        ff_ref[slot, rows, :] = jnp.square(jnp.maximum(ff, 0.0)).astype(BF16)

    def down(rows, slot):
        return jnp.dot(ff_ref[slot, rows, :], w2_ref[...], preferred_element_type=F32)

    @pl.when(f == 0)
    def _():
        o_ref[...] = jnp.zeros_like(o_ref)
        for r in range(ROW_CHUNKS):
            rows = pl.ds(r * rc, rc)
            h_ref[rows, :] = _rms_mod(x_ref[rows, :], g2_ref[...], sh_ref[...],
                                      sc_ref[...]).astype(BF16)
            up(rows, 0)

    @pl.when((f > 0) & (f < nf))
    def _():
        o_ref[...] += down(slice(None), (f - 1) % 2)
        up(slice(None), f % 2)

    @pl.when(f == nf)
    def _():
        for r in range(ROW_CHUNKS):
            rows = pl.ds(r * rc, rc)
            x2 = x_ref[rows, :] + gate_ref[...] * (o_ref[rows, :] + down(rows, (nf - 1) % 2))
            var = jnp.mean(x2 * x2, axis=-1, keepdims=True)
            o_ref[rows, :] = x2 * lax.rsqrt(var + EPS) * gf_ref[...]


def _mlp(x1, shift, scale, gate, g2, gf, w1, w2, seq):
    m, d = x1.shape
    dff = w1.shape[1]
    tm = _tile(seq, MLP_ROWS, SUBLANES * ROW_CHUNKS)
    tf = _tile(dff, MLP_FF_CHUNK, LANES)
    nf = dff // tf
    per_b = seq // tm
    row = lambda i, f: (i // per_b, 0, 0)
    const = lambda i, f: (0, 0)
    return pl.pallas_call(
        functools.partial(_mlp_kernel, nf=nf),
        grid=(m // tm, nf + 1),
        in_specs=[pl.BlockSpec((tm, d), lambda i, f: (i, 0)),
                  pl.BlockSpec((None, 1, d), row),
                  pl.BlockSpec((None, 1, d), row),
                  pl.BlockSpec((None, 1, d), row),
                  pl.BlockSpec((1, d), const),
                  pl.BlockSpec((1, d), const),
                  pl.BlockSpec((d, tf), lambda i, f: (0, jnp.minimum(f, nf - 1))),
                  pl.BlockSpec((tf, d), lambda i, f: (jnp.maximum(f - 1, 0), 0))],
        out_specs=pl.BlockSpec((tm, d), lambda i, f: (i, 0)),
        out_shape=jax.ShapeDtypeStruct((m, d), F32),
        scratch_shapes=[pltpu.VMEM((tm, d), BF16), pltpu.VMEM((2, tm, tf), BF16)],
        compiler_params=_params("parallel", "arbitrary"),
        name="mlp",
    )(x1, shift, scale, gate, g2, gf, w1, w2)


def _mlp_down_kernel(ff_ref, w2_ref, x_ref, gate_ref, gf_ref, o_ref):
    k = pl.program_id(1)
    last = pl.num_programs(1) - 1
    tm, d = o_ref.shape
    cc = max(d // COL_CHUNKS, LANES)
    rc = tm // ROW_CHUNKS

    @pl.when(k == 0)
    def _():
        for c in range(0, d, cc):
            o_ref[:, c:c + cc] = jnp.dot(ff_ref[...], w2_ref[:, c:c + cc],
                                         preferred_element_type=F32)

    @pl.when((k > 0) & (k < last))
    def _():
        for c in range(0, d, cc):
            o_ref[:, c:c + cc] += jnp.dot(ff_ref[...], w2_ref[:, c:c + cc],
                                          preferred_element_type=F32)

    @pl.when(k == last)
    def _():
        for r in range(ROW_CHUNKS):
            rows = pl.ds(r * rc, rc)
            y = o_ref[rows, :] + jnp.dot(ff_ref[rows, :], w2_ref[...], preferred_element_type=F32)
            x2 = x_ref[rows, :] + gate_ref[...] * y
            var = jnp.mean(x2 * x2, axis=-1, keepdims=True)
            o_ref[rows, :] = x2 * lax.rsqrt(var + EPS) * gf_ref[...]


def _mlp_down(ff, w2, x1, gate, gf, seq):
    m, dff = ff.shape
    d = w2.shape[1]
    tm = _tile(seq, MLP_ROWS, SUBLANES * ROW_CHUNKS)
    tk = _tile(dff // 2, MLP_DOWN_K, LANES)
    per_b = seq // tm
    return pl.pallas_call(
        _mlp_down_kernel,
        grid=(m // tm, dff // tk),
        in_specs=[pl.BlockSpec((tm, tk), lambda i, k: (i, k)),
                  pl.BlockSpec((tk, d), lambda i, k: (k, 0)),
                  pl.BlockSpec((tm, d), lambda i, k: (i, 0)),
                  pl.BlockSpec((None, 1, d), lambda i, k: (i // per_b, 0, 0)),
                  pl.BlockSpec((1, d), lambda i, k: (0, 0))],
        out_specs=pl.BlockSpec((tm, d), lambda i, k: (i, 0)),
        out_shape=jax.ShapeDtypeStruct((m, d), F32),
        compiler_params=_params("parallel", "arbitrary"),
        name="mlp_down",
    )(ff, w2, x1, gate, gf)


def kernel(x, c, w_ada, b_ada, norm1_g, w_in, conv_w, conv_b, lru_wa, lru_ba, lru_wx, lru_bx,
           lru_lambda, w_lru_out, w_attn_out, attn_sinks, rel_bias, w_out, norm2_g, w_ff1, w_ff2,
           final_g):
    bsz, seq, d = x.shape
    depth = w_in.shape[0]
    lru_w = w_lru_out.shape[1]
    attn_w = w_attn_out.shape[1]
    kv_w = (w_in.shape[2] - 2 * lru_w - attn_w - 2 * d) // 2
    n_heads = attn_sinks.shape[1]
    n_kv = kv_w // (attn_w // n_heads)
    q_off = 2 * lru_w
    k_off = q_off + attn_w
    v_off = k_off + kv_w
    ga_off = v_off + kv_w
    gb_off = ga_off + d

    xs = x.reshape(bsz * seq, d)
    for l in range(depth):
        band_bias = _band_bias(rel_bias, attn_sinks[l], n_kv)
        mod = _adaln(c, w_ada[l], b_ada[l])
        shift1, scale1, gate1, shift2, scale2, gate2 = [
            t.reshape(bsz, 1, d) for t in jnp.split(mod, 6, axis=-1)]

        rest = _in_proj(xs, shift1, scale1, norm1_g[l].reshape(1, d), w_in[l].astype(BF16), seq,
                        IN_PROJ_COLS, "in_proj")
        wcat = jnp.concatenate([lru_wa[l], lru_wx[l]], axis=-1).astype(BF16)
        rec = _rglru(rest, wcat, conv_w[l], conv_b[l], lru_ba[l], lru_bx[l], lru_lambda[l],
                     bsz, seq, lru_w)
        x1 = _mix(rest, rec, xs, gate1, band_bias, w_lru_out[l].astype(BF16),
                  w_attn_out[l].astype(BF16), w_out[l].astype(BF16), seq, q_off, k_off, v_off,
                  ga_off, gb_off, kv_w, n_heads)
        if l != depth - 1:
            raise NotImplementedError("the fused final norm assumes a single layer")
        ff = _in_proj(x1, shift2, scale2, norm2_g[l].reshape(1, d), w_ff1[l].astype(BF16), seq,
                      MLP_UP_COLS, "mlp_up", sq_relu=True)
        xs = _mlp_down(ff, w_ff2[l].astype(BF16), x1, gate2, final_g.reshape(1, d), seq)
    return xs.reshape(bsz, seq, d)
```

```python
import functools
import math

import jax
import jax.numpy as jnp
from jax import lax
from jax.experimental import pallas as pl
from jax.experimental.pallas import tpu as pltpu

EPS = 1e-6
LRU_C = 8.0
LOG2E = 1.4426950408889634
ATTN_BLOCK = 128
NEG_INF = -1e30
MAX_DISTANCE = 128
LANES = 128
SUBLANES = 8
VMEM_LIMIT = 56 * 1024 * 1024
IN_PROJ_ROWS, IN_PROJ_COLS = 1024, 2688
RGLRU_STEPS, RGLRU_CHANNELS = 256, 512
MLP_ROWS, MLP_FF_CHUNK = 512, 1024
MLP_UP_COLS, MLP_DOWN_K = 2048, 2048
COL_CHUNKS = 4
ADALN_COLS = 1024
ROW_CHUNKS = 2
IN_PROJ_ROW_CHUNKS = 4
QUERY_BLOCKS = 2

F32 = jnp.float32
BF16 = jnp.bfloat16


def _tile(n, target, align):
    best = None
    t = align
    while t <= min(n, target):
        if n % t == 0:
            best = t
        t += align
    if best is None:
        raise ValueError(f"no tile for {n} (target {target}, align {align})")
    return best


def _params(*sem):
    return pltpu.CompilerParams(dimension_semantics=sem, vmem_limit_bytes=VMEM_LIMIT)


def _sigmoid(v):
    return 1.0 / (1.0 + jnp.exp2(v * -LOG2E))


def _rms_mod(x, g, shift, scale):
    var = jnp.mean(x * x, axis=-1, keepdims=True)
    y = x * lax.rsqrt(var + EPS) * g
    return y * (1.0 + scale) + shift


def _adaln_kernel(c_ref, w_ref, b_ref, o_ref):
    c = c_ref[...]
    act = (c * _sigmoid(c)).astype(BF16)
    o_ref[...] = jnp.dot(act, w_ref[...].astype(BF16), preferred_element_type=F32) + b_ref[...]


def _adaln(c, w, b):
    bsz, d = c.shape
    n = w.shape[1]
    tn = _tile(n, ADALN_COLS, LANES)
    return pl.pallas_call(
        _adaln_kernel,
        grid=(n // tn,),
        in_specs=[pl.BlockSpec((bsz, d), lambda j: (0, 0)),
                  pl.BlockSpec((d, tn), lambda j: (0, j)),
                  pl.BlockSpec((1, tn), lambda j: (0, j))],
        out_specs=pl.BlockSpec((bsz, tn), lambda j: (0, j)),
        out_shape=jax.ShapeDtypeStruct((bsz, n), F32),
        compiler_params=_params("parallel"),
        name="adaln_mod",
    )(c, w, b.reshape(1, n))


def _gelu_tanh(v):
    k1 = -2.0 * math.sqrt(2.0 / math.pi) * LOG2E
    return v / (1.0 + jnp.exp2(v * (k1 + (k1 * 0.044715) * (v * v))))


def _rglru_kernel(x_ref, gate_ref, w_ref, cw_ref, cb_ref, ba_ref, bx_ref, lam_ref, o_ref,
                  lx_ref, xtail, hcar, hm_ref, *, conv_k):
    bsz, tt, cw = x_ref.shape
    slabs = cw // LANES
    seq_start = pl.program_id(1) == 0

    @pl.when(seq_start)
    def _():
        xtail[...] = jnp.zeros_like(xtail)
        hcar[...] = jnp.zeros_like(hcar)

    for b in range(bsz):
        xb = x_ref[b].astype(F32)
        for s in range(slabs):
            lx_ref[s, pl.ds(b, tt, stride=SUBLANES), :] = xb[:, s * LANES:(s + 1) * LANES]

    lam = lam_ref[...]
    softplus_neg_lam = jnp.maximum(-lam, 0.0) + jnp.log(1.0 + jnp.exp(-jnp.abs(lam)))
    log2_a_coef = (-LRU_C * LOG2E) * softplus_neg_lam

    def conv_and_gates(s):
        sl = slice(s * LANES, (s + 1) * LANES)
        x3 = lx_ref[s].reshape(tt, SUBLANES, LANES)
        xs = jnp.concatenate([xtail[s], x3], axis=0)
        xtail[s] = xs[tt:]
        xc = cb_ref[:, sl] + cw_ref[conv_k - 1:conv_k, sl] * x3
        for k in range(conv_k - 1):
            xc = xc + cw_ref[k:k + 1, sl] * xs[k:k + tt]
        z = jnp.dot(xc.reshape(tt * SUBLANES, LANES).astype(BF16), w_ref[s],
                    preferred_element_type=F32)
        return xc, z

    ahead = conv_and_gates(0)
    for s in range(slabs):
        sl = slice(s * LANES, (s + 1) * LANES)
        xc, z = ahead
        if s + 1 < slabs:
            ahead = conv_and_gates(s + 1)
        r = _sigmoid(z[:, :LANES] + ba_ref[:, sl]).reshape(tt, SUBLANES, LANES)
        i = _sigmoid(z[:, LANES:] + bx_ref[:, sl]).reshape(tt, SUBLANES, LANES)
        a = jnp.exp2(log2_a_coef[:, sl] * r)
        om = 1.0 - a * a
        mult = jnp.where(om > 0.0, om * lax.rsqrt(om), 0.0)
        mult = jnp.concatenate([jnp.where(seq_start, 1.0, mult[:1]), mult[1:]], axis=0)
        u = mult * (i * xc)
        h = hcar[s]
        for t in range(tt):
            h = a[t] * h + u[t]
            hm_ref[s, t * SUBLANES:(t + 1) * SUBLANES, :] = h
        hcar[s] = h
        for b in range(bsz):
            hb = hm_ref[s, pl.ds(b, tt, stride=SUBLANES), :]
            gate = gate_ref[b, :, sl].astype(F32)
            o_ref[b, :, sl] = (hb * _gelu_tanh(gate)).astype(o_ref.dtype)


def _rglru(proj, wcat, conv_w, conv_b, ba, bx, lam, bsz, seq, lru_w):
    assert bsz == SUBLANES
    n = proj.shape[1]
    conv_k = conv_w.shape[0]
    tt = _tile(seq, RGLRU_STEPS, SUBLANES)
    cw = _tile(lru_w, RGLRU_CHANNELS, LANES)
    slabs = cw // LANES
    vec = pl.BlockSpec((1, cw), lambda c, t: (0, c))
    rec = pl.pallas_call(
        functools.partial(_rglru_kernel, conv_k=conv_k),
        grid=(lru_w // cw, seq // tt),
        in_specs=[pl.BlockSpec((bsz, tt, cw), lambda c, t: (0, t, c)),
                  pl.BlockSpec((bsz, tt, cw), lambda c, t: (0, t, lru_w // cw + c)),
                  pl.BlockSpec((slabs, LANES, 2 * LANES), lambda c, t: (c, 0, 0)),
                  pl.BlockSpec((conv_k, cw), lambda c, t: (0, c)),
                  vec, vec, vec, vec],
        out_specs=pl.BlockSpec((bsz, tt, cw), lambda c, t: (0, t, c)),
        out_shape=jax.ShapeDtypeStruct((bsz, seq, lru_w), BF16),
        scratch_shapes=[pltpu.VMEM((slabs, bsz * tt, LANES), F32),
                        pltpu.VMEM((slabs, conv_k - 1, SUBLANES, LANES), F32),
                        pltpu.VMEM((slabs, SUBLANES, LANES), F32),
                        pltpu.VMEM((slabs, bsz * tt, LANES), F32)],
        compiler_params=_params("parallel", "arbitrary"),
        name="rglru",
    )(proj.reshape(bsz, seq, n), proj.reshape(bsz, seq, n), wcat, conv_w, conv_b.reshape(1, -1),
      ba.reshape(1, -1), bx.reshape(1, -1), lam.reshape(1, -1))
    return rec.reshape(bsz * seq, lru_w)


def _in_proj_kernel(x_ref, sh_ref, sc_ref, g_ref, w_ref, o_ref, h_ref, *, sq_relu):
    j = pl.program_id(1)
    rc = x_ref.shape[0] // IN_PROJ_ROW_CHUNKS

    def project(h):
        y = jnp.dot(h, w_ref[...], preferred_element_type=F32)
        if sq_relu:
            y = jnp.square(jnp.maximum(y, 0.0))
        return y.astype(o_ref.dtype)

    @pl.when(j == 0)
    def _():
        for r in range(IN_PROJ_ROW_CHUNKS):
            rows = pl.ds(r * rc, rc)
            h = _rms_mod(x_ref[rows, :], g_ref[...], sh_ref[...], sc_ref[...]).astype(BF16)
            h_ref[rows, :] = h
            o_ref[rows, :] = project(h)

    @pl.when(j > 0)
    def _():
        o_ref[...] = project(h_ref[...])


def _in_proj(x2d, shift, scale, g, w, seq, tn_target, name, sq_relu=False):
    m, d = x2d.shape
    n = w.shape[1]
    tm = _tile(seq, IN_PROJ_ROWS, SUBLANES * IN_PROJ_ROW_CHUNKS)
    tn = _tile(n, tn_target, LANES)
    per_b = seq // tm
    row = lambda i, j: (i // per_b, 0, 0)
    return pl.pallas_call(
        functools.partial(_in_proj_kernel, sq_relu=sq_relu),
        grid=(m // tm, n // tn),
        in_specs=[pl.BlockSpec((tm, d), lambda i, j: (i, 0)),
                  pl.BlockSpec((None, 1, d), row),
                  pl.BlockSpec((None, 1, d), row),
                  pl.BlockSpec((1, d), lambda i, j: (0, 0)),
                  pl.BlockSpec((d, tn), lambda i, j: (0, j))],
        out_specs=pl.BlockSpec((tm, tn), lambda i, j: (i, j)),
        out_shape=jax.ShapeDtypeStruct((m, n), BF16),
        scratch_shapes=[pltpu.VMEM((tm, d), BF16)],
        compiler_params=_params("parallel", "arbitrary"),
        name=name,
    )(x2d, shift, scale, g, w)


def _t5_bucket(rel, n_buckets):
    max_exact = n_buckets // 2
    relf = jnp.maximum(rel, 1).astype(F32)
    large = max_exact + (jnp.log(relf / max_exact) / math.log(MAX_DISTANCE / max_exact)
                         * (n_buckets - max_exact)).astype(jnp.int32)
    large = jnp.minimum(large, n_buckets - 1)
    return jnp.where(rel < max_exact, rel, large)


def _band_bias_kernel(rb_ref, sink_ref, bucket_ref, o_ref, *, pairs, n_buckets):
    kv = pl.program_id(0)
    blk = ATTN_BLOCK
    bucket = bucket_ref[...]
    qi = lax.broadcasted_iota(jnp.int32, (blk, 2 * blk), 0)
    ki = lax.broadcasted_iota(jnp.int32, (blk, 2 * blk), 1)
    rel = qi + blk - ki
    valid = (rel >= 0) & (rel < blk)
    for p in range(pairs):
        for par in range(2):
            h = (kv * pairs + p) * 2 + par
            bias = jnp.zeros((blk, 2 * blk), F32)
            for b in range(n_buckets):
                bias = jnp.where(bucket == b, rb_ref[b, h], bias)
            sink = sink_ref[h]
            rows, cols = slice(p * blk, (p + 1) * blk), slice(par * 2 * blk, (par + 1) * 2 * blk)
            o_ref[rows, cols] = jnp.where(ki == 0, sink, jnp.where(valid, bias, NEG_INF))


def _band_bias(rel_bias, sinks, n_kv):
    n_buckets, n_heads = rel_bias.shape
    pairs = n_heads // n_kv // 2
    blk = ATTN_BLOCK
    qi = jnp.arange(blk)[:, None]
    ki = jnp.arange(2 * blk)[None, :]
    bucket = _t5_bucket(jnp.maximum(qi + blk - ki, 0), n_buckets)
    smem = pl.BlockSpec(memory_space=pltpu.SMEM)
    return pl.pallas_call(
        functools.partial(_band_bias_kernel, pairs=pairs, n_buckets=n_buckets),
        grid=(n_kv,),
        in_specs=[smem, smem, pl.BlockSpec((blk, 2 * blk), lambda kv: (0, 0))],
        out_specs=pl.BlockSpec((None, pairs * blk, 4 * blk), lambda kv: (kv, 0, 0)),
        out_shape=jax.ShapeDtypeStruct((n_kv, pairs * blk, 4 * blk), F32),
        compiler_params=_params("parallel"),
        name="band_bias",
    )(rel_bias, sinks, bucket)


def _swap_lane_halves(v):
    u = pltpu.bitcast(v, jnp.uint32)
    return pltpu.bitcast(pltpu.roll(u, LANES // 2, axis=1), BF16)


def _mix_kernel(*refs, n_kv, pairs, scale, per_b):
    q_ref, kp_ref, kc_ref, vp_ref, vc_ref, bm_ref, rec_ref = refs[:7]
    ga_refs, gb_refs = refs[7:7 + n_kv], refs[7 + n_kv:7 + 2 * n_kv]
    x_ref, gate_ref, wl_ref, wa_ref, wo_ref, o_ref, att_ref, ya_ref, m_ref = refs[7 + 2 * n_kv:]
    blk = ATTN_BLOCK
    d = o_ref.shape[1]
    cc = d // n_kv
    fold = math.log2(scale).is_integer()
    lane = lax.broadcasted_iota(jnp.int32, (blk, LANES), 1)
    key = lax.broadcasted_iota(jnp.int32, (blk, LANES), 0)
    low = jnp.where(lane < LANES // 2, 1.0, 0.0).astype(BF16)
    high = jnp.where(lane < LANES // 2, 0.0, 1.0).astype(BF16)
    not_key0 = jnp.where(key == 0, 0.0, 1.0).astype(BF16)
    ones_blk = jnp.concatenate([low, low, high, high], axis=0)
    key_col = lax.broadcasted_iota(jnp.int32, (1, 4 * blk), 1) % (2 * blk)
    seq_start = pl.program_id(0) % per_b == 0
    start_mask = jnp.where(seq_start & (key_col >= 1) & (key_col < blk), NEG_INF, 0.0)

    def halves(t, kv):
        s = _swap_lane_halves(t)
        lo, hi = (t, s) if kv % 2 == 0 else (s, t)
        return lo * low, hi * high

    def window(prev, cur):
        return jnp.concatenate([prev[0] * not_key0, cur[0], prev[1] * not_key0, cur[1]], axis=0)

    for kv in range(n_kv):
        sl = slice((kv // 2) * LANES, (kv // 2 + 1) * LANES)
        kb = [halves(kp_ref[:, sl], kv)] + [halves(kc_ref[t * blk:(t + 1) * blk, sl], kv)
                                            for t in range(QUERY_BLOCKS)]
        vb = [halves(vp_ref[:, sl], kv)] + [halves(vc_ref[t * blk:(t + 1) * blk, sl], kv)
                                            for t in range(QUERY_BLOCKS)]
        for t in range(QUERY_BLOCKS):
            rows = slice(t * blk, (t + 1) * blk)
            kblk = window(kb[t], kb[t + 1])
            vaug = jnp.concatenate([window(vb[t], vb[t + 1]), ones_blk], axis=1)
            q = jnp.concatenate(
                [q_ref[rows, (kv * pairs + p) * LANES:(kv * pairs + p + 1) * LANES]
                 for p in range(pairs)], axis=0)
            if fold:
                q = q * scale
            s = lax.dot_general(q, kblk, (((1,), (1,)), ((), ())), preferred_element_type=F32)
            if not fold:
                s = s * scale
            s = s + bm_ref[kv]
            if t == 0:
                s = s + start_mask
            es = []
            for par in range(2):
                sp = s[:, par * 2 * blk:(par + 1) * 2 * blk]
                es.append(jnp.exp(sp - jnp.max(sp, axis=-1, keepdims=True)).astype(BF16))
            o = jnp.dot(jnp.concatenate(es, axis=1), vaug, preferred_element_type=F32)
            res = (o[:, :LANES] / o[:, LANES:]).astype(BF16)
            for p in range(pairs):
                att_ref[rows, (kv * pairs + p) * LANES:(kv * pairs + p + 1) * LANES] = (
                    res[p * blk:(p + 1) * blk])
        cols = slice(kv * cc, (kv + 1) * cc)
        ya = jnp.dot(rec_ref[...], wl_ref[:, cols], preferred_element_type=F32)
        ya_ref[:, cols] = _sigmoid(ga_refs[kv][...].astype(F32)) * ya

    for c in range(n_kv):
        cols = slice(c * cc, (c + 1) * cc)
        yb = jnp.dot(att_ref[...], wa_ref[:, cols], preferred_element_type=F32)
        gb = _sigmoid(gb_refs[c][...].astype(F32))
        m_ref[:, cols] = (ya_ref[:, cols] + gb * yb).astype(BF16)
    for c in range(n_kv):
        cols = slice(c * cc, (c + 1) * cc)
        y = jnp.dot(m_ref[...], wo_ref[:, cols], preferred_element_type=F32)
        o_ref[:, cols] = x_ref[:, cols] + gate_ref[:, cols] * y


def _mix(rest, rec, x2d, gate, band_bias, w_lru, w_att, w_out, seq, q_off, k_off, v_off, ga_off,
         gb_off, kv_w, n_heads):
    m, d = x2d.shape
    attn_w = w_att.shape[0]
    hd = attn_w // n_heads
    n_kv = kv_w // hd
    pairs = n_heads // n_kv // 2
    blk = ATTN_BLOCK
    tm = QUERY_BLOCKS * blk
    per_b = seq // tm
    cc = d // n_kv
    assert 2 * hd == LANES and n_heads == 2 * pairs * n_kv and kv_w % LANES == 0
    assert q_off % attn_w == 0 and k_off % kv_w == 0 and v_off % kv_w == 0
    assert seq % tm == 0 and cc % LANES == 0 and ga_off % cc == 0 and gb_off % cc == 0
    kb, vb = k_off // kv_w, v_off // kv_w
    prev = lambda col: (lambda i: (jnp.maximum(QUERY_BLOCKS * i - 1, 0), col))
    col_chunk = lambda off: [pl.BlockSpec((tm, cc), functools.partial(lambda c, i: (i, c),
                                                                      off // cc + c))
                             for c in range(n_kv)]
    resident = lambda a: pl.BlockSpec(a.shape, lambda i: (0,) * a.ndim,
                                      pipeline_mode=pl.Buffered(1))
    kernel = functools.partial(_mix_kernel, n_kv=n_kv, pairs=pairs, scale=hd ** -0.5, per_b=per_b)
    return pl.pallas_call(
        kernel,
        grid=(m // tm,),
        in_specs=[pl.BlockSpec((tm, attn_w), lambda i: (i, q_off // attn_w)),
                  pl.BlockSpec((blk, kv_w), prev(kb)),
                  pl.BlockSpec((tm, kv_w), lambda i: (i, kb)),
                  pl.BlockSpec((blk, kv_w), prev(vb)),
                  pl.BlockSpec((tm, kv_w), lambda i: (i, vb)),
                  resident(band_bias),
                  pl.BlockSpec((tm, rec.shape[1]), lambda i: (i, 0))]
                 + col_chunk(ga_off) + col_chunk(gb_off)
                 + [pl.BlockSpec((tm, d), lambda i: (i, 0)),
                    pl.BlockSpec((None, 1, d), lambda i: (i // per_b, 0, 0)),
                    resident(w_lru), resident(w_att), resident(w_out)],
        out_specs=pl.BlockSpec((tm, d), lambda i: (i, 0)),
        out_shape=jax.ShapeDtypeStruct((m, d), F32),
        scratch_shapes=[pltpu.VMEM((tm, attn_w), BF16), pltpu.VMEM((tm, d), F32),
                        pltpu.VMEM((tm, d), BF16)],
        compiler_params=_params("parallel"),
        name="mix",
    )(rest, rest, rest, rest, rest, band_bias, rec, *([rest] * (2 * n_kv)), x2d, gate,
      w_lru, w_att, w_out)


def _mlp_kernel(x_ref, sh_ref, sc_ref, gate_ref, g2_ref, gf_ref, w1_ref, w2_ref, o_ref,
                h_ref, ff_ref, *, nf):
    f = pl.program_id(1)
    rc = x_ref.shape[0] // ROW_CHUNKS

    def up(rows, slot):
        ff = jnp.dot(h_ref[rows, :], w1_ref[...], preferred_element_type=F32)
        ff_ref[slot, rows, :] = jnp.square(jnp.maximum(ff, 0.0)).astype(BF16)

    def down(rows, slot):
        return jnp.dot(ff_ref[slot, rows, :], w2_ref[...], preferred_element_type=F32)

    @pl.when(f == 0)
    def _():
        o_ref[...] = jnp.zeros_like(o_ref)
        for r in range(ROW_CHUNKS):
            rows = pl.ds(r * rc, rc)
            h_ref[rows, :] = _rms_mod(x_ref[rows, :], g2_ref[...], sh_ref[...],
                                      sc_ref[...]).astype(BF16)
            up(rows, 0)

    @pl.when((f > 0) & (f < nf))
    def _():
        o_ref[...] += down(slice(None), (f - 1) % 2)
        up(slice(None), f % 2)

    @pl.when(f == nf)
    def _():
        for r in range(ROW_CHUNKS):
            rows = pl.ds(r * rc, rc)
            x2 = x_ref[rows, :] + gate_ref[...] * (o_ref[rows, :] + down(rows, (nf - 1) % 2))
            var = jnp.mean(x2 * x2, axis=-1, keepdims=True)
            o_ref[rows, :] = x2 * lax.rsqrt(var + EPS) * gf_ref[...]


def _mlp(x1, shift, scale, gate, g2, gf, w1, w2, seq):
    m, d = x1.shape
    dff = w1.shape[1]
    tm = _tile(seq, MLP_ROWS, SUBLANES * ROW_CHUNKS)
    tf = _tile(dff, MLP_FF_CHUNK, LANES)
    nf = dff // tf
    per_b = seq // tm
    row = lambda i, f: (i // per_b, 0, 0)
    const = lambda i, f: (0, 0)
    return pl.pallas_call(
        functools.partial(_mlp_kernel, nf=nf),
        grid=(m // tm, nf + 1),
        in_specs=[pl.BlockSpec((tm, d), lambda i, f: (i, 0)),
                  pl.BlockSpec((None, 1, d), row),
                  pl.BlockSpec((None, 1, d), row),
                  pl.BlockSpec((None, 1, d), row),
                  pl.BlockSpec((1, d), const),
                  pl.BlockSpec((1, d), const),
                  pl.BlockSpec((d, tf), lambda i, f: (0, jnp.minimum(f, nf - 1))),
                  pl.BlockSpec((tf, d), lambda i, f: (jnp.maximum(f - 1, 0), 0))],
        out_specs=pl.BlockSpec((tm, d), lambda i, f: (i, 0)),
        out_shape=jax.ShapeDtypeStruct((m, d), F32),
        scratch_shapes=[pltpu.VMEM((tm, d), BF16), pltpu.VMEM((2, tm, tf), BF16)],
        compiler_params=_params("parallel", "arbitrary"),
        name="mlp",
    )(x1, shift, scale, gate, g2, gf, w1, w2)


def _mlp_down_kernel(ff_ref, w2_ref, x_ref, gate_ref, gf_ref, o_ref):
    k = pl.program_id(1)
    last = pl.num_programs(1) - 1
    tm, d = o_ref.shape
    cc = max(d // COL_CHUNKS, LANES)
    rc = tm // ROW_CHUNKS

    @pl.when(k == 0)
    def _():
        for c in range(0, d, cc):
            o_ref[:, c:c + cc] = jnp.dot(ff_ref[...], w2_ref[:, c:c + cc],
                                         preferred_element_type=F32)

    @pl.when((k > 0) & (k < last))
    def _():
        for c in range(0, d, cc):
            o_ref[:, c:c + cc] += jnp.dot(ff_ref[...], w2_ref[:, c:c + cc],
                                          preferred_element_type=F32)

    @pl.when(k == last)
    def _():
        for r in range(ROW_CHUNKS):
            rows = pl.ds(r * rc, rc)
            y = o_ref[rows, :] + jnp.dot(ff_ref[rows, :], w2_ref[...], preferred_element_type=F32)
            x2 = x_ref[rows, :] + gate_ref[...] * y
            var = jnp.mean(x2 * x2, axis=-1, keepdims=True)
            o_ref[rows, :] = x2 * lax.rsqrt(var + EPS) * gf_ref[...]


def _mlp_down(ff, w2, x1, gate, gf, seq):
    m, dff = ff.shape
    d = w2.shape[1]
    tm = _tile(seq, MLP_ROWS, SUBLANES * ROW_CHUNKS)
    tk = _tile(dff // 2, MLP_DOWN_K, LANES)
    per_b = seq // tm
    return pl.pallas_call(
        _mlp_down_kernel,
        grid=(m // tm, dff // tk),
        in_specs=[pl.BlockSpec((tm, tk), lambda i, k: (i, k)),
                  pl.BlockSpec((tk, d), lambda i, k: (k, 0)),
                  pl.BlockSpec((tm, d), lambda i, k: (i, 0)),
                  pl.BlockSpec((None, 1, d), lambda i, k: (i // per_b, 0, 0)),
                  pl.BlockSpec((1, d), lambda i, k: (0, 0))],
        out_specs=pl.BlockSpec((tm, d), lambda i, k: (i, 0)),
        out_shape=jax.ShapeDtypeStruct((m, d), F32),
        compiler_params=_params("parallel", "arbitrary"),
        name="mlp_down",
    )(ff, w2, x1, gate, gf)


def kernel(x, c, w_ada, b_ada, norm1_g, w_in, conv_w, conv_b, lru_wa, lru_ba, lru_wx, lru_bx,
           lru_lambda, w_lru_out, w_attn_out, attn_sinks, rel_bias, w_out, norm2_g, w_ff1, w_ff2,
           final_g):
    bsz, seq, d = x.shape
    depth = w_in.shape[0]
    lru_w = w_lru_out.shape[1]
    attn_w = w_attn_out.shape[1]
    kv_w = (w_in.shape[2] - 2 * lru_w - attn_w - 2 * d) // 2
    n_heads = attn_sinks.shape[1]
    n_kv = kv_w // (attn_w // n_heads)
    q_off = 2 * lru_w
    k_off = q_off + attn_w
    v_off = k_off + kv_w
    ga_off = v_off + kv_w
    gb_off = ga_off + d

    xs = x.reshape(bsz * seq, d)
    for l in range(depth):
        band_bias = _band_bias(rel_bias, attn_sinks[l], n_kv)
        mod = _adaln(c, w_ada[l], b_ada[l])
        shift1, scale1, gate1, shift2, scale2, gate2 = [
            t.reshape(bsz, 1, d) for t in jnp.split(mod, 6, axis=-1)]

        rest = _in_proj(xs, shift1, scale1, norm1_g[l].reshape(1, d), w_in[l].astype(BF16), seq,
                        IN_PROJ_COLS, "in_proj")
        wcat = jnp.concatenate([lru_wa[l], lru_wx[l]], axis=-1).astype(BF16)
        rec = _rglru(rest, wcat, conv_w[l], conv_b[l], lru_ba[l], lru_bx[l], lru_lambda[l],
                     bsz, seq, lru_w)
        x1 = _mix(rest, rec, xs, gate1, band_bias, w_lru_out[l].astype(BF16),
                  w_attn_out[l].astype(BF16), w_out[l].astype(BF16), seq, q_off, k_off, v_off,
                  ga_off, gb_off, kv_w, n_heads)
        if l != depth - 1:
            raise NotImplementedError("the fused final norm assumes a single layer")
        ff = _in_proj(x1, shift2, scale2, norm2_g[l].reshape(1, d), w_ff1[l].astype(BF16), seq,
                      MLP_UP_COLS, "mlp_up", sq_relu=True)
        xs = _mlp_down(ff, w_ff2[l].astype(BF16), x1, gate2, final_g.reshape(1, d), seq)
    return xs.reshape(bsz, seq, d)
```

```python
import functools
import math

import jax
import jax.numpy as jnp
from jax import lax
from jax.experimental import pallas as pl
from jax.experimental.pallas import tpu as pltpu

EPS = 1e-6
LRU_C = 8.0
LOG2E = 1.4426950408889634
ATTN_BLOCK = 128
NEG_INF = -1e30
MAX_DISTANCE = 128
LANES = 128
SUBLANES = 8
VMEM_LIMIT = 56 * 1024 * 1024
IN_PROJ_ROWS, IN_PROJ_COLS = 1024, 1792
RGLRU_STEPS, RGLRU_CHANNELS = 256, 512
MLP_ROWS, MLP_FF_CHUNK = 512, 1024
MLP_UP_COLS, MLP_DOWN_K = 2048, 2048
COL_CHUNKS = 4
ADALN_COLS = 1024
ROW_CHUNKS = 2
IN_PROJ_ROW_CHUNKS = 4
QUERY_BLOCKS = 2

F32 = jnp.float32
BF16 = jnp.bfloat16


def _tile(n, target, align):
    best = None
    t = align
    while t <= min(n, target):
        if n % t == 0:
            best = t
        t += align
    if best is None:
        raise ValueError(f"no tile for {n} (target {target}, align {align})")
    return best


def _params(*sem):
    return pltpu.CompilerParams(dimension_semantics=sem, vmem_limit_bytes=VMEM_LIMIT)


def _sigmoid(v):
    return 1.0 / (1.0 + jnp.exp2(v * -LOG2E))


def _rms_mod(x, g, shift, scale):
    var = jnp.mean(x * x, axis=-1, keepdims=True)
    y = x * lax.rsqrt(var + EPS) * g
    return y * (1.0 + scale) + shift


def _adaln_kernel(c_ref, w_ref, b_ref, o_ref):
    c = c_ref[...]
    act = (c * _sigmoid(c)).astype(BF16)
    o_ref[...] = jnp.dot(act, w_ref[...].astype(BF16), preferred_element_type=F32) + b_ref[...]


def _adaln(c, w, b):
    bsz, d = c.shape
    n = w.shape[1]
    tn = _tile(n, ADALN_COLS, LANES)
    return pl.pallas_call(
        _adaln_kernel,
        grid=(n // tn,),
        in_specs=[pl.BlockSpec((bsz, d), lambda j: (0, 0)),
                  pl.BlockSpec((d, tn), lambda j: (0, j)),
                  pl.BlockSpec((1, tn), lambda j: (0, j))],
        out_specs=pl.BlockSpec((bsz, tn), lambda j: (0, j)),
        out_shape=jax.ShapeDtypeStruct((bsz, n), F32),
        compiler_params=_params("parallel"),
        name="adaln_mod",
    )(c, w, b.reshape(1, n))


def _gelu_tanh(v):
    k1 = -2.0 * math.sqrt(2.0 / math.pi) * LOG2E
    return v / (1.0 + jnp.exp2(v * (k1 + (k1 * 0.044715) * (v * v))))


def _rglru_kernel(x_ref, gate_ref, w_ref, cw_ref, cb_ref, ba_ref, bx_ref, lam_ref, o_ref,
                  lx_ref, xtail, hcar, hm_ref, *, conv_k):
    bsz, tt, cw = x_ref.shape
    slabs = cw // LANES
    seq_start = pl.program_id(1) == 0

    @pl.when(seq_start)
    def _():
        xtail[...] = jnp.zeros_like(xtail)
        hcar[...] = jnp.zeros_like(hcar)

    for b in range(bsz):
        xb = x_ref[b].astype(F32)
        for s in range(slabs):
            lx_ref[s, pl.ds(b, tt, stride=SUBLANES), :] = xb[:, s * LANES:(s + 1) * LANES]

    lam = lam_ref[...]
    softplus_neg_lam = jnp.maximum(-lam, 0.0) + jnp.log(1.0 + jnp.exp(-jnp.abs(lam)))
    log2_a_coef = (-LRU_C * LOG2E) * softplus_neg_lam

    def conv_and_gates(s):
        sl = slice(s * LANES, (s + 1) * LANES)
        x3 = lx_ref[s].reshape(tt, SUBLANES, LANES)
        xs = jnp.concatenate([xtail[s], x3], axis=0)
        xtail[s] = xs[tt:]
        xc = cb_ref[:, sl] + cw_ref[conv_k - 1:conv_k, sl] * x3
        for k in range(conv_k - 1):
            xc = xc + cw_ref[k:k + 1, sl] * xs[k:k + tt]
        z = jnp.dot(xc.reshape(tt * SUBLANES, LANES).astype(BF16), w_ref[s],
                    preferred_element_type=F32)
        return xc, z

    ahead = conv_and_gates(0)
    for s in range(slabs):
        sl = slice(s * LANES, (s + 1) * LANES)
        xc, z = ahead
        if s + 1 < slabs:
            ahead = conv_and_gates(s + 1)
        r = _sigmoid(z[:, :LANES] + ba_ref[:, sl]).reshape(tt, SUBLANES, LANES)
        i = _sigmoid(z[:, LANES:] + bx_ref[:, sl]).reshape(tt, SUBLANES, LANES)
        a = jnp.exp2(log2_a_coef[:, sl] * r)
        om = 1.0 - a * a
        mult = jnp.where(om > 0.0, om * lax.rsqrt(om), 0.0)
        mult = jnp.concatenate([jnp.where(seq_start, 1.0, mult[:1]), mult[1:]], axis=0)
        u = mult * (i * xc)
        h = hcar[s]
        for t in range(tt):
            h = a[t] * h + u[t]
            hm_ref[s, t * SUBLANES:(t + 1) * SUBLANES, :] = h
        hcar[s] = h
        for b in range(bsz):
            hb = hm_ref[s, pl.ds(b, tt, stride=SUBLANES), :]
            gate = gate_ref[b, :, sl].astype(F32)
            o_ref[b, :, sl] = (hb * _gelu_tanh(gate)).astype(o_ref.dtype)


def _rglru(proj, wcat, conv_w, conv_b, ba, bx, lam, bsz, seq, lru_w):
    assert bsz == SUBLANES
    n = proj.shape[1]
    conv_k = conv_w.shape[0]
    tt = _tile(seq, RGLRU_STEPS, SUBLANES)
    cw = _tile(lru_w, RGLRU_CHANNELS, LANES)
    slabs = cw // LANES
    vec = pl.BlockSpec((1, cw), lambda c, t: (0, c))
    rec = pl.pallas_call(
        functools.partial(_rglru_kernel, conv_k=conv_k),
        grid=(lru_w // cw, seq // tt),
        in_specs=[pl.BlockSpec((bsz, tt, cw), lambda c, t: (0, t, c)),
                  pl.BlockSpec((bsz, tt, cw), lambda c, t: (0, t, lru_w // cw + c)),
                  pl.BlockSpec((slabs, LANES, 2 * LANES), lambda c, t: (c, 0, 0)),
                  pl.BlockSpec((conv_k, cw), lambda c, t: (0, c)),
                  vec, vec, vec, vec],
        out_specs=pl.BlockSpec((bsz, tt, cw), lambda c, t: (0, t, c)),
        out_shape=jax.ShapeDtypeStruct((bsz, seq, lru_w), BF16),
        scratch_shapes=[pltpu.VMEM((slabs, bsz * tt, LANES), F32),
                        pltpu.VMEM((slabs, conv_k - 1, SUBLANES, LANES), F32),
                        pltpu.VMEM((slabs, SUBLANES, LANES), F32),
                        pltpu.VMEM((slabs, bsz * tt, LANES), F32)],
        compiler_params=_params("parallel", "arbitrary"),
        name="rglru",
    )(proj.reshape(bsz, seq, n), proj.reshape(bsz, seq, n), wcat, conv_w, conv_b.reshape(1, -1),
      ba.reshape(1, -1), bx.reshape(1, -1), lam.reshape(1, -1))
    return rec.reshape(bsz * seq, lru_w)


def _in_proj_kernel(x_ref, sh_ref, sc_ref, g_ref, w_ref, o_ref, h_ref, *, sq_relu):
    j = pl.program_id(1)
    rc = x_ref.shape[0] // IN_PROJ_ROW_CHUNKS

    def project(h):
        y = jnp.dot(h, w_ref[...], preferred_element_type=F32)
        if sq_relu:
            y = jnp.square(jnp.maximum(y, 0.0))
        return y.astype(o_ref.dtype)

    @pl.when(j == 0)
    def _():
        for r in range(IN_PROJ_ROW_CHUNKS):
            rows = pl.ds(r * rc, rc)
            h = _rms_mod(x_ref[rows, :], g_ref[...], sh_ref[...], sc_ref[...]).astype(BF16)
            h_ref[rows, :] = h
            o_ref[rows, :] = project(h)

    @pl.when(j > 0)
    def _():
        o_ref[...] = project(h_ref[...])


def _in_proj(x2d, shift, scale, g, w, seq, tn_target, name, sq_relu=False):
    m, d = x2d.shape
    n = w.shape[1]
    tm = _tile(seq, IN_PROJ_ROWS, SUBLANES * IN_PROJ_ROW_CHUNKS)
    tn = _tile(n, tn_target, LANES)
    per_b = seq // tm
    row = lambda i, j: (i // per_b, 0, 0)
    return pl.pallas_call(
        functools.partial(_in_proj_kernel, sq_relu=sq_relu),
        grid=(m // tm, n // tn),
        in_specs=[pl.BlockSpec((tm, d), lambda i, j: (i, 0)),
                  pl.BlockSpec((None, 1, d), row),
                  pl.BlockSpec((None, 1, d), row),
                  pl.BlockSpec((1, d), lambda i, j: (0, 0)),
                  pl.BlockSpec((d, tn), lambda i, j: (0, j))],
        out_specs=pl.BlockSpec((tm, tn), lambda i, j: (i, j)),
        out_shape=jax.ShapeDtypeStruct((m, n), BF16),
        scratch_shapes=[pltpu.VMEM((tm, d), BF16)],
        compiler_params=_params("parallel", "arbitrary"),
        name=name,
    )(x2d, shift, scale, g, w)


def _t5_bucket(rel, n_buckets):
    max_exact = n_buckets // 2
    relf = jnp.maximum(rel, 1).astype(F32)
    large = max_exact + (jnp.log(relf / max_exact) / math.log(MAX_DISTANCE / max_exact)
                         * (n_buckets - max_exact)).astype(jnp.int32)
    large = jnp.minimum(large, n_buckets - 1)
    return jnp.where(rel < max_exact, rel, large)


def _band_bias_kernel(rb_ref, sink_ref, bucket_ref, o_ref, *, pairs, n_buckets):
    kv = pl.program_id(0)
    blk = ATTN_BLOCK
    bucket = bucket_ref[...]
    qi = lax.broadcasted_iota(jnp.int32, (blk, 2 * blk), 0)
    ki = lax.broadcasted_iota(jnp.int32, (blk, 2 * blk), 1)
    rel = qi + blk - ki
    valid = (rel >= 0) & (rel < blk)
    for p in range(pairs):
        for par in range(2):
            h = (kv * pairs + p) * 2 + par
            bias = jnp.zeros((blk, 2 * blk), F32)
            for b in range(n_buckets):
                bias = jnp.where(bucket == b, rb_ref[b, h], bias)
            sink = sink_ref[h]
            rows, cols = slice(p * blk, (p + 1) * blk), slice(par * 2 * blk, (par + 1) * 2 * blk)
            o_ref[rows, cols] = jnp.where(ki == 0, sink, jnp.where(valid, bias, NEG_INF))


def _band_bias(rel_bias, sinks, n_kv):
    n_buckets, n_heads = rel_bias.shape
    pairs = n_heads // n_kv // 2
    blk = ATTN_BLOCK
    qi = jnp.arange(blk)[:, None]
    ki = jnp.arange(2 * blk)[None, :]
    bucket = _t5_bucket(jnp.maximum(qi + blk - ki, 0), n_buckets)
    smem = pl.BlockSpec(memory_space=pltpu.SMEM)
    return pl.pallas_call(
        functools.partial(_band_bias_kernel, pairs=pairs, n_buckets=n_buckets),
        grid=(n_kv,),
        in_specs=[smem, smem, pl.BlockSpec((blk, 2 * blk), lambda kv: (0, 0))],
        out_specs=pl.BlockSpec((None, pairs * blk, 4 * blk), lambda kv: (kv, 0, 0)),
        out_shape=jax.ShapeDtypeStruct((n_kv, pairs * blk, 4 * blk), F32),
        compiler_params=_params("parallel"),
        name="band_bias",
    )(rel_bias, sinks, bucket)


def _swap_lane_halves(v):
    u = pltpu.bitcast(v, jnp.uint32)
    return pltpu.bitcast(pltpu.roll(u, LANES // 2, axis=1), BF16)


def _mix_kernel(*refs, n_kv, pairs, scale, per_b):
    q_ref, kp_ref, kc_ref, vp_ref, vc_ref, bm_ref, rec_ref = refs[:7]
    ga_refs, gb_refs = refs[7:7 + n_kv], refs[7 + n_kv:7 + 2 * n_kv]
    x_ref, gate_ref, wl_ref, wa_ref, wo_ref, o_ref, att_ref, ya_ref, m_ref = refs[7 + 2 * n_kv:]
    blk = ATTN_BLOCK
    d = o_ref.shape[1]
    cc = d // n_kv
    fold = math.log2(scale).is_integer()
    lane = lax.broadcasted_iota(jnp.int32, (blk, LANES), 1)
    key = lax.broadcasted_iota(jnp.int32, (blk, LANES), 0)
    low = jnp.where(lane < LANES // 2, 1.0, 0.0).astype(BF16)
    high = jnp.where(lane < LANES // 2, 0.0, 1.0).astype(BF16)
    not_key0 = jnp.where(key == 0, 0.0, 1.0).astype(BF16)
    ones_blk = jnp.concatenate([low, low, high, high], axis=0)
    key_col = lax.broadcasted_iota(jnp.int32, (1, 4 * blk), 1) % (2 * blk)
    seq_start = pl.program_id(0) % per_b == 0
    start_mask = jnp.where(seq_start & (key_col >= 1) & (key_col < blk), NEG_INF, 0.0)

    def halves(t, kv):
        s = _swap_lane_halves(t)
        lo, hi = (t, s) if kv % 2 == 0 else (s, t)
        return lo * low, hi * high

    def window(prev, cur):
        return jnp.concatenate([prev[0] * not_key0, cur[0], prev[1] * not_key0, cur[1]], axis=0)

    for kv in range(n_kv):
        sl = slice((kv // 2) * LANES, (kv // 2 + 1) * LANES)
        kb = [halves(kp_ref[:, sl], kv)] + [halves(kc_ref[t * blk:(t + 1) * blk, sl], kv)
                                            for t in range(QUERY_BLOCKS)]
        vb = [halves(vp_ref[:, sl], kv)] + [halves(vc_ref[t * blk:(t + 1) * blk, sl], kv)
                                            for t in range(QUERY_BLOCKS)]
        for t in range(QUERY_BLOCKS):
            rows = slice(t * blk, (t + 1) * blk)
            kblk = window(kb[t], kb[t + 1])
            vaug = jnp.concatenate([window(vb[t], vb[t + 1]), ones_blk], axis=1)
            q = jnp.concatenate(
                [q_ref[rows, (kv * pairs + p) * LANES:(kv * pairs + p + 1) * LANES]
                 for p in range(pairs)], axis=0)
            if fold:
                q = q * scale
            s = lax.dot_general(q, kblk, (((1,), (1,)), ((), ())), preferred_element_type=F32)
            if not fold:
                s = s * scale
            s = s + bm_ref[kv]
            if t == 0:
                s = s + start_mask
            es = []
            for par in range(2):
                sp = s[:, par * 2 * blk:(par + 1) * 2 * blk]
                es.append(jnp.exp(sp - jnp.max(sp, axis=-1, keepdims=True)).astype(BF16))
            o = jnp.dot(jnp.concatenate(es, axis=1), vaug, preferred_element_type=F32)
            res = (o[:, :LANES] / o[:, LANES:]).astype(BF16)
            for p in range(pairs):
                att_ref[rows, (kv * pairs + p) * LANES:(kv * pairs + p + 1) * LANES] = (
                    res[p * blk:(p + 1) * blk])
        cols = slice(kv * cc, (kv + 1) * cc)
        ya = jnp.dot(rec_ref[...], wl_ref[:, cols], preferred_element_type=F32)
        ya_ref[:, cols] = _sigmoid(ga_refs[kv][...].astype(F32)) * ya

    for c in range(n_kv):
        cols = slice(c * cc, (c + 1) * cc)
        yb = jnp.dot(att_ref[...], wa_ref[:, cols], preferred_element_type=F32)
        gb = _sigmoid(gb_refs[c][...].astype(F32))
        m_ref[:, cols] = (ya_ref[:, cols] + gb * yb).astype(BF16)
    for c in range(n_kv):
        cols = slice(c * cc, (c + 1) * cc)
        y = jnp.dot(m_ref[...], wo_ref[:, cols], preferred_element_type=F32)
        o_ref[:, cols] = x_ref[:, cols] + gate_ref[:, cols] * y


def _mix(rest, rec, x2d, gate, band_bias, w_lru, w_att, w_out, seq, q_off, k_off, v_off, ga_off,
         gb_off, kv_w, n_heads):
    m, d = x2d.shape
    attn_w = w_att.shape[0]
    hd = attn_w // n_heads
    n_kv = kv_w // hd
    pairs = n_heads // n_kv // 2
    blk = ATTN_BLOCK
    tm = QUERY_BLOCKS * blk
    per_b = seq // tm
    cc = d // n_kv
    assert 2 * hd == LANES and n_heads == 2 * pairs * n_kv and kv_w % LANES == 0
    assert q_off % attn_w == 0 and k_off % kv_w == 0 and v_off % kv_w == 0
    assert seq % tm == 0 and cc % LANES == 0 and ga_off % cc == 0 and gb_off % cc == 0
    kb, vb = k_off // kv_w, v_off // kv_w
    prev = lambda col: (lambda i: (jnp.maximum(QUERY_BLOCKS * i - 1, 0), col))
    col_chunk = lambda off: [pl.BlockSpec((tm, cc), functools.partial(lambda c, i: (i, c),
                                                                      off // cc + c))
                             for c in range(n_kv)]
    resident = lambda a: pl.BlockSpec(a.shape, lambda i: (0,) * a.ndim,
                                      pipeline_mode=pl.Buffered(1))
    kernel = functools.partial(_mix_kernel, n_kv=n_kv, pairs=pairs, scale=hd ** -0.5, per_b=per_b)
    return pl.pallas_call(
        kernel,
        grid=(m // tm,),
        in_specs=[pl.BlockSpec((tm, attn_w), lambda i: (i, q_off // attn_w)),
                  pl.BlockSpec((blk, kv_w), prev(kb)),
                  pl.BlockSpec((tm, kv_w), lambda i: (i, kb)),
                  pl.BlockSpec((blk, kv_w), prev(vb)),
                  pl.BlockSpec((tm, kv_w), lambda i: (i, vb)),
                  resident(band_bias),
                  pl.BlockSpec((tm, rec.shape[1]), lambda i: (i, 0))]
                 + col_chunk(ga_off) + col_chunk(gb_off)
                 + [pl.BlockSpec((tm, d), lambda i: (i, 0)),
                    pl.BlockSpec((None, 1, d), lambda i: (i // per_b, 0, 0)),
                    resident(w_lru), resident(w_att), resident(w_out)],
        out_specs=pl.BlockSpec((tm, d), lambda i: (i, 0)),
        out_shape=jax.ShapeDtypeStruct((m, d), F32),
        scratch_shapes=[pltpu.VMEM((tm, attn_w), BF16), pltpu.VMEM((tm, d), F32),
                        pltpu.VMEM((tm, d), BF16)],
        compiler_params=_params("parallel"),
        name="mix",
    )(rest, rest, rest, rest, rest, band_bias, rec, *([rest] * (2 * n_kv)), x2d, gate,
      w_lru, w_att, w_out)


def _mlp_kernel(x_ref, sh_ref, sc_ref, gate_ref, g2_ref, gf_ref, w1_ref, w2_ref, o_ref,
                h_ref, ff_ref, *, nf):
    f = pl.program_id(1)
    rc = x_ref.shape[0] // ROW_CHUNKS

    def up(rows, slot):
        ff = jnp.dot(h_ref[rows, :], w1_ref[...], preferred_element_type=F32)
        ff_ref[slot, rows, :] = jnp.square(jnp.maximum(ff, 0.0)).astype(BF16)

    def down(rows, slot):
        return jnp.dot(ff_ref[slot, rows, :], w2_ref[...], preferred_element_type=F32)

    @pl.when(f == 0)
    def _():
        o_ref[...] = jnp.zeros_like(o_ref)
        for r in range(ROW_CHUNKS):
            rows = pl.ds(r * rc, rc)
            h_ref[rows, :] = _rms_mod(x_ref[rows, :], g2_ref[...], sh_ref[...],
                                      sc_ref[...]).astype(BF16)
            up(rows, 0)

    @pl.when((f > 0) & (f < nf))
    def _():
        o_ref[...] += down(slice(None), (f - 1) % 2)
        up(slice(None), f % 2)

    @pl.when(f == nf)
    def _():
        for r in range(ROW_CHUNKS):
            rows = pl.ds(r * rc, rc)
            x2 = x_ref[rows, :] + gate_ref[...] * (o_ref[rows, :] + down(rows, (nf - 1) % 2))
            var = jnp.mean(x2 * x2, axis=-1, keepdims=True)
            o_ref[rows, :] = x2 * lax.rsqrt(var + EPS) * gf_ref[...]


def _mlp(x1, shift, scale, gate, g2, gf, w1, w2, seq):
    m, d = x1.shape
    dff = w1.shape[1]
    tm = _tile(seq, MLP_ROWS, SUBLANES * ROW_CHUNKS)
    tf = _tile(dff, MLP_FF_CHUNK, LANES)
    nf = dff // tf
    per_b = seq // tm
    row = lambda i, f: (i // per_b, 0, 0)
    const = lambda i, f: (0, 0)
    return pl.pallas_call(
        functools.partial(_mlp_kernel, nf=nf),
        grid=(m // tm, nf + 1),
        in_specs=[pl.BlockSpec((tm, d), lambda i, f: (i, 0)),
                  pl.BlockSpec((None, 1, d), row),
                  pl.BlockSpec((None, 1, d), row),
                  pl.BlockSpec((None, 1, d), row),
                  pl.BlockSpec((1, d), const),
                  pl.BlockSpec((1, d), const),
                  pl.BlockSpec((d, tf), lambda i, f: (0, jnp.minimum(f, nf - 1))),
                  pl.BlockSpec((tf, d), lambda i, f: (jnp.maximum(f - 1, 0), 0))],
        out_specs=pl.BlockSpec((tm, d), lambda i, f: (i, 0)),
        out_shape=jax.ShapeDtypeStruct((m, d), F32),
        scratch_shapes=[pltpu.VMEM((tm, d), BF16), pltpu.VMEM((2, tm, tf), BF16)],
        compiler_params=_params("parallel", "arbitrary"),
        name="mlp",
    )(x1, shift, scale, gate, g2, gf, w1, w2)


def _mlp_down_kernel(ff_ref, w2_ref, x_ref, gate_ref, gf_ref, o_ref):
    k = pl.program_id(1)
    last = pl.num_programs(1) - 1
    tm, d = o_ref.shape
    cc = max(d // COL_CHUNKS, LANES)
    rc = tm // ROW_CHUNKS

    @pl.when(k == 0)
    def _():
        for c in range(0, d, cc):
            o_ref[:, c:c + cc] = jnp.dot(ff_ref[...], w2_ref[:, c:c + cc],
                                         preferred_element_type=F32)

    @pl.when((k > 0) & (k < last))
    def _():
        for c in range(0, d, cc):
            o_ref[:, c:c + cc] += jnp.dot(ff_ref[...], w2_ref[:, c:c + cc],
                                          preferred_element_type=F32)

    @pl.when(k == last)
    def _():
        for r in range(ROW_CHUNKS):
            rows = pl.ds(r * rc, rc)
            y = o_ref[rows, :] + jnp.dot(ff_ref[rows, :], w2_ref[...], preferred_element_type=F32)
            x2 = x_ref[rows, :] + gate_ref[...] * y
            var = jnp.mean(x2 * x2, axis=-1, keepdims=True)
            o_ref[rows, :] = x2 * lax.rsqrt(var + EPS) * gf_ref[...]


def _mlp_down(ff, w2, x1, gate, gf, seq):
    m, dff = ff.shape
    d = w2.shape[1]
    tm = _tile(seq, MLP_ROWS, SUBLANES * ROW_CHUNKS)
    tk = _tile(dff // 2, MLP_DOWN_K, LANES)
    per_b = seq // tm
    return pl.pallas_call(
        _mlp_down_kernel,
        grid=(m // tm, dff // tk),
        in_specs=[pl.BlockSpec((tm, tk), lambda i, k: (i, k)),
                  pl.BlockSpec((tk, d), lambda i, k: (k, 0)),
                  pl.BlockSpec((tm, d), lambda i, k: (i, 0)),
                  pl.BlockSpec((None, 1, d), lambda i, k: (i // per_b, 0, 0)),
                  pl.BlockSpec((1, d), lambda i, k: (0, 0))],
        out_specs=pl.BlockSpec((tm, d), lambda i, k: (i, 0)),
        out_shape=jax.ShapeDtypeStruct((m, d), F32),
        compiler_params=_params("parallel", "arbitrary"),
        name="mlp_down",
    )(ff, w2, x1, gate, gf)


def kernel(x, c, w_ada, b_ada, norm1_g, w_in, conv_w, conv_b, lru_wa, lru_ba, lru_wx, lru_bx,
           lru_lambda, w_lru_out, w_attn_out, attn_sinks, rel_bias, w_out, norm2_g, w_ff1, w_ff2,
           final_g):
    bsz, seq, d = x.shape
    depth = w_in.shape[0]
    lru_w = w_lru_out.shape[1]
    attn_w = w_attn_out.shape[1]
    kv_w = (w_in.shape[2] - 2 * lru_w - attn_w - 2 * d) // 2
    n_heads = attn_sinks.shape[1]
    n_kv = kv_w // (attn_w // n_heads)
    q_off = 2 * lru_w
    k_off = q_off + attn_w
    v_off = k_off + kv_w
    ga_off = v_off + kv_w
    gb_off = ga_off + d

    xs = x.reshape(bsz * seq, d)
    for l in range(depth):
        band_bias = _band_bias(rel_bias, attn_sinks[l], n_kv)
        mod = _adaln(c, w_ada[l], b_ada[l])
        shift1, scale1, gate1, shift2, scale2, gate2 = [
            t.reshape(bsz, 1, d) for t in jnp.split(mod, 6, axis=-1)]

        rest = _in_proj(xs, shift1, scale1, norm1_g[l].reshape(1, d), w_in[l].astype(BF16), seq,
                        IN_PROJ_COLS, "in_proj")
        wcat = jnp.concatenate([lru_wa[l], lru_wx[l]], axis=-1).astype(BF16)
        rec = _rglru(rest, wcat, conv_w[l], conv_b[l], lru_ba[l], lru_bx[l], lru_lambda[l],
                     bsz, seq, lru_w)
        x1 = _mix(rest, rec, xs, gate1, band_bias, w_lru_out[l].astype(BF16),
                  w_attn_out[l].astype(BF16), w_out[l].astype(BF16), seq, q_off, k_off, v_off,
                  ga_off, gb_off, kv_w, n_heads)
        if l != depth - 1:
            raise NotImplementedError("the fused final norm assumes a single layer")
        ff = _in_proj(x1, shift2, scale2, norm2_g[l].reshape(1, d), w_ff1[l].astype(BF16), seq,
                      MLP_UP_COLS, "mlp_up", sq_relu=True)
        xs = _mlp_down(ff, w_ff2[l].astype(BF16), x1, gate2, final_g.reshape(1, d), seq)
    return xs.reshape(bsz, seq, d)
```

```python
import functools
import math

import jax
import jax.numpy as jnp
from jax import lax
from jax.experimental import pallas as pl
from jax.experimental.pallas import tpu as pltpu

EPS = 1e-6
LRU_C = 8.0
LOG2E = 1.4426950408889634
ATTN_BLOCK = 128
NEG_INF = -1e30
MAX_DISTANCE = 128
LANES = 128
SUBLANES = 8
VMEM_LIMIT = 56 * 1024 * 1024
IN_PROJ_ROWS, IN_PROJ_COLS = 1024, 1792
RGLRU_STEPS, RGLRU_CHANNELS = 256, 512
MLP_UP_COLS = 2048
MLP_ROWS, MLP_DOWN_K = 512, 2048
COL_CHUNKS = 4
ADALN_COLS = 1024
ROW_CHUNKS = 2
IN_PROJ_ROW_CHUNKS = 4
QUERY_BLOCKS = 2

F32 = jnp.float32
BF16 = jnp.bfloat16


def _tile(n, target, align):
    best = None
    t = align
    while t <= min(n, target):
        if n % t == 0:
            best = t
        t += align
    if best is None:
        raise ValueError(f"no tile for {n} (target {target}, align {align})")
    return best


def _params(*sem):
    return pltpu.CompilerParams(dimension_semantics=sem, vmem_limit_bytes=VMEM_LIMIT)


def _sigmoid(v):
    return 1.0 / (1.0 + jnp.exp2(v * -LOG2E))


def _rms_mod(x, g, shift, scale):
    var = jnp.mean(x * x, axis=-1, keepdims=True)
    y = x * lax.rsqrt(var + EPS) * g
    return y * (1.0 + scale) + shift


def _adaln_kernel(c_ref, w_ref, b_ref, o_ref):
    c = c_ref[...]
    act = (c * _sigmoid(c)).astype(BF16)
    o_ref[...] = jnp.dot(act, w_ref[...].astype(BF16), preferred_element_type=F32) + b_ref[...]


def _adaln(c, w, b):
    bsz, d = c.shape
    n = w.shape[1]
    tn = _tile(n, ADALN_COLS, LANES)
    return pl.pallas_call(
        _adaln_kernel,
        grid=(n // tn,),
        in_specs=[pl.BlockSpec((bsz, d), lambda j: (0, 0)),
                  pl.BlockSpec((d, tn), lambda j: (0, j)),
                  pl.BlockSpec((1, tn), lambda j: (0, j))],
        out_specs=pl.BlockSpec((bsz, tn), lambda j: (0, j)),
        out_shape=jax.ShapeDtypeStruct((bsz, n), F32),
        compiler_params=_params("parallel"),
        name="adaln_mod",
    )(c, w, b.reshape(1, n))


def _gelu_tanh(v):
    k1 = -2.0 * math.sqrt(2.0 / math.pi) * LOG2E
    return v / (1.0 + jnp.exp2(v * (k1 + (k1 * 0.044715) * (v * v))))


def _rglru_kernel(x_ref, gate_ref, w_ref, cw_ref, cb_ref, ba_ref, bx_ref, lam_ref, o_ref,
                  lx_ref, xtail, hcar, hm_ref, *, conv_k):
    bsz, tt, cw = x_ref.shape
    slabs = cw // LANES
    seq_start = pl.program_id(1) == 0

    @pl.when(seq_start)
    def _():
        xtail[...] = jnp.zeros_like(xtail)
        hcar[...] = jnp.zeros_like(hcar)

    for b in range(bsz):
        xb = x_ref[b].astype(F32)
        for s in range(slabs):
            lx_ref[s, pl.ds(b, tt, stride=SUBLANES), :] = xb[:, s * LANES:(s + 1) * LANES]

    lam = lam_ref[...]
    softplus_neg_lam = jnp.maximum(-lam, 0.0) + jnp.log(1.0 + jnp.exp(-jnp.abs(lam)))
    log2_a_coef = (-LRU_C * LOG2E) * softplus_neg_lam

    def conv_and_gates(s):
        sl = slice(s * LANES, (s + 1) * LANES)
        x3 = lx_ref[s].reshape(tt, SUBLANES, LANES)
        xs = jnp.concatenate([xtail[s], x3], axis=0)
        xtail[s] = xs[tt:]
        xc = cb_ref[:, sl] + cw_ref[conv_k - 1:conv_k, sl] * x3
        for k in range(conv_k - 1):
            xc = xc + cw_ref[k:k + 1, sl] * xs[k:k + tt]
        z = jnp.dot(xc.reshape(tt * SUBLANES, LANES).astype(BF16), w_ref[s],
                    preferred_element_type=F32)
        return xc, z

    ahead = conv_and_gates(0)
    for s in range(slabs):
        sl = slice(s * LANES, (s + 1) * LANES)
        xc, z = ahead
        if s + 1 < slabs:
            ahead = conv_and_gates(s + 1)
        r = _sigmoid(z[:, :LANES] + ba_ref[:, sl]).reshape(tt, SUBLANES, LANES)
        i = _sigmoid(z[:, LANES:] + bx_ref[:, sl]).reshape(tt, SUBLANES, LANES)
        a = jnp.exp2(log2_a_coef[:, sl] * r)
        om = 1.0 - a * a
        mult = jnp.where(om > 0.0, om * lax.rsqrt(om), 0.0)
        mult = jnp.concatenate([jnp.where(seq_start, 1.0, mult[:1]), mult[1:]], axis=0)
        u = mult * (i * xc)
        h = hcar[s]
        for t in range(tt):
            h = a[t] * h + u[t]
            hm_ref[s, t * SUBLANES:(t + 1) * SUBLANES, :] = h
        hcar[s] = h
        for b in range(bsz):
            hb = hm_ref[s, pl.ds(b, tt, stride=SUBLANES), :]
            gate = gate_ref[b, :, sl].astype(F32)
            o_ref[b, :, sl] = (hb * _gelu_tanh(gate)).astype(o_ref.dtype)


def _rglru(proj, wcat, conv_w, conv_b, ba, bx, lam, bsz, seq, lru_w):
    assert bsz == SUBLANES
    n = proj.shape[1]
    conv_k = conv_w.shape[0]
    tt = _tile(seq, RGLRU_STEPS, SUBLANES)
    cw = _tile(lru_w, RGLRU_CHANNELS, LANES)
    slabs = cw // LANES
    vec = pl.BlockSpec((1, cw), lambda c, t: (0, c))
    rec = pl.pallas_call(
        functools.partial(_rglru_kernel, conv_k=conv_k),
        grid=(lru_w // cw, seq // tt),
        in_specs=[pl.BlockSpec((bsz, tt, cw), lambda c, t: (0, t, c)),
                  pl.BlockSpec((bsz, tt, cw), lambda c, t: (0, t, lru_w // cw + c)),
                  pl.BlockSpec((slabs, LANES, 2 * LANES), lambda c, t: (c, 0, 0)),
                  pl.BlockSpec((conv_k, cw), lambda c, t: (0, c)),
                  vec, vec, vec, vec],
        out_specs=pl.BlockSpec((bsz, tt, cw), lambda c, t: (0, t, c)),
        out_shape=jax.ShapeDtypeStruct((bsz, seq, lru_w), BF16),
        scratch_shapes=[pltpu.VMEM((slabs, bsz * tt, LANES), F32),
                        pltpu.VMEM((slabs, conv_k - 1, SUBLANES, LANES), F32),
                        pltpu.VMEM((slabs, SUBLANES, LANES), F32),
                        pltpu.VMEM((slabs, bsz * tt, LANES), F32)],
        compiler_params=_params("parallel", "arbitrary"),
        name="rglru",
    )(proj.reshape(bsz, seq, n), proj.reshape(bsz, seq, n), wcat, conv_w, conv_b.reshape(1, -1),
      ba.reshape(1, -1), bx.reshape(1, -1), lam.reshape(1, -1))
    return rec.reshape(bsz * seq, lru_w)


def _in_proj_kernel(x_ref, sh_ref, sc_ref, g_ref, w_ref, o_ref, h_ref, *, sq_relu):
    j = pl.program_id(1)
    rc = x_ref.shape[0] // IN_PROJ_ROW_CHUNKS

    def project(h):
        y = jnp.dot(h, w_ref[...], preferred_element_type=F32)
        if sq_relu:
            y = jnp.square(jnp.maximum(y, 0.0))
        return y.astype(o_ref.dtype)

    @pl.when(j == 0)
    def _():
        for r in range(IN_PROJ_ROW_CHUNKS):
            rows = pl.ds(r * rc, rc)
            h = _rms_mod(x_ref[rows, :], g_ref[...], sh_ref[...], sc_ref[...]).astype(BF16)
            h_ref[rows, :] = h
            o_ref[rows, :] = project(h)

    @pl.when(j > 0)
    def _():
        o_ref[...] = project(h_ref[...])


def _in_proj(x2d, shift, scale, g, w, seq, tn_target, name, sq_relu=False):
    m, d = x2d.shape
    n = w.shape[1]
    tm = _tile(seq, IN_PROJ_ROWS, SUBLANES * IN_PROJ_ROW_CHUNKS)
    tn = _tile(n, tn_target, LANES)
    per_b = seq // tm
    row = lambda i, j: (i // per_b, 0, 0)
    return pl.pallas_call(
        functools.partial(_in_proj_kernel, sq_relu=sq_relu),
        grid=(m // tm, n // tn),
        in_specs=[pl.BlockSpec((tm, d), lambda i, j: (i, 0)),
                  pl.BlockSpec((None, 1, d), row),
                  pl.BlockSpec((None, 1, d), row),
                  pl.BlockSpec((1, d), lambda i, j: (0, 0)),
                  pl.BlockSpec((d, tn), lambda i, j: (0, j))],
        out_specs=pl.BlockSpec((tm, tn), lambda i, j: (i, j)),
        out_shape=jax.ShapeDtypeStruct((m, n), BF16),
        scratch_shapes=[pltpu.VMEM((tm, d), BF16)],
        compiler_params=_params("parallel", "arbitrary"),
        name=name,
    )(x2d, shift, scale, g, w)


def _t5_bucket(rel, n_buckets):
    max_exact = n_buckets // 2
    relf = jnp.maximum(rel, 1).astype(F32)
    large = max_exact + (jnp.log(relf / max_exact) / math.log(MAX_DISTANCE / max_exact)
                         * (n_buckets - max_exact)).astype(jnp.int32)
    large = jnp.minimum(large, n_buckets - 1)
    return jnp.where(rel < max_exact, rel, large)


def _band_bias_kernel(rb_ref, sink_ref, bucket_ref, o_ref, *, pairs, n_buckets):
    kv = pl.program_id(0)
    blk = ATTN_BLOCK
    bucket = bucket_ref[...]
    qi = lax.broadcasted_iota(jnp.int32, (blk, 2 * blk), 0)
    ki = lax.broadcasted_iota(jnp.int32, (blk, 2 * blk), 1)
    rel = qi + blk - ki
    valid = (rel >= 0) & (rel < blk)
    for p in range(pairs):
        for par in range(2):
            h = (kv * pairs + p) * 2 + par
            bias = jnp.zeros((blk, 2 * blk), F32)
            for b in range(n_buckets):
                bias = jnp.where(bucket == b, rb_ref[b, h], bias)
            sink = sink_ref[h]
            rows, cols = slice(p * blk, (p + 1) * blk), slice(par * 2 * blk, (par + 1) * 2 * blk)
            o_ref[rows, cols] = jnp.where(ki == 0, sink, jnp.where(valid, bias, NEG_INF))


def _band_bias(rel_bias, sinks, n_kv):
    n_buckets, n_heads = rel_bias.shape
    pairs = n_heads // n_kv // 2
    blk = ATTN_BLOCK
    qi = jnp.arange(blk)[:, None]
    ki = jnp.arange(2 * blk)[None, :]
    bucket = _t5_bucket(jnp.maximum(qi + blk - ki, 0), n_buckets)
    smem = pl.BlockSpec(memory_space=pltpu.SMEM)
    return pl.pallas_call(
        functools.partial(_band_bias_kernel, pairs=pairs, n_buckets=n_buckets),
        grid=(n_kv,),
        in_specs=[smem, smem, pl.BlockSpec((blk, 2 * blk), lambda kv: (0, 0))],
        out_specs=pl.BlockSpec((None, pairs * blk, 4 * blk), lambda kv: (kv, 0, 0)),
        out_shape=jax.ShapeDtypeStruct((n_kv, pairs * blk, 4 * blk), F32),
        compiler_params=_params("parallel"),
        name="band_bias",
    )(rel_bias, sinks, bucket)


def _swap_lane_halves(v):
    u = pltpu.bitcast(v, jnp.uint32)
    return pltpu.bitcast(pltpu.roll(u, LANES // 2, axis=1), BF16)


def _mix_kernel(*refs, n_kv, pairs, scale, per_b):
    q_ref, kp_ref, kc_ref, vp_ref, vc_ref, bm_ref, rec_ref = refs[:7]
    ga_refs, gb_refs = refs[7:7 + n_kv], refs[7 + n_kv:7 + 2 * n_kv]
    x_ref, gate_ref, wl_ref, wa_ref, wo_ref, o_ref, att_ref, ya_ref, m_ref = refs[7 + 2 * n_kv:]
    blk = ATTN_BLOCK
    d = o_ref.shape[1]
    cc = d // n_kv
    fold = math.log2(scale).is_integer()
    lane = lax.broadcasted_iota(jnp.int32, (blk, LANES), 1)
    key = lax.broadcasted_iota(jnp.int32, (blk, LANES), 0)
    low = jnp.where(lane < LANES // 2, 1.0, 0.0).astype(BF16)
    high = jnp.where(lane < LANES // 2, 0.0, 1.0).astype(BF16)
    not_key0 = jnp.where(key == 0, 0.0, 1.0).astype(BF16)
    ones_blk = jnp.concatenate([low, low, high, high], axis=0)
    key_col = lax.broadcasted_iota(jnp.int32, (1, 4 * blk), 1) % (2 * blk)
    seq_start = pl.program_id(0) % per_b == 0
    start_mask = jnp.where(seq_start & (key_col >= 1) & (key_col < blk), NEG_INF, 0.0)

    def halves(t, kv):
        s = _swap_lane_halves(t)
        lo, hi = (t, s) if kv % 2 == 0 else (s, t)
        return lo * low, hi * high

    def window(prev, cur):
        return jnp.concatenate([prev[0] * not_key0, cur[0], prev[1] * not_key0, cur[1]], axis=0)

    for kv in range(n_kv):
        sl = slice((kv // 2) * LANES, (kv // 2 + 1) * LANES)
        kb = [halves(kp_ref[:, sl], kv)] + [halves(kc_ref[t * blk:(t + 1) * blk, sl], kv)
                                            for t in range(QUERY_BLOCKS)]
        vb = [halves(vp_ref[:, sl], kv)] + [halves(vc_ref[t * blk:(t + 1) * blk, sl], kv)
                                            for t in range(QUERY_BLOCKS)]
        for t in range(QUERY_BLOCKS):
            rows = slice(t * blk, (t + 1) * blk)
            kblk = window(kb[t], kb[t + 1])
            vaug = jnp.concatenate([window(vb[t], vb[t + 1]), ones_blk], axis=1)
            q = jnp.concatenate(
                [q_ref[rows, (kv * pairs + p) * LANES:(kv * pairs + p + 1) * LANES]
                 for p in range(pairs)], axis=0)
            if fold:
                q = q * scale
            s = lax.dot_general(q, kblk, (((1,), (1,)), ((), ())), preferred_element_type=F32)
            if not fold:
                s = s * scale
            s = s + bm_ref[kv]
            if t == 0:
                s = s + start_mask
            es = []
            for par in range(2):
                sp = s[:, par * 2 * blk:(par + 1) * 2 * blk]
                es.append(jnp.exp(sp - jnp.max(sp, axis=-1, keepdims=True)).astype(BF16))
            o = jnp.dot(jnp.concatenate(es, axis=1), vaug, preferred_element_type=F32)
            res = (o[:, :LANES] / o[:, LANES:]).astype(BF16)
            for p in range(pairs):
                att_ref[rows, (kv * pairs + p) * LANES:(kv * pairs + p + 1) * LANES] = (
                    res[p * blk:(p + 1) * blk])
        cols = slice(kv * cc, (kv + 1) * cc)
        ya = jnp.dot(rec_ref[...], wl_ref[:, cols], preferred_element_type=F32)
        ya_ref[:, cols] = _sigmoid(ga_refs[kv][...].astype(F32)) * ya

    for c in range(n_kv):
        cols = slice(c * cc, (c + 1) * cc)
        yb = jnp.dot(att_ref[...], wa_ref[:, cols], preferred_element_type=F32)
        gb = _sigmoid(gb_refs[c][...].astype(F32))
        m_ref[:, cols] = (ya_ref[:, cols] + gb * yb).astype(BF16)
    for c in range(n_kv):
        cols = slice(c * cc, (c + 1) * cc)
        y = jnp.dot(m_ref[...], wo_ref[:, cols], preferred_element_type=F32)
        o_ref[:, cols] = x_ref[:, cols] + gate_ref[:, cols] * y


def _mix(rest, rec, x2d, gate, band_bias, w_lru, w_att, w_out, seq, q_off, k_off, v_off, ga_off,
         gb_off, kv_w, n_heads):
    m, d = x2d.shape
    attn_w = w_att.shape[0]
    hd = attn_w // n_heads
    n_kv = kv_w // hd
    pairs = n_heads // n_kv // 2
    blk = ATTN_BLOCK
    tm = QUERY_BLOCKS * blk
    per_b = seq // tm
    cc = d // n_kv
    assert 2 * hd == LANES and n_heads == 2 * pairs * n_kv and kv_w % LANES == 0
    assert q_off % attn_w == 0 and k_off % kv_w == 0 and v_off % kv_w == 0
    assert seq % tm == 0 and cc % LANES == 0 and ga_off % cc == 0 and gb_off % cc == 0
    kb, vb = k_off // kv_w, v_off // kv_w
    prev = lambda col: (lambda i: (jnp.maximum(QUERY_BLOCKS * i - 1, 0), col))
    col_chunk = lambda off: [pl.BlockSpec((tm, cc), functools.partial(lambda c, i: (i, c),
                                                                      off // cc + c))
                             for c in range(n_kv)]
    resident = lambda a: pl.BlockSpec(a.shape, lambda i: (0,) * a.ndim,
                                      pipeline_mode=pl.Buffered(1))
    kernel = functools.partial(_mix_kernel, n_kv=n_kv, pairs=pairs, scale=hd ** -0.5, per_b=per_b)
    return pl.pallas_call(
        kernel,
        grid=(m // tm,),
        in_specs=[pl.BlockSpec((tm, attn_w), lambda i: (i, q_off // attn_w)),
                  pl.BlockSpec((blk, kv_w), prev(kb)),
                  pl.BlockSpec((tm, kv_w), lambda i: (i, kb)),
                  pl.BlockSpec((blk, kv_w), prev(vb)),
                  pl.BlockSpec((tm, kv_w), lambda i: (i, vb)),
                  resident(band_bias),
                  pl.BlockSpec((tm, rec.shape[1]), lambda i: (i, 0))]
                 + col_chunk(ga_off) + col_chunk(gb_off)
                 + [pl.BlockSpec((tm, d), lambda i: (i, 0)),
                    pl.BlockSpec((None, 1, d), lambda i: (i // per_b, 0, 0)),
                    resident(w_lru), resident(w_att), resident(w_out)],
        out_specs=pl.BlockSpec((tm, d), lambda i: (i, 0)),
        out_shape=jax.ShapeDtypeStruct((m, d), F32),
        scratch_shapes=[pltpu.VMEM((tm, attn_w), BF16), pltpu.VMEM((tm, d), F32),
                        pltpu.VMEM((tm, d), BF16)],
        compiler_params=_params("parallel"),
        name="mix",
    )(rest, rest, rest, rest, rest, band_bias, rec, *([rest] * (2 * n_kv)), x2d, gate,
      w_lru, w_att, w_out)


def _mlp_down_kernel(ff_ref, w2_ref, x_ref, gate_ref, gf_ref, o_ref):
    k = pl.program_id(1)
    last = pl.num_programs(1) - 1
    tm, d = o_ref.shape
    cc = max(d // COL_CHUNKS, LANES)
    rc = tm // ROW_CHUNKS

    @pl.when(k == 0)
    def _():
        for c in range(0, d, cc):
            o_ref[:, c:c + cc] = jnp.dot(ff_ref[...], w2_ref[:, c:c + cc],
                                         preferred_element_type=F32)

    @pl.when((k > 0) & (k < last))
    def _():
        for c in range(0, d, cc):
            o_ref[:, c:c + cc] += jnp.dot(ff_ref[...], w2_ref[:, c:c + cc],
                                          preferred_element_type=F32)

    @pl.when(k == last)
    def _():
        for r in range(ROW_CHUNKS):
            rows = pl.ds(r * rc, rc)
            y = o_ref[rows, :] + jnp.dot(ff_ref[rows, :], w2_ref[...], preferred_element_type=F32)
            x2 = x_ref[rows, :] + gate_ref[...] * y
            var = jnp.mean(x2 * x2, axis=-1, keepdims=True)
            o_ref[rows, :] = x2 * lax.rsqrt(var + EPS) * gf_ref[...]


def _mlp_down(ff, w2, x1, gate, gf, seq):
    m, dff = ff.shape
    d = w2.shape[1]
    tm = _tile(seq, MLP_ROWS, SUBLANES * ROW_CHUNKS)
    tk = _tile(dff // 2, MLP_DOWN_K, LANES)
    per_b = seq // tm
    return pl.pallas_call(
        _mlp_down_kernel,
        grid=(m // tm, dff // tk),
        in_specs=[pl.BlockSpec((tm, tk), lambda i, k: (i, k)),
                  pl.BlockSpec((tk, d), lambda i, k: (k, 0)),
                  pl.BlockSpec((tm, d), lambda i, k: (i, 0)),
                  pl.BlockSpec((None, 1, d), lambda i, k: (i // per_b, 0, 0)),
                  pl.BlockSpec((1, d), lambda i, k: (0, 0))],
        out_specs=pl.BlockSpec((tm, d), lambda i, k: (i, 0)),
        out_shape=jax.ShapeDtypeStruct((m, d), F32),
        compiler_params=_params("parallel", "arbitrary"),
        name="mlp_down",
    )(ff, w2, x1, gate, gf)


def kernel(x, c, w_ada, b_ada, norm1_g, w_in, conv_w, conv_b, lru_wa, lru_ba, lru_wx, lru_bx,
           lru_lambda, w_lru_out, w_attn_out, attn_sinks, rel_bias, w_out, norm2_g, w_ff1, w_ff2,
           final_g):
    bsz, seq, d = x.shape
    depth = w_in.shape[0]
    lru_w = w_lru_out.shape[1]
    attn_w = w_attn_out.shape[1]
    kv_w = (w_in.shape[2] - 2 * lru_w - attn_w - 2 * d) // 2
    n_heads = attn_sinks.shape[1]
    n_kv = kv_w // (attn_w // n_heads)
    q_off = 2 * lru_w
    k_off = q_off + attn_w
    v_off = k_off + kv_w
    ga_off = v_off + kv_w
    gb_off = ga_off + d

    xs = x.reshape(bsz * seq, d)
    for l in range(depth):
        band_bias = _band_bias(rel_bias, attn_sinks[l], n_kv)
        mod = _adaln(c, w_ada[l], b_ada[l])
        shift1, scale1, gate1, shift2, scale2, gate2 = [
            t.reshape(bsz, 1, d) for t in jnp.split(mod, 6, axis=-1)]

        rest = _in_proj(xs, shift1, scale1, norm1_g[l].reshape(1, d), w_in[l].astype(BF16), seq,
                        IN_PROJ_COLS, "in_proj")
        wcat = jnp.concatenate([lru_wa[l], lru_wx[l]], axis=-1).astype(BF16)
        rec = _rglru(rest, wcat, conv_w[l], conv_b[l], lru_ba[l], lru_bx[l], lru_lambda[l],
                     bsz, seq, lru_w)
        x1 = _mix(rest, rec, xs, gate1, band_bias, w_lru_out[l].astype(BF16),
                  w_attn_out[l].astype(BF16), w_out[l].astype(BF16), seq, q_off, k_off, v_off,
                  ga_off, gb_off, kv_w, n_heads)
        if l != depth - 1:
            raise NotImplementedError("the fused final norm assumes a single layer")
        ff = _in_proj(x1, shift2, scale2, norm2_g[l].reshape(1, d), w_ff1[l].astype(BF16), seq,
                      MLP_UP_COLS, "mlp_up", sq_relu=True)
        xs = _mlp_down(ff, w_ff2[l].astype(BF16), x1, gate2, final_g.reshape(1, d), seq)
    return xs.reshape(bsz, seq, d)
```

```python
import functools
import math

import jax
import jax.numpy as jnp
from jax import lax
from jax.experimental import pallas as pl
from jax.experimental.pallas import tpu as pltpu

EPS = 1e-6
LRU_C = 8.0
LOG2E = 1.4426950408889634
ATTN_BLOCK = 128
NEG_INF = -1e30
MAX_DISTANCE = 128
LANES = 128
SUBLANES = 8
VMEM_LIMIT = 56 * 1024 * 1024
IN_PROJ_ROWS, IN_PROJ_COLS = 1024, 1792
RGLRU_STEPS, RGLRU_CHANNELS = 256, 512
MLP_UP_COLS = 2048
MLP_ROWS, MLP_DOWN_K = 1024, 1024
COL_CHUNKS = 4
ADALN_COLS = 1024
ROW_CHUNKS = 2
IN_PROJ_ROW_CHUNKS = 4
QUERY_BLOCKS = 2

F32 = jnp.float32
BF16 = jnp.bfloat16


def _tile(n, target, align):
    best = None
    t = align
    while t <= min(n, target):
        if n % t == 0:
            best = t
        t += align
    if best is None:
        raise ValueError(f"no tile for {n} (target {target}, align {align})")
    return best


def _params(*sem):
    return pltpu.CompilerParams(dimension_semantics=sem, vmem_limit_bytes=VMEM_LIMIT)


def _sigmoid(v):
    return 1.0 / (1.0 + jnp.exp2(v * -LOG2E))


def _rms_mod(x, g, shift, scale):
    var = jnp.mean(x * x, axis=-1, keepdims=True)
    y = x * lax.rsqrt(var + EPS) * g
    return y * (1.0 + scale) + shift


def _adaln_kernel(c_ref, w_ref, b_ref, o_ref):
    c = c_ref[...]
    act = (c * _sigmoid(c)).astype(BF16)
    o_ref[...] = jnp.dot(act, w_ref[...].astype(BF16), preferred_element_type=F32) + b_ref[...]


def _adaln(c, w, b):
    bsz, d = c.shape
    n = w.shape[1]
    tn = _tile(n, ADALN_COLS, LANES)
    return pl.pallas_call(
        _adaln_kernel,
        grid=(n // tn,),
        in_specs=[pl.BlockSpec((bsz, d), lambda j: (0, 0)),
                  pl.BlockSpec((d, tn), lambda j: (0, j)),
                  pl.BlockSpec((1, tn), lambda j: (0, j))],
        out_specs=pl.BlockSpec((bsz, tn), lambda j: (0, j)),
        out_shape=jax.ShapeDtypeStruct((bsz, n), F32),
        compiler_params=_params("parallel"),
        name="adaln_mod",
    )(c, w, b.reshape(1, n))


def _gelu_tanh(v):
    k1 = -2.0 * math.sqrt(2.0 / math.pi) * LOG2E
    return v / (1.0 + jnp.exp2(v * (k1 + (k1 * 0.044715) * (v * v))))


def _rglru_kernel(x_ref, gate_ref, w_ref, cw_ref, cb_ref, ba_ref, bx_ref, lam_ref, o_ref,
                  lx_ref, xtail, hcar, hm_ref, *, conv_k):
    bsz, tt, cw = x_ref.shape
    slabs = cw // LANES
    seq_start = pl.program_id(1) == 0

    @pl.when(seq_start)
    def _():
        xtail[...] = jnp.zeros_like(xtail)
        hcar[...] = jnp.zeros_like(hcar)

    for b in range(bsz):
        xb = x_ref[b].astype(F32)
        for s in range(slabs):
            lx_ref[s, pl.ds(b, tt, stride=SUBLANES), :] = xb[:, s * LANES:(s + 1) * LANES]

    lam = lam_ref[...]
    softplus_neg_lam = jnp.maximum(-lam, 0.0) + jnp.log(1.0 + jnp.exp(-jnp.abs(lam)))
    log2_a_coef = (-LRU_C * LOG2E) * softplus_neg_lam

    def conv_and_gates(s):
        sl = slice(s * LANES, (s + 1) * LANES)
        x3 = lx_ref[s].reshape(tt, SUBLANES, LANES)
        xs = jnp.concatenate([xtail[s], x3], axis=0)
        xtail[s] = xs[tt:]
        xc = cb_ref[:, sl] + cw_ref[conv_k - 1:conv_k, sl] * x3
        for k in range(conv_k - 1):
            xc = xc + cw_ref[k:k + 1, sl] * xs[k:k + tt]
        z = jnp.dot(xc.reshape(tt * SUBLANES, LANES).astype(BF16), w_ref[s],
                    preferred_element_type=F32)
        return xc, z

    ahead = conv_and_gates(0)
    for s in range(slabs):
        sl = slice(s * LANES, (s + 1) * LANES)
        xc, z = ahead
        if s + 1 < slabs:
            ahead = conv_and_gates(s + 1)
        r = _sigmoid(z[:, :LANES] + ba_ref[:, sl]).reshape(tt, SUBLANES, LANES)
        i = _sigmoid(z[:, LANES:] + bx_ref[:, sl]).reshape(tt, SUBLANES, LANES)
        a = jnp.exp2(log2_a_coef[:, sl] * r)
        om = 1.0 - a * a
        mult = jnp.where(om > 0.0, om * lax.rsqrt(om), 0.0)
        mult = jnp.concatenate([jnp.where(seq_start, 1.0, mult[:1]), mult[1:]], axis=0)
        u = mult * (i * xc)
        h = hcar[s]
        for t in range(tt):
            h = a[t] * h + u[t]
            hm_ref[s, t * SUBLANES:(t + 1) * SUBLANES, :] = h
        hcar[s] = h
        for b in range(bsz):
            hb = hm_ref[s, pl.ds(b, tt, stride=SUBLANES), :]
            gate = gate_ref[b, :, sl].astype(F32)
            o_ref[b, :, sl] = (hb * _gelu_tanh(gate)).astype(o_ref.dtype)


def _rglru(proj, wcat, conv_w, conv_b, ba, bx, lam, bsz, seq, lru_w):
    assert bsz == SUBLANES
    n = proj.shape[1]
    conv_k = conv_w.shape[0]
    tt = _tile(seq, RGLRU_STEPS, SUBLANES)
    cw = _tile(lru_w, RGLRU_CHANNELS, LANES)
    slabs = cw // LANES
    vec = pl.BlockSpec((1, cw), lambda c, t: (0, c))
    rec = pl.pallas_call(
        functools.partial(_rglru_kernel, conv_k=conv_k),
        grid=(lru_w // cw, seq // tt),
        in_specs=[pl.BlockSpec((bsz, tt, cw), lambda c, t: (0, t, c)),
                  pl.BlockSpec((bsz, tt, cw), lambda c, t: (0, t, lru_w // cw + c)),
                  pl.BlockSpec((slabs, LANES, 2 * LANES), lambda c, t: (c, 0, 0)),
                  pl.BlockSpec((conv_k, cw), lambda c, t: (0, c)),
                  vec, vec, vec, vec],
        out_specs=pl.BlockSpec((bsz, tt, cw), lambda c, t: (0, t, c)),
        out_shape=jax.ShapeDtypeStruct((bsz, seq, lru_w), BF16),
        scratch_shapes=[pltpu.VMEM((slabs, bsz * tt, LANES), F32),
                        pltpu.VMEM((slabs, conv_k - 1, SUBLANES, LANES), F32),
                        pltpu.VMEM((slabs, SUBLANES, LANES), F32),
                        pltpu.VMEM((slabs, bsz * tt, LANES), F32)],
        compiler_params=_params("parallel", "arbitrary"),
        name="rglru",
    )(proj.reshape(bsz, seq, n), proj.reshape(bsz, seq, n), wcat, conv_w, conv_b.reshape(1, -1),
      ba.reshape(1, -1), bx.reshape(1, -1), lam.reshape(1, -1))
    return rec.reshape(bsz * seq, lru_w)


def _in_proj_kernel(x_ref, sh_ref, sc_ref, g_ref, w_ref, o_ref, h_ref, *, sq_relu):
    j = pl.program_id(1)
    rc = x_ref.shape[0] // IN_PROJ_ROW_CHUNKS

    def project(h):
        y = jnp.dot(h, w_ref[...], preferred_element_type=F32)
        if sq_relu:
            y = jnp.square(jnp.maximum(y, 0.0))
        return y.astype(o_ref.dtype)

    @pl.when(j == 0)
    def _():
        for r in range(IN_PROJ_ROW_CHUNKS):
            rows = pl.ds(r * rc, rc)
            h = _rms_mod(x_ref[rows, :], g_ref[...], sh_ref[...], sc_ref[...]).astype(BF16)
            h_ref[rows, :] = h
            o_ref[rows, :] = project(h)

    @pl.when(j > 0)
    def _():
        o_ref[...] = project(h_ref[...])


def _in_proj(x2d, shift, scale, g, w, seq, tn_target, name, sq_relu=False):
    m, d = x2d.shape
    n = w.shape[1]
    tm = _tile(seq, IN_PROJ_ROWS, SUBLANES * IN_PROJ_ROW_CHUNKS)
    tn = _tile(n, tn_target, LANES)
    per_b = seq // tm
    row = lambda i, j: (i // per_b, 0, 0)
    return pl.pallas_call(
        functools.partial(_in_proj_kernel, sq_relu=sq_relu),
        grid=(m // tm, n // tn),
        in_specs=[pl.BlockSpec((tm, d), lambda i, j: (i, 0)),
                  pl.BlockSpec((None, 1, d), row),
                  pl.BlockSpec((None, 1, d), row),
                  pl.BlockSpec((1, d), lambda i, j: (0, 0)),
                  pl.BlockSpec((d, tn), lambda i, j: (0, j))],
        out_specs=pl.BlockSpec((tm, tn), lambda i, j: (i, j)),
        out_shape=jax.ShapeDtypeStruct((m, n), BF16),
        scratch_shapes=[pltpu.VMEM((tm, d), BF16)],
        compiler_params=_params("parallel", "arbitrary"),
        name=name,
    )(x2d, shift, scale, g, w)


def _t5_bucket(rel, n_buckets):
    max_exact = n_buckets // 2
    relf = jnp.maximum(rel, 1).astype(F32)
    large = max_exact + (jnp.log(relf / max_exact) / math.log(MAX_DISTANCE / max_exact)
                         * (n_buckets - max_exact)).astype(jnp.int32)
    large = jnp.minimum(large, n_buckets - 1)
    return jnp.where(rel < max_exact, rel, large)


def _band_bias_kernel(rb_ref, sink_ref, bucket_ref, o_ref, *, pairs, n_buckets):
    kv = pl.program_id(0)
    blk = ATTN_BLOCK
    bucket = bucket_ref[...]
    qi = lax.broadcasted_iota(jnp.int32, (blk, 2 * blk), 0)
    ki = lax.broadcasted_iota(jnp.int32, (blk, 2 * blk), 1)
    rel = qi + blk - ki
    valid = (rel >= 0) & (rel < blk)
    for p in range(pairs):
        for par in range(2):
            h = (kv * pairs + p) * 2 + par
            bias = jnp.zeros((blk, 2 * blk), F32)
            for b in range(n_buckets):
                bias = jnp.where(bucket == b, rb_ref[b, h], bias)
            sink = sink_ref[h]
            rows, cols = slice(p * blk, (p + 1) * blk), slice(par * 2 * blk, (par + 1) * 2 * blk)
            o_ref[rows, cols] = jnp.where(ki == 0, sink, jnp.where(valid, bias, NEG_INF))


def _band_bias(rel_bias, sinks, n_kv):
    n_buckets, n_heads = rel_bias.shape
    pairs = n_heads // n_kv // 2
    blk = ATTN_BLOCK
    qi = jnp.arange(blk)[:, None]
    ki = jnp.arange(2 * blk)[None, :]
    bucket = _t5_bucket(jnp.maximum(qi + blk - ki, 0), n_buckets)
    smem = pl.BlockSpec(memory_space=pltpu.SMEM)
    return pl.pallas_call(
        functools.partial(_band_bias_kernel, pairs=pairs, n_buckets=n_buckets),
        grid=(n_kv,),
        in_specs=[smem, smem, pl.BlockSpec((blk, 2 * blk), lambda kv: (0, 0))],
        out_specs=pl.BlockSpec((None, pairs * blk, 4 * blk), lambda kv: (kv, 0, 0)),
        out_shape=jax.ShapeDtypeStruct((n_kv, pairs * blk, 4 * blk), F32),
        compiler_params=_params("parallel"),
        name="band_bias",
    )(rel_bias, sinks, bucket)


def _swap_lane_halves(v):
    u = pltpu.bitcast(v, jnp.uint32)
    return pltpu.bitcast(pltpu.roll(u, LANES // 2, axis=1), BF16)


def _mix_kernel(*refs, n_kv, pairs, scale, per_b):
    q_ref, kp_ref, kc_ref, vp_ref, vc_ref, bm_ref, rec_ref = refs[:7]
    ga_refs, gb_refs = refs[7:7 + n_kv], refs[7 + n_kv:7 + 2 * n_kv]
    x_ref, gate_ref, wl_ref, wa_ref, wo_ref, o_ref, att_ref, ya_ref, m_ref = refs[7 + 2 * n_kv:]
    blk = ATTN_BLOCK
    d = o_ref.shape[1]
    cc = d // n_kv
    fold = math.log2(scale).is_integer()
    lane = lax.broadcasted_iota(jnp.int32, (blk, LANES), 1)
    key = lax.broadcasted_iota(jnp.int32, (blk, LANES), 0)
    low = jnp.where(lane < LANES // 2, 1.0, 0.0).astype(BF16)
    high = jnp.where(lane < LANES // 2, 0.0, 1.0).astype(BF16)
    not_key0 = jnp.where(key == 0, 0.0, 1.0).astype(BF16)
    ones_blk = jnp.concatenate([low, low, high, high], axis=0)
    key_col = lax.broadcasted_iota(jnp.int32, (1, 4 * blk), 1) % (2 * blk)
    seq_start = pl.program_id(0) % per_b == 0
    start_mask = jnp.where(seq_start & (key_col >= 1) & (key_col < blk), NEG_INF, 0.0)

    def halves(t, kv):
        s = _swap_lane_halves(t)
        lo, hi = (t, s) if kv % 2 == 0 else (s, t)
        return lo * low, hi * high

    def window(prev, cur):
        return jnp.concatenate([prev[0] * not_key0, cur[0], prev[1] * not_key0, cur[1]], axis=0)

    for kv in range(n_kv):
        sl = slice((kv // 2) * LANES, (kv // 2 + 1) * LANES)
        kb = [halves(kp_ref[:, sl], kv)] + [halves(kc_ref[t * blk:(t + 1) * blk, sl], kv)
                                            for t in range(QUERY_BLOCKS)]
        vb = [halves(vp_ref[:, sl], kv)] + [halves(vc_ref[t * blk:(t + 1) * blk, sl], kv)
                                            for t in range(QUERY_BLOCKS)]
        for t in range(QUERY_BLOCKS):
            rows = slice(t * blk, (t + 1) * blk)
            kblk = window(kb[t], kb[t + 1])
            vaug = jnp.concatenate([window(vb[t], vb[t + 1]), ones_blk], axis=1)
            q = jnp.concatenate(
                [q_ref[rows, (kv * pairs + p) * LANES:(kv * pairs + p + 1) * LANES]
                 for p in range(pairs)], axis=0)
            if fold:
                q = q * scale
            s = lax.dot_general(q, kblk, (((1,), (1,)), ((), ())), preferred_element_type=F32)
            if not fold:
                s = s * scale
            s = s + bm_ref[kv]
            if t == 0:
                s = s + start_mask
            es = []
            for par in range(2):
                sp = s[:, par * 2 * blk:(par + 1) * 2 * blk]
                es.append(jnp.exp(sp - jnp.max(sp, axis=-1, keepdims=True)).astype(BF16))
            o = jnp.dot(jnp.concatenate(es, axis=1), vaug, preferred_element_type=F32)
            res = (o[:, :LANES] / o[:, LANES:]).astype(BF16)
            for p in range(pairs):
                att_ref[rows, (kv * pairs + p) * LANES:(kv * pairs + p + 1) * LANES] = (
                    res[p * blk:(p + 1) * blk])
        cols = slice(kv * cc, (kv + 1) * cc)
        ya = jnp.dot(rec_ref[...], wl_ref[:, cols], preferred_element_type=F32)
        ya_ref[:, cols] = _sigmoid(ga_refs[kv][...].astype(F32)) * ya

    for c in range(n_kv):
        cols = slice(c * cc, (c + 1) * cc)
        yb = jnp.dot(att_ref[...], wa_ref[:, cols], preferred_element_type=F32)
        gb = _sigmoid(gb_refs[c][...].astype(F32))
        m_ref[:, cols] = (ya_ref[:, cols] + gb * yb).astype(BF16)
    for c in range(n_kv):
        cols = slice(c * cc, (c + 1) * cc)
        y = jnp.dot(m_ref[...], wo_ref[:, cols], preferred_element_type=F32)
        o_ref[:, cols] = x_ref[:, cols] + gate_ref[:, cols] * y


def _mix(rest, rec, x2d, gate, band_bias, w_lru, w_att, w_out, seq, q_off, k_off, v_off, ga_off,
         gb_off, kv_w, n_heads):
    m, d = x2d.shape
    attn_w = w_att.shape[0]
    hd = attn_w // n_heads
    n_kv = kv_w // hd
    pairs = n_heads // n_kv // 2
    blk = ATTN_BLOCK
    tm = QUERY_BLOCKS * blk
    per_b = seq // tm
    cc = d // n_kv
    assert 2 * hd == LANES and n_heads == 2 * pairs * n_kv and kv_w % LANES == 0
    assert q_off % attn_w == 0 and k_off % kv_w == 0 and v_off % kv_w == 0
    assert seq % tm == 0 and cc % LANES == 0 and ga_off % cc == 0 and gb_off % cc == 0
    kb, vb = k_off // kv_w, v_off // kv_w
    prev = lambda col: (lambda i: (jnp.maximum(QUERY_BLOCKS * i - 1, 0), col))
    col_chunk = lambda off: [pl.BlockSpec((tm, cc), functools.partial(lambda c, i: (i, c),
                                                                      off // cc + c))
                             for c in range(n_kv)]
    resident = lambda a: pl.BlockSpec(a.shape, lambda i: (0,) * a.ndim,
                                      pipeline_mode=pl.Buffered(1))
    kernel = functools.partial(_mix_kernel, n_kv=n_kv, pairs=pairs, scale=hd ** -0.5, per_b=per_b)
    return pl.pallas_call(
        kernel,
        grid=(m // tm,),
        in_specs=[pl.BlockSpec((tm, attn_w), lambda i: (i, q_off // attn_w)),
                  pl.BlockSpec((blk, kv_w), prev(kb)),
                  pl.BlockSpec((tm, kv_w), lambda i: (i, kb)),
                  pl.BlockSpec((blk, kv_w), prev(vb)),
                  pl.BlockSpec((tm, kv_w), lambda i: (i, vb)),
                  resident(band_bias),
                  pl.BlockSpec((tm, rec.shape[1]), lambda i: (i, 0))]
                 + col_chunk(ga_off) + col_chunk(gb_off)
                 + [pl.BlockSpec((tm, d), lambda i: (i, 0)),
                    pl.BlockSpec((None, 1, d), lambda i: (i // per_b, 0, 0)),
                    resident(w_lru), resident(w_att), resident(w_out)],
        out_specs=pl.BlockSpec((tm, d), lambda i: (i, 0)),
        out_shape=jax.ShapeDtypeStruct((m, d), F32),
        scratch_shapes=[pltpu.VMEM((tm, attn_w), BF16), pltpu.VMEM((tm, d), F32),
                        pltpu.VMEM((tm, d), BF16)],
        compiler_params=_params("parallel"),
        name="mix",
    )(rest, rest, rest, rest, rest, band_bias, rec, *([rest] * (2 * n_kv)), x2d, gate,
      w_lru, w_att, w_out)


def _mlp_down_kernel(ff_ref, w2_ref, x_ref, gate_ref, gf_ref, o_ref):
    k = pl.program_id(1)
    last = pl.num_programs(1) - 1
    tm, d = o_ref.shape
    cc = max(d // COL_CHUNKS, LANES)
    rc = tm // ROW_CHUNKS

    @pl.when(k == 0)
    def _():
        for c in range(0, d, cc):
            o_ref[:, c:c + cc] = jnp.dot(ff_ref[...], w2_ref[:, c:c + cc],
                                         preferred_element_type=F32)

    @pl.when((k > 0) & (k < last))
    def _():
        for c in range(0, d, cc):
            o_ref[:, c:c + cc] += jnp.dot(ff_ref[...], w2_ref[:, c:c + cc],
                                          preferred_element_type=F32)

    @pl.when(k == last)
    def _():
        for r in range(ROW_CHUNKS):
            rows = pl.ds(r * rc, rc)
            y = o_ref[rows, :] + jnp.dot(ff_ref[rows, :], w2_ref[...], preferred_element_type=F32)
            x2 = x_ref[rows, :] + gate_ref[...] * y
            var = jnp.mean(x2 * x2, axis=-1, keepdims=True)
            o_ref[rows, :] = x2 * lax.rsqrt(var + EPS) * gf_ref[...]


def _mlp_down(ff, w2, x1, gate, gf, seq):
    m, dff = ff.shape
    d = w2.shape[1]
    tm = _tile(seq, MLP_ROWS, SUBLANES * ROW_CHUNKS)
    tk = _tile(dff // 2, MLP_DOWN_K, LANES)
    per_b = seq // tm
    return pl.pallas_call(
        _mlp_down_kernel,
        grid=(m // tm, dff // tk),
        in_specs=[pl.BlockSpec((tm, tk), lambda i, k: (i, k)),
                  pl.BlockSpec((tk, d), lambda i, k: (k, 0)),
                  pl.BlockSpec((tm, d), lambda i, k: (i, 0)),
                  pl.BlockSpec((None, 1, d), lambda i, k: (i // per_b, 0, 0)),
                  pl.BlockSpec((1, d), lambda i, k: (0, 0))],
        out_specs=pl.BlockSpec((tm, d), lambda i, k: (i, 0)),
        out_shape=jax.ShapeDtypeStruct((m, d), F32),
        compiler_params=_params("parallel", "arbitrary"),
        name="mlp_down",
    )(ff, w2, x1, gate, gf)


def kernel(x, c, w_ada, b_ada, norm1_g, w_in, conv_w, conv_b, lru_wa, lru_ba, lru_wx, lru_bx,
           lru_lambda, w_lru_out, w_attn_out, attn_sinks, rel_bias, w_out, norm2_g, w_ff1, w_ff2,
           final_g):
    bsz, seq, d = x.shape
    depth = w_in.shape[0]
    lru_w = w_lru_out.shape[1]
    attn_w = w_attn_out.shape[1]
    kv_w = (w_in.shape[2] - 2 * lru_w - attn_w - 2 * d) // 2
    n_heads = attn_sinks.shape[1]
    n_kv = kv_w // (attn_w // n_heads)
    q_off = 2 * lru_w
    k_off = q_off + attn_w
    v_off = k_off + kv_w
    ga_off = v_off + kv_w
    gb_off = ga_off + d

    xs = x.reshape(bsz * seq, d)
    for l in range(depth):
        band_bias = _band_bias(rel_bias, attn_sinks[l], n_kv)
        mod = _adaln(c, w_ada[l], b_ada[l])
        shift1, scale1, gate1, shift2, scale2, gate2 = [
            t.reshape(bsz, 1, d) for t in jnp.split(mod, 6, axis=-1)]

        rest = _in_proj(xs, shift1, scale1, norm1_g[l].reshape(1, d), w_in[l].astype(BF16), seq,
                        IN_PROJ_COLS, "in_proj")
        wcat = jnp.concatenate([lru_wa[l], lru_wx[l]], axis=-1).astype(BF16)
        rec = _rglru(rest, wcat, conv_w[l], conv_b[l], lru_ba[l], lru_bx[l], lru_lambda[l],
                     bsz, seq, lru_w)
        x1 = _mix(rest, rec, xs, gate1, band_bias, w_lru_out[l].astype(BF16),
                  w_attn_out[l].astype(BF16), w_out[l].astype(BF16), seq, q_off, k_off, v_off,
                  ga_off, gb_off, kv_w, n_heads)
        if l != depth - 1:
            raise NotImplementedError("the fused final norm assumes a single layer")
        ff = _in_proj(x1, shift2, scale2, norm2_g[l].reshape(1, d), w_ff1[l].astype(BF16), seq,
                      MLP_UP_COLS, "mlp_up", sq_relu=True)
        xs = _mlp_down(ff, w_ff2[l].astype(BF16), x1, gate2, final_g.reshape(1, d), seq)
    return xs.reshape(bsz, seq, d)
```

```python
import functools
import math

import jax
import jax.numpy as jnp
from jax import lax
from jax.experimental import pallas as pl
from jax.experimental.pallas import tpu as pltpu

EPS = 1e-6
LRU_C = 8.0
LOG2E = 1.4426950408889634
ATTN_BLOCK = 128
NEG_INF = -1e30
MAX_DISTANCE = 128
LANES = 128
SUBLANES = 8
VMEM_LIMIT = 56 * 1024 * 1024
IN_PROJ_ROWS, IN_PROJ_COLS = 1024, 1792
RGLRU_STEPS, RGLRU_CHANNELS = 256, 1024
MLP_UP_COLS = 2048
MLP_ROWS, MLP_DOWN_K = 1024, 1024
COL_CHUNKS = 4
ADALN_COLS = 1024
ROW_CHUNKS = 2
IN_PROJ_ROW_CHUNKS = 2
QUERY_BLOCKS = 2

F32 = jnp.float32
BF16 = jnp.bfloat16


def _tile(n, target, align):
    best = None
    t = align
    while t <= min(n, target):
        if n % t == 0:
            best = t
        t += align
    if best is None:
        raise ValueError(f"no tile for {n} (target {target}, align {align})")
    return best


def _params(*sem):
    return pltpu.CompilerParams(dimension_semantics=sem, vmem_limit_bytes=VMEM_LIMIT)


def _sigmoid(v):
    return 1.0 / (1.0 + jnp.exp2(v * -LOG2E))


def _rms_mod(x, g, shift, scale):
    var = jnp.mean(x * x, axis=-1, keepdims=True)
    y = x * lax.rsqrt(var + EPS) * g
    return y * (1.0 + scale) + shift


def _adaln_kernel(c_ref, w_ref, b_ref, o_ref):
    c = c_ref[...]
    act = (c * _sigmoid(c)).astype(BF16)
    o_ref[...] = jnp.dot(act, w_ref[...].astype(BF16), preferred_element_type=F32) + b_ref[...]


def _adaln(c, w, b):
    bsz, d = c.shape
    n = w.shape[1]
    tn = _tile(n, ADALN_COLS, LANES)
    return pl.pallas_call(
        _adaln_kernel,
        grid=(n // tn,),
        in_specs=[pl.BlockSpec((bsz, d), lambda j: (0, 0)),
                  pl.BlockSpec((d, tn), lambda j: (0, j)),
                  pl.BlockSpec((1, tn), lambda j: (0, j))],
        out_specs=pl.BlockSpec((bsz, tn), lambda j: (0, j)),
        out_shape=jax.ShapeDtypeStruct((bsz, n), F32),
        compiler_params=_params("parallel"),
        name="adaln_mod",
    )(c, w, b.reshape(1, n))


def _gelu_tanh(v):
    k1 = -2.0 * math.sqrt(2.0 / math.pi) * LOG2E
    return v / (1.0 + jnp.exp2(v * (k1 + (k1 * 0.044715) * (v * v))))


def _rglru_kernel(x_ref, gate_ref, w_ref, cw_ref, cb_ref, ba_ref, bx_ref, lam_ref, o_ref,
                  lx_ref, xtail, hcar, hm_ref, *, conv_k):
    bsz, tt, cw = x_ref.shape
    slabs = cw // LANES
    seq_start = pl.program_id(1) == 0

    @pl.when(seq_start)
    def _():
        xtail[...] = jnp.zeros_like(xtail)
        hcar[...] = jnp.zeros_like(hcar)

    for b in range(bsz):
        xb = x_ref[b].astype(F32)
        for s in range(slabs):
            lx_ref[s, pl.ds(b, tt, stride=SUBLANES), :] = xb[:, s * LANES:(s + 1) * LANES]

    lam = lam_ref[...]
    softplus_neg_lam = jnp.maximum(-lam, 0.0) + jnp.log(1.0 + jnp.exp(-jnp.abs(lam)))
    log2_a_coef = (-LRU_C * LOG2E) * softplus_neg_lam

    def conv_and_gates(s):
        sl = slice(s * LANES, (s + 1) * LANES)
        x3 = lx_ref[s].reshape(tt, SUBLANES, LANES)
        xs = jnp.concatenate([xtail[s], x3], axis=0)
        xtail[s] = xs[tt:]
        xc = cb_ref[:, sl] + cw_ref[conv_k - 1:conv_k, sl] * x3
        for k in range(conv_k - 1):
            xc = xc + cw_ref[k:k + 1, sl] * xs[k:k + tt]
        z = jnp.dot(xc.reshape(tt * SUBLANES, LANES).astype(BF16), w_ref[s],
                    preferred_element_type=F32)
        return xc, z

    ahead = conv_and_gates(0)
    for s in range(slabs):
        sl = slice(s * LANES, (s + 1) * LANES)
        xc, z = ahead
        if s + 1 < slabs:
            ahead = conv_and_gates(s + 1)
        r = _sigmoid(z[:, :LANES] + ba_ref[:, sl]).reshape(tt, SUBLANES, LANES)
        i = _sigmoid(z[:, LANES:] + bx_ref[:, sl]).reshape(tt, SUBLANES, LANES)
        a = jnp.exp2(log2_a_coef[:, sl] * r)
        om = 1.0 - a * a
        mult = jnp.where(om > 0.0, om * lax.rsqrt(om), 0.0)
        mult = jnp.concatenate([jnp.where(seq_start, 1.0, mult[:1]), mult[1:]], axis=0)
        u = mult * (i * xc)
        h = hcar[s]
        for t in range(tt):
            h = a[t] * h + u[t]
            hm_ref[s, t * SUBLANES:(t + 1) * SUBLANES, :] = h
        hcar[s] = h
        for b in range(bsz):
            hb = hm_ref[s, pl.ds(b, tt, stride=SUBLANES), :]
            gate = gate_ref[b, :, sl].astype(F32)
            o_ref[b, :, sl] = (hb * _gelu_tanh(gate)).astype(o_ref.dtype)


def _rglru(proj, wcat, conv_w, conv_b, ba, bx, lam, bsz, seq, lru_w):
    assert bsz == SUBLANES
    n = proj.shape[1]
    conv_k = conv_w.shape[0]
    tt = _tile(seq, RGLRU_STEPS, SUBLANES)
    cw = _tile(lru_w, RGLRU_CHANNELS, LANES)
    slabs = cw // LANES
    vec = pl.BlockSpec((1, cw), lambda c, t: (0, c))
    rec = pl.pallas_call(
        functools.partial(_rglru_kernel, conv_k=conv_k),
        grid=(lru_w // cw, seq // tt),
        in_specs=[pl.BlockSpec((bsz, tt, cw), lambda c, t: (0, t, c)),
                  pl.BlockSpec((bsz, tt, cw), lambda c, t: (0, t, lru_w // cw + c)),
                  pl.BlockSpec((slabs, LANES, 2 * LANES), lambda c, t: (c, 0, 0)),
                  pl.BlockSpec((conv_k, cw), lambda c, t: (0, c)),
                  vec, vec, vec, vec],
        out_specs=pl.BlockSpec((bsz, tt, cw), lambda c, t: (0, t, c)),
        out_shape=jax.ShapeDtypeStruct((bsz, seq, lru_w), BF16),
        scratch_shapes=[pltpu.VMEM((slabs, bsz * tt, LANES), F32),
                        pltpu.VMEM((slabs, conv_k - 1, SUBLANES, LANES), F32),
                        pltpu.VMEM((slabs, SUBLANES, LANES), F32),
                        pltpu.VMEM((slabs, bsz * tt, LANES), F32)],
        compiler_params=_params("parallel", "arbitrary"),
        name="rglru",
    )(proj.reshape(bsz, seq, n), proj.reshape(bsz, seq, n), wcat, conv_w, conv_b.reshape(1, -1),
      ba.reshape(1, -1), bx.reshape(1, -1), lam.reshape(1, -1))
    return rec.reshape(bsz * seq, lru_w)


def _in_proj_kernel(x_ref, sh_ref, sc_ref, g_ref, w_ref, o_ref, h_ref, *, sq_relu):
    j = pl.program_id(1)
    rc = x_ref.shape[0] // IN_PROJ_ROW_CHUNKS

    def project(h):
        y = jnp.dot(h, w_ref[...], preferred_element_type=F32)
        if sq_relu:
            y = jnp.square(jnp.maximum(y, 0.0))
        return y.astype(o_ref.dtype)

    @pl.when(j == 0)
    def _():
        for r in range(IN_PROJ_ROW_CHUNKS):
            rows = pl.ds(r * rc, rc)
            h = _rms_mod(x_ref[rows, :], g_ref[...], sh_ref[...], sc_ref[...]).astype(BF16)
            h_ref[rows, :] = h
            o_ref[rows, :] = project(h)

    @pl.when(j > 0)
    def _():
        o_ref[...] = project(h_ref[...])


def _in_proj(x2d, shift, scale, g, w, seq, tn_target, name, sq_relu=False):
    m, d = x2d.shape
    n = w.shape[1]
    tm = _tile(seq, IN_PROJ_ROWS, SUBLANES * IN_PROJ_ROW_CHUNKS)
    tn = _tile(n, tn_target, LANES)
    per_b = seq // tm
    row = lambda i, j: (i // per_b, 0, 0)
    return pl.pallas_call(
        functools.partial(_in_proj_kernel, sq_relu=sq_relu),
        grid=(m // tm, n // tn),
        in_specs=[pl.BlockSpec((tm, d), lambda i, j: (i, 0)),
                  pl.BlockSpec((None, 1, d), row),
                  pl.BlockSpec((None, 1, d), row),
                  pl.BlockSpec((1, d), lambda i, j: (0, 0)),
                  pl.BlockSpec((d, tn), lambda i, j: (0, j))],
        out_specs=pl.BlockSpec((tm, tn), lambda i, j: (i, j)),
        out_shape=jax.ShapeDtypeStruct((m, n), BF16),
        scratch_shapes=[pltpu.VMEM((tm, d), BF16)],
        compiler_params=_params("parallel", "arbitrary"),
        name=name,
    )(x2d, shift, scale, g, w)


def _t5_bucket(rel, n_buckets):
    max_exact = n_buckets // 2
    relf = jnp.maximum(rel, 1).astype(F32)
    large = max_exact + (jnp.log(relf / max_exact) / math.log(MAX_DISTANCE / max_exact)
                         * (n_buckets - max_exact)).astype(jnp.int32)
    large = jnp.minimum(large, n_buckets - 1)
    return jnp.where(rel < max_exact, rel, large)


def _band_bias_kernel(rb_ref, sink_ref, bucket_ref, o_ref, *, pairs, n_buckets):
    kv = pl.program_id(0)
    blk = ATTN_BLOCK
    bucket = bucket_ref[...]
    qi = lax.broadcasted_iota(jnp.int32, (blk, 2 * blk), 0)
    ki = lax.broadcasted_iota(jnp.int32, (blk, 2 * blk), 1)
    rel = qi + blk - ki
    valid = (rel >= 0) & (rel < blk)
    for p in range(pairs):
        for par in range(2):
            h = (kv * pairs + p) * 2 + par
            bias = jnp.zeros((blk, 2 * blk), F32)
            for b in range(n_buckets):
                bias = jnp.where(bucket == b, rb_ref[b, h], bias)
            sink = sink_ref[h]
            rows, cols = slice(p * blk, (p + 1) * blk), slice(par * 2 * blk, (par + 1) * 2 * blk)
            o_ref[rows, cols] = jnp.where(ki == 0, sink, jnp.where(valid, bias, NEG_INF))


def _band_bias(rel_bias, sinks, n_kv):
    n_buckets, n_heads = rel_bias.shape
    pairs = n_heads // n_kv // 2
    blk = ATTN_BLOCK
    qi = jnp.arange(blk)[:, None]
    ki = jnp.arange(2 * blk)[None, :]
    bucket = _t5_bucket(jnp.maximum(qi + blk - ki, 0), n_buckets)
    smem = pl.BlockSpec(memory_space=pltpu.SMEM)
    return pl.pallas_call(
        functools.partial(_band_bias_kernel, pairs=pairs, n_buckets=n_buckets),
        grid=(n_kv,),
        in_specs=[smem, smem, pl.BlockSpec((blk, 2 * blk), lambda kv: (0, 0))],
        out_specs=pl.BlockSpec((None, pairs * blk, 4 * blk), lambda kv: (kv, 0, 0)),
        out_shape=jax.ShapeDtypeStruct((n_kv, pairs * blk, 4 * blk), F32),
        compiler_params=_params("parallel"),
        name="band_bias",
    )(rel_bias, sinks, bucket)


def _swap_lane_halves(v):
    u = pltpu.bitcast(v, jnp.uint32)
    return pltpu.bitcast(pltpu.roll(u, LANES // 2, axis=1), BF16)


def _mix_kernel(*refs, n_kv, pairs, scale, per_b):
    q_ref, kp_ref, kc_ref, vp_ref, vc_ref, bm_ref, rec_ref = refs[:7]
    ga_refs, gb_refs = refs[7:7 + n_kv], refs[7 + n_kv:7 + 2 * n_kv]
    x_ref, gate_ref, wl_ref, wa_ref, wo_ref, o_ref, att_ref, ya_ref, m_ref = refs[7 + 2 * n_kv:]
    blk = ATTN_BLOCK
    d = o_ref.shape[1]
    cc = d // n_kv
    fold = math.log2(scale).is_integer()
    lane = lax.broadcasted_iota(jnp.int32, (blk, LANES), 1)
    key = lax.broadcasted_iota(jnp.int32, (blk, LANES), 0)
    low = jnp.where(lane < LANES // 2, 1.0, 0.0).astype(BF16)
    high = jnp.where(lane < LANES // 2, 0.0, 1.0).astype(BF16)
    not_key0 = jnp.where(key == 0, 0.0, 1.0).astype(BF16)
    ones_blk = jnp.concatenate([low, low, high, high], axis=0)
    key_col = lax.broadcasted_iota(jnp.int32, (1, 4 * blk), 1) % (2 * blk)
    seq_start = pl.program_id(0) % per_b == 0
    start_mask = jnp.where(seq_start & (key_col >= 1) & (key_col < blk), NEG_INF, 0.0)

    def halves(t, kv):
        s = _swap_lane_halves(t)
        lo, hi = (t, s) if kv % 2 == 0 else (s, t)
        return lo * low, hi * high

    def window(prev, cur):
        return jnp.concatenate([prev[0] * not_key0, cur[0], prev[1] * not_key0, cur[1]], axis=0)

    for kv in range(n_kv):
        sl = slice((kv // 2) * LANES, (kv // 2 + 1) * LANES)
        kb = [halves(kp_ref[:, sl], kv)] + [halves(kc_ref[t * blk:(t + 1) * blk, sl], kv)
                                            for t in range(QUERY_BLOCKS)]
        vb = [halves(vp_ref[:, sl], kv)] + [halves(vc_ref[t * blk:(t + 1) * blk, sl], kv)
                                            for t in range(QUERY_BLOCKS)]
        for t in range(QUERY_BLOCKS):
            rows = slice(t * blk, (t + 1) * blk)
            kblk = window(kb[t], kb[t + 1])
            vaug = jnp.concatenate([window(vb[t], vb[t + 1]), ones_blk], axis=1)
            q = jnp.concatenate(
                [q_ref[rows, (kv * pairs + p) * LANES:(kv * pairs + p + 1) * LANES]
                 for p in range(pairs)], axis=0)
            if fold:
                q = q * scale
            s = lax.dot_general(q, kblk, (((1,), (1,)), ((), ())), preferred_element_type=F32)
            if not fold:
                s = s * scale
            s = s + bm_ref[kv]
            if t == 0:
                s = s + start_mask
            es = []
            for par in range(2):
                sp = s[:, par * 2 * blk:(par + 1) * 2 * blk]
                es.append(jnp.exp(sp - jnp.max(sp, axis=-1, keepdims=True)).astype(BF16))
            o = jnp.dot(jnp.concatenate(es, axis=1), vaug, preferred_element_type=F32)
            res = (o[:, :LANES] / o[:, LANES:]).astype(BF16)
            for p in range(pairs):
                att_ref[rows, (kv * pairs + p) * LANES:(kv * pairs + p + 1) * LANES] = (
                    res[p * blk:(p + 1) * blk])
        cols = slice(kv * cc, (kv + 1) * cc)
        ya = jnp.dot(rec_ref[...], wl_ref[:, cols], preferred_element_type=F32)
        ya_ref[:, cols] = _sigmoid(ga_refs[kv][...].astype(F32)) * ya

    for c in range(n_kv):
        cols = slice(c * cc, (c + 1) * cc)
        yb = jnp.dot(att_ref[...], wa_ref[:, cols], preferred_element_type=F32)
        gb = _sigmoid(gb_refs[c][...].astype(F32))
        m_ref[:, cols] = (ya_ref[:, cols] + gb * yb).astype(BF16)
    for c in range(n_kv):
        cols = slice(c * cc, (c + 1) * cc)
        y = jnp.dot(m_ref[...], wo_ref[:, cols], preferred_element_type=F32)
        o_ref[:, cols] = x_ref[:, cols] + gate_ref[:, cols] * y


def _mix(rest, rec, x2d, gate, band_bias, w_lru, w_att, w_out, seq, q_off, k_off, v_off, ga_off,
         gb_off, kv_w, n_heads):
    m, d = x2d.shape
    attn_w = w_att.shape[0]
    hd = attn_w // n_heads
    n_kv = kv_w // hd
    pairs = n_heads // n_kv // 2
    blk = ATTN_BLOCK
    tm = QUERY_BLOCKS * blk
    per_b = seq // tm
    cc = d // n_kv
    assert 2 * hd == LANES and n_heads == 2 * pairs * n_kv and kv_w % LANES == 0
    assert q_off % attn_w == 0 and k_off % kv_w == 0 and v_off % kv_w == 0
    assert seq % tm == 0 and cc % LANES == 0 and ga_off % cc == 0 and gb_off % cc == 0
    kb, vb = k_off // kv_w, v_off // kv_w
    prev = lambda col: (lambda i: (jnp.maximum(QUERY_BLOCKS * i - 1, 0), col))
    col_chunk = lambda off: [pl.BlockSpec((tm, cc), functools.partial(lambda c, i: (i, c),
                                                                      off // cc + c))
                             for c in range(n_kv)]
    resident = lambda a: pl.BlockSpec(a.shape, lambda i: (0,) * a.ndim,
                                      pipeline_mode=pl.Buffered(1))
    kernel = functools.partial(_mix_kernel, n_kv=n_kv, pairs=pairs, scale=hd ** -0.5, per_b=per_b)
    return pl.pallas_call(
        kernel,
        grid=(m // tm,),
        in_specs=[pl.BlockSpec((tm, attn_w), lambda i: (i, q_off // attn_w)),
                  pl.BlockSpec((blk, kv_w), prev(kb)),
                  pl.BlockSpec((tm, kv_w), lambda i: (i, kb)),
                  pl.BlockSpec((blk, kv_w), prev(vb)),
                  pl.BlockSpec((tm, kv_w), lambda i: (i, vb)),
                  resident(band_bias),
                  pl.BlockSpec((tm, rec.shape[1]), lambda i: (i, 0))]
                 + col_chunk(ga_off) + col_chunk(gb_off)
                 + [pl.BlockSpec((tm, d), lambda i: (i, 0)),
                    pl.BlockSpec((None, 1, d), lambda i: (i // per_b, 0, 0)),
                    resident(w_lru), resident(w_att), resident(w_out)],
        out_specs=pl.BlockSpec((tm, d), lambda i: (i, 0)),
        out_shape=jax.ShapeDtypeStruct((m, d), F32),
        scratch_shapes=[pltpu.VMEM((tm, attn_w), BF16), pltpu.VMEM((tm, d), F32),
                        pltpu.VMEM((tm, d), BF16)],
        compiler_params=_params("parallel"),
        name="mix",
    )(rest, rest, rest, rest, rest, band_bias, rec, *([rest] * (2 * n_kv)), x2d, gate,
      w_lru, w_att, w_out)


def _mlp_down_kernel(ff_ref, w2_ref, x_ref, gate_ref, gf_ref, o_ref):
    k = pl.program_id(1)
    last = pl.num_programs(1) - 1
    tm, d = o_ref.shape
    cc = max(d // COL_CHUNKS, LANES)
    rc = tm // ROW_CHUNKS

    @pl.when(k == 0)
    def _():
        for c in range(0, d, cc):
            o_ref[:, c:c + cc] = jnp.dot(ff_ref[...], w2_ref[:, c:c + cc],
                                         preferred_element_type=F32)

    @pl.when((k > 0) & (k < last))
    def _():
        for c in range(0, d, cc):
            o_ref[:, c:c + cc] += jnp.dot(ff_ref[...], w2_ref[:, c:c + cc],
                                          preferred_element_type=F32)

    @pl.when(k == last)
    def _():
        for r in range(ROW_CHUNKS):
            rows = pl.ds(r * rc, rc)
            y = o_ref[rows, :] + jnp.dot(ff_ref[rows, :], w2_ref[...], preferred_element_type=F32)
            x2 = x_ref[rows, :] + gate_ref[...] * y
            var = jnp.mean(x2 * x2, axis=-1, keepdims=True)
            o_ref[rows, :] = x2 * lax.rsqrt(var + EPS) * gf_ref[...]


def _mlp_down(ff, w2, x1, gate, gf, seq):
    m, dff = ff.shape
    d = w2.shape[1]
    tm = _tile(seq, MLP_ROWS, SUBLANES * ROW_CHUNKS)
    tk = _tile(dff // 2, MLP_DOWN_K, LANES)
    per_b = seq // tm
    return pl.pallas_call(
        _mlp_down_kernel,
        grid=(m // tm, dff // tk),
        in_specs=[pl.BlockSpec((tm, tk), lambda i, k: (i, k)),
                  pl.BlockSpec((tk, d), lambda i, k: (k, 0)),
                  pl.BlockSpec((tm, d), lambda i, k: (i, 0)),
                  pl.BlockSpec((None, 1, d), lambda i, k: (i // per_b, 0, 0)),
                  pl.BlockSpec((1, d), lambda i, k: (0, 0))],
        out_specs=pl.BlockSpec((tm, d), lambda i, k: (i, 0)),
        out_shape=jax.ShapeDtypeStruct((m, d), F32),
        compiler_params=_params("parallel", "arbitrary"),
        name="mlp_down",
    )(ff, w2, x1, gate, gf)


def kernel(x, c, w_ada, b_ada, norm1_g, w_in, conv_w, conv_b, lru_wa, lru_ba, lru_wx, lru_bx,
           lru_lambda, w_lru_out, w_attn_out, attn_sinks, rel_bias, w_out, norm2_g, w_ff1, w_ff2,
           final_g):
    bsz, seq, d = x.shape
    depth = w_in.shape[0]
    lru_w = w_lru_out.shape[1]
    attn_w = w_attn_out.shape[1]
    kv_w = (w_in.shape[2] - 2 * lru_w - attn_w - 2 * d) // 2
    n_heads = attn_sinks.shape[1]
    n_kv = kv_w // (attn_w // n_heads)
    q_off = 2 * lru_w
    k_off = q_off + attn_w
    v_off = k_off + kv_w
    ga_off = v_off + kv_w
    gb_off = ga_off + d

    xs = x.reshape(bsz * seq, d)
    for l in range(depth):
        band_bias = _band_bias(rel_bias, attn_sinks[l], n_kv)
        mod = _adaln(c, w_ada[l], b_ada[l])
        shift1, scale1, gate1, shift2, scale2, gate2 = [
            t.reshape(bsz, 1, d) for t in jnp.split(mod, 6, axis=-1)]

        rest = _in_proj(xs, shift1, scale1, norm1_g[l].reshape(1, d), w_in[l].astype(BF16), seq,
                        IN_PROJ_COLS, "in_proj")
        wcat = jnp.concatenate([lru_wa[l], lru_wx[l]], axis=-1).astype(BF16)
        rec = _rglru(rest, wcat, conv_w[l], conv_b[l], lru_ba[l], lru_bx[l], lru_lambda[l],
                     bsz, seq, lru_w)
        x1 = _mix(rest, rec, xs, gate1, band_bias, w_lru_out[l].astype(BF16),
                  w_attn_out[l].astype(BF16), w_out[l].astype(BF16), seq, q_off, k_off, v_off,
                  ga_off, gb_off, kv_w, n_heads)
        if l != depth - 1:
            raise NotImplementedError("the fused final norm assumes a single layer")
        ff = _in_proj(x1, shift2, scale2, norm2_g[l].reshape(1, d), w_ff1[l].astype(BF16), seq,
                      MLP_UP_COLS, "mlp_up", sq_relu=True)
        xs = _mlp_down(ff, w_ff2[l].astype(BF16), x1, gate2, final_g.reshape(1, d), seq)
    return xs.reshape(bsz, seq, d)
```

```python
import functools
import math

import jax
import jax.numpy as jnp
from jax import lax
from jax.experimental import pallas as pl
from jax.experimental.pallas import tpu as pltpu

EPS = 1e-6
LRU_C = 8.0
LOG2E = 1.4426950408889634
ATTN_BLOCK = 128
NEG_INF = -1e30
MAX_DISTANCE = 128
LANES = 128
SUBLANES = 8
VMEM_LIMIT = 56 * 1024 * 1024
IN_PROJ_ROWS, IN_PROJ_COLS = 1024, 1792
RGLRU_STEPS, RGLRU_CHANNELS = 256, 512
MLP_UP_COLS = 2048
MLP_ROWS, MLP_DOWN_K = 1024, 1024
COL_CHUNKS = 4
ADALN_COLS = 1024
ROW_CHUNKS = 2
IN_PROJ_ROW_CHUNKS = 4
QUERY_BLOCKS = 2

F32 = jnp.float32
BF16 = jnp.bfloat16


def _tile(n, target, align):
    best = None
    t = align
    while t <= min(n, target):
        if n % t == 0:
            best = t
        t += align
    if best is None:
        raise ValueError(f"no tile for {n} (target {target}, align {align})")
    return best


def _params(*sem):
    return pltpu.CompilerParams(dimension_semantics=sem, vmem_limit_bytes=VMEM_LIMIT)


def _sigmoid(v):
    return 1.0 / (1.0 + jnp.exp2(v * -LOG2E))


def _rms_mod(x, g, shift, scale):
    var = jnp.mean(x * x, axis=-1, keepdims=True)
    y = x * lax.rsqrt(var + EPS) * g
    return y * (1.0 + scale) + shift


def _adaln_kernel(c_ref, w_ref, b_ref, o_ref):
    c = c_ref[...]
    act = (c * _sigmoid(c)).astype(BF16)
    o_ref[...] = jnp.dot(act, w_ref[...].astype(BF16), preferred_element_type=F32) + b_ref[...]


def _adaln(c, w, b):
    bsz, d = c.shape
    n = w.shape[1]
    tn = _tile(n, ADALN_COLS, LANES)
    return pl.pallas_call(
        _adaln_kernel,
        grid=(n // tn,),
        in_specs=[pl.BlockSpec((bsz, d), lambda j: (0, 0)),
                  pl.BlockSpec((d, tn), lambda j: (0, j)),
                  pl.BlockSpec((1, tn), lambda j: (0, j))],
        out_specs=pl.BlockSpec((bsz, tn), lambda j: (0, j)),
        out_shape=jax.ShapeDtypeStruct((bsz, n), F32),
        compiler_params=_params("parallel"),
        name="adaln_mod",
    )(c, w, b.reshape(1, n))


def _gelu_tanh(v):
    k1 = -2.0 * math.sqrt(2.0 / math.pi) * LOG2E
    return v / (1.0 + jnp.exp2(v * (k1 + (k1 * 0.044715) * (v * v))))


def _rglru_kernel(x_ref, w_ref, cw_ref, cb_ref, ba_ref, bx_ref, lam_ref, o_ref,
                  lx_ref, xtail, hcar, hm_ref, *, conv_k):
    bsz, tt, cw = x_ref.shape
    slabs = cw // LANES
    seq_start = pl.program_id(1) == 0

    @pl.when(seq_start)
    def _():
        xtail[...] = jnp.zeros_like(xtail)
        hcar[...] = jnp.zeros_like(hcar)

    for b in range(bsz):
        xb = x_ref[b].astype(F32)
        for s in range(slabs):
            lx_ref[s, pl.ds(b, tt, stride=SUBLANES), :] = xb[:, s * LANES:(s + 1) * LANES]

    lam = lam_ref[...]
    softplus_neg_lam = jnp.maximum(-lam, 0.0) + jnp.log(1.0 + jnp.exp(-jnp.abs(lam)))
    log2_a_coef = (-LRU_C * LOG2E) * softplus_neg_lam

    def conv_and_gates(s):
        sl = slice(s * LANES, (s + 1) * LANES)
        x3 = lx_ref[s].reshape(tt, SUBLANES, LANES)
        xs = jnp.concatenate([xtail[s], x3], axis=0)
        xtail[s] = xs[tt:]
        xc = cb_ref[:, sl] + cw_ref[conv_k - 1:conv_k, sl] * x3
        for k in range(conv_k - 1):
            xc = xc + cw_ref[k:k + 1, sl] * xs[k:k + tt]
        z = jnp.dot(xc.reshape(tt * SUBLANES, LANES).astype(BF16), w_ref[s],
                    preferred_element_type=F32)
        return xc, z

    ahead = conv_and_gates(0)
    for s in range(slabs):
        sl = slice(s * LANES, (s + 1) * LANES)
        xc, z = ahead
        if s + 1 < slabs:
            ahead = conv_and_gates(s + 1)
        r = _sigmoid(z[:, :LANES] + ba_ref[:, sl]).reshape(tt, SUBLANES, LANES)
        i = _sigmoid(z[:, LANES:] + bx_ref[:, sl]).reshape(tt, SUBLANES, LANES)
        a = jnp.exp2(log2_a_coef[:, sl] * r)
        om = 1.0 - a * a
        mult = jnp.where(om > 0.0, om * lax.rsqrt(om), 0.0)
        mult = jnp.concatenate([jnp.where(seq_start, 1.0, mult[:1]), mult[1:]], axis=0)
        u = mult * (i * xc)
        h = hcar[s]
        for t in range(tt):
            h = a[t] * h + u[t]
            hm_ref[s, t * SUBLANES:(t + 1) * SUBLANES, :] = h
        hcar[s] = h
        for b in range(bsz):
            hb = hm_ref[s, pl.ds(b, tt, stride=SUBLANES), :]
            o_ref[b, :, sl] = hb.astype(o_ref.dtype)


def _rglru(proj, wcat, conv_w, conv_b, ba, bx, lam, bsz, seq, lru_w):
    assert bsz == SUBLANES
    n = proj.shape[1]
    conv_k = conv_w.shape[0]
    tt = _tile(seq, RGLRU_STEPS, SUBLANES)
    cw = _tile(lru_w, RGLRU_CHANNELS, LANES)
    slabs = cw // LANES
    vec = pl.BlockSpec((1, cw), lambda c, t: (0, c))
    rec = pl.pallas_call(
        functools.partial(_rglru_kernel, conv_k=conv_k),
        grid=(lru_w // cw, seq // tt),
        in_specs=[pl.BlockSpec((bsz, tt, cw), lambda c, t: (0, t, c)),
                  pl.BlockSpec((slabs, LANES, 2 * LANES), lambda c, t: (c, 0, 0)),
                  pl.BlockSpec((conv_k, cw), lambda c, t: (0, c)),
                  vec, vec, vec, vec],
        out_specs=pl.BlockSpec((bsz, tt, cw), lambda c, t: (0, t, c)),
        out_shape=jax.ShapeDtypeStruct((bsz, seq, lru_w), BF16),
        scratch_shapes=[pltpu.VMEM((slabs, bsz * tt, LANES), F32),
                        pltpu.VMEM((slabs, conv_k - 1, SUBLANES, LANES), F32),
                        pltpu.VMEM((slabs, SUBLANES, LANES), F32),
                        pltpu.VMEM((slabs, bsz * tt, LANES), F32)],
        compiler_params=_params("parallel", "arbitrary"),
        name="rglru",
    )(proj.reshape(bsz, seq, n), wcat, conv_w, conv_b.reshape(1, -1),
      ba.reshape(1, -1), bx.reshape(1, -1), lam.reshape(1, -1))
    return rec.reshape(bsz * seq, lru_w)


def _in_proj_kernel(x_ref, sh_ref, sc_ref, g_ref, w_ref, o_ref, h_ref, *, sq_relu):
    j = pl.program_id(1)
    rc = x_ref.shape[0] // IN_PROJ_ROW_CHUNKS

    def project(h):
        y = jnp.dot(h, w_ref[...], preferred_element_type=F32)
        if sq_relu:
            y = jnp.square(jnp.maximum(y, 0.0))
        return y.astype(o_ref.dtype)

    @pl.when(j == 0)
    def _():
        for r in range(IN_PROJ_ROW_CHUNKS):
            rows = pl.ds(r * rc, rc)
            h = _rms_mod(x_ref[rows, :], g_ref[...], sh_ref[...], sc_ref[...]).astype(BF16)
            h_ref[rows, :] = h
            o_ref[rows, :] = project(h)

    @pl.when(j > 0)
    def _():
        o_ref[...] = project(h_ref[...])


def _in_proj(x2d, shift, scale, g, w, seq, tn_target, name, sq_relu=False):
    m, d = x2d.shape
    n = w.shape[1]
    tm = _tile(seq, IN_PROJ_ROWS, SUBLANES * IN_PROJ_ROW_CHUNKS)
    tn = _tile(n, tn_target, LANES)
    per_b = seq // tm
    row = lambda i, j: (i // per_b, 0, 0)
    return pl.pallas_call(
        functools.partial(_in_proj_kernel, sq_relu=sq_relu),
        grid=(m // tm, n // tn),
        in_specs=[pl.BlockSpec((tm, d), lambda i, j: (i, 0)),
                  pl.BlockSpec((None, 1, d), row),
                  pl.BlockSpec((None, 1, d), row),
                  pl.BlockSpec((1, d), lambda i, j: (0, 0)),
                  pl.BlockSpec((d, tn), lambda i, j: (0, j))],
        out_specs=pl.BlockSpec((tm, tn), lambda i, j: (i, j)),
        out_shape=jax.ShapeDtypeStruct((m, n), BF16),
        scratch_shapes=[pltpu.VMEM((tm, d), BF16)],
        compiler_params=_params("parallel", "arbitrary"),
        name=name,
    )(x2d, shift, scale, g, w)


def _t5_bucket(rel, n_buckets):
    max_exact = n_buckets // 2
    relf = jnp.maximum(rel, 1).astype(F32)
    large = max_exact + (jnp.log(relf / max_exact) / math.log(MAX_DISTANCE / max_exact)
                         * (n_buckets - max_exact)).astype(jnp.int32)
    large = jnp.minimum(large, n_buckets - 1)
    return jnp.where(rel < max_exact, rel, large)


def _band_bias_kernel(rb_ref, sink_ref, bucket_ref, o_ref, *, pairs, n_buckets):
    kv = pl.program_id(0)
    blk = ATTN_BLOCK
    bucket = bucket_ref[...]
    qi = lax.broadcasted_iota(jnp.int32, (blk, 2 * blk), 0)
    ki = lax.broadcasted_iota(jnp.int32, (blk, 2 * blk), 1)
    rel = qi + blk - ki
    valid = (rel >= 0) & (rel < blk)
    for p in range(pairs):
        for par in range(2):
            h = (kv * pairs + p) * 2 + par
            bias = jnp.zeros((blk, 2 * blk), F32)
            for b in range(n_buckets):
                bias = jnp.where(bucket == b, rb_ref[b, h], bias)
            sink = sink_ref[h]
            rows, cols = slice(p * blk, (p + 1) * blk), slice(par * 2 * blk, (par + 1) * 2 * blk)
            o_ref[rows, cols] = jnp.where(ki == 0, sink, jnp.where(valid, bias, NEG_INF))


def _band_bias(rel_bias, sinks, n_kv):
    n_buckets, n_heads = rel_bias.shape
    pairs = n_heads // n_kv // 2
    blk = ATTN_BLOCK
    qi = jnp.arange(blk)[:, None]
    ki = jnp.arange(2 * blk)[None, :]
    bucket = _t5_bucket(jnp.maximum(qi + blk - ki, 0), n_buckets)
    smem = pl.BlockSpec(memory_space=pltpu.SMEM)
    return pl.pallas_call(
        functools.partial(_band_bias_kernel, pairs=pairs, n_buckets=n_buckets),
        grid=(n_kv,),
        in_specs=[smem, smem, pl.BlockSpec((blk, 2 * blk), lambda kv: (0, 0))],
        out_specs=pl.BlockSpec((None, pairs * blk, 4 * blk), lambda kv: (kv, 0, 0)),
        out_shape=jax.ShapeDtypeStruct((n_kv, pairs * blk, 4 * blk), F32),
        compiler_params=_params("parallel"),
        name="band_bias",
    )(rel_bias, sinks, bucket)


def _swap_lane_halves(v):
    u = pltpu.bitcast(v, jnp.uint32)
    return pltpu.bitcast(pltpu.roll(u, LANES // 2, axis=1), BF16)


def _mix_kernel(*refs, n_kv, pairs, scale, per_b):
    q_ref, kp_ref, kc_ref, vp_ref, vc_ref, bm_ref, h_ref, lg_ref = refs[:8]
    ga_refs, gb_refs = refs[8:8 + n_kv], refs[8 + n_kv:8 + 2 * n_kv]
    (x_ref, gate_ref, wl_ref, wa_ref, wo_ref, o_ref, att_ref, ya_ref, m_ref,
     rec_ref) = refs[8 + 2 * n_kv:]
    rec_ref[...] = (h_ref[...].astype(F32) * _gelu_tanh(lg_ref[...].astype(F32))).astype(BF16)
    blk = ATTN_BLOCK
    d = o_ref.shape[1]
    cc = d // n_kv
    fold = math.log2(scale).is_integer()
    lane = lax.broadcasted_iota(jnp.int32, (blk, LANES), 1)
    key = lax.broadcasted_iota(jnp.int32, (blk, LANES), 0)
    low = jnp.where(lane < LANES // 2, 1.0, 0.0).astype(BF16)
    high = jnp.where(lane < LANES // 2, 0.0, 1.0).astype(BF16)
    not_key0 = jnp.where(key == 0, 0.0, 1.0).astype(BF16)
    ones_blk = jnp.concatenate([low, low, high, high], axis=0)
    key_col = lax.broadcasted_iota(jnp.int32, (1, 4 * blk), 1) % (2 * blk)
    seq_start = pl.program_id(0) % per_b == 0
    start_mask = jnp.where(seq_start & (key_col >= 1) & (key_col < blk), NEG_INF, 0.0)

    def halves(t, kv):
        s = _swap_lane_halves(t)
        lo, hi = (t, s) if kv % 2 == 0 else (s, t)
        return lo * low, hi * high

    def window(prev, cur):
        return jnp.concatenate([prev[0] * not_key0, cur[0], prev[1] * not_key0, cur[1]], axis=0)

    for kv in range(n_kv):
        sl = slice((kv // 2) * LANES, (kv // 2 + 1) * LANES)
        kb = [halves(kp_ref[:, sl], kv)] + [halves(kc_ref[t * blk:(t + 1) * blk, sl], kv)
                                            for t in range(QUERY_BLOCKS)]
        vb = [halves(vp_ref[:, sl], kv)] + [halves(vc_ref[t * blk:(t + 1) * blk, sl], kv)
                                            for t in range(QUERY_BLOCKS)]
        for t in range(QUERY_BLOCKS):
            rows = slice(t * blk, (t + 1) * blk)
            kblk = window(kb[t], kb[t + 1])
            vaug = jnp.concatenate([window(vb[t], vb[t + 1]), ones_blk], axis=1)
            q = jnp.concatenate(
                [q_ref[rows, (kv * pairs + p) * LANES:(kv * pairs + p + 1) * LANES]
                 for p in range(pairs)], axis=0)
            if fold:
                q = q * scale
            s = lax.dot_general(q, kblk, (((1,), (1,)), ((), ())), preferred_element_type=F32)
            if not fold:
                s = s * scale
            s = s + bm_ref[kv]
            if t == 0:
                s = s + start_mask
            es = []
            for par in range(2):
                sp = s[:, par * 2 * blk:(par + 1) * 2 * blk]
                es.append(jnp.exp(sp - jnp.max(sp, axis=-1, keepdims=True)).astype(BF16))
            o = jnp.dot(jnp.concatenate(es, axis=1), vaug, preferred_element_type=F32)
            res = (o[:, :LANES] / o[:, LANES:]).astype(BF16)
            for p in range(pairs):
                att_ref[rows, (kv * pairs + p) * LANES:(kv * pairs + p + 1) * LANES] = (
                    res[p * blk:(p + 1) * blk])
        cols = slice(kv * cc, (kv + 1) * cc)
        ya = jnp.dot(rec_ref[...], wl_ref[:, cols], preferred_element_type=F32)
        ya_ref[:, cols] = _sigmoid(ga_refs[kv][...].astype(F32)) * ya

    for c in range(n_kv):
        cols = slice(c * cc, (c + 1) * cc)
        yb = jnp.dot(att_ref[...], wa_ref[:, cols], preferred_element_type=F32)
        gb = _sigmoid(gb_refs[c][...].astype(F32))
        m_ref[:, cols] = (ya_ref[:, cols] + gb * yb).astype(BF16)
    for c in range(n_kv):
        cols = slice(c * cc, (c + 1) * cc)
        y = jnp.dot(m_ref[...], wo_ref[:, cols], preferred_element_type=F32)
        o_ref[:, cols] = x_ref[:, cols] + gate_ref[:, cols] * y


def _mix(rest, rec, x2d, gate, band_bias, w_lru, w_att, w_out, seq, q_off, k_off, v_off, ga_off,
         gb_off, kv_w, n_heads):
    m, d = x2d.shape
    attn_w = w_att.shape[0]
    lru_w = rec.shape[1]
    hd = attn_w // n_heads
    n_kv = kv_w // hd
    pairs = n_heads // n_kv // 2
    blk = ATTN_BLOCK
    tm = QUERY_BLOCKS * blk
    per_b = seq // tm
    cc = d // n_kv
    assert 2 * hd == LANES and n_heads == 2 * pairs * n_kv and kv_w % LANES == 0
    assert q_off % attn_w == 0 and k_off % kv_w == 0 and v_off % kv_w == 0
    assert seq % tm == 0 and cc % LANES == 0 and ga_off % cc == 0 and gb_off % cc == 0
    kb, vb = k_off // kv_w, v_off // kv_w
    prev = lambda col: (lambda i: (jnp.maximum(QUERY_BLOCKS * i - 1, 0), col))
    col_chunk = lambda off: [pl.BlockSpec((tm, cc), functools.partial(lambda c, i: (i, c),
                                                                      off // cc + c))
                             for c in range(n_kv)]
    resident = lambda a: pl.BlockSpec(a.shape, lambda i: (0,) * a.ndim,
                                      pipeline_mode=pl.Buffered(1))
    kernel = functools.partial(_mix_kernel, n_kv=n_kv, pairs=pairs, scale=hd ** -0.5, per_b=per_b)
    return pl.pallas_call(
        kernel,
        grid=(m // tm,),
        in_specs=[pl.BlockSpec((tm, attn_w), lambda i: (i, q_off // attn_w)),
                  pl.BlockSpec((blk, kv_w), prev(kb)),
                  pl.BlockSpec((tm, kv_w), lambda i: (i, kb)),
                  pl.BlockSpec((blk, kv_w), prev(vb)),
                  pl.BlockSpec((tm, kv_w), lambda i: (i, vb)),
                  resident(band_bias),
                  pl.BlockSpec((tm, lru_w), lambda i: (i, 0)),
                  pl.BlockSpec((tm, lru_w), lambda i: (i, 1))]
                 + col_chunk(ga_off) + col_chunk(gb_off)
                 + [pl.BlockSpec((tm, d), lambda i: (i, 0)),
                    pl.BlockSpec((None, 1, d), lambda i: (i // per_b, 0, 0)),
                    resident(w_lru), resident(w_att), resident(w_out)],
        out_specs=pl.BlockSpec((tm, d), lambda i: (i, 0)),
        out_shape=jax.ShapeDtypeStruct((m, d), F32),
        scratch_shapes=[pltpu.VMEM((tm, attn_w), BF16), pltpu.VMEM((tm, d), F32),
                        pltpu.VMEM((tm, d), BF16), pltpu.VMEM((tm, lru_w), BF16)],
        compiler_params=_params("parallel"),
        name="mix",
    )(rest, rest, rest, rest, rest, band_bias, rec, rest, *([rest] * (2 * n_kv)), x2d, gate,
      w_lru, w_att, w_out)


def _mlp_down_kernel(ff_ref, w2_ref, x_ref, gate_ref, gf_ref, o_ref):
    k = pl.program_id(1)
    last = pl.num_programs(1) - 1
    tm, d = o_ref.shape
    cc = max(d // COL_CHUNKS, LANES)
    rc = tm // ROW_CHUNKS

    @pl.when(k == 0)
    def _():
        for c in range(0, d, cc):
            o_ref[:, c:c + cc] = jnp.dot(ff_ref[...], w2_ref[:, c:c + cc],
                                         preferred_element_type=F32)

    @pl.when((k > 0) & (k < last))
    def _():
        for c in range(0, d, cc):
            o_ref[:, c:c + cc] += jnp.dot(ff_ref[...], w2_ref[:, c:c + cc],
                                          preferred_element_type=F32)

    @pl.when(k == last)
    def _():
        for r in range(ROW_CHUNKS):
            rows = pl.ds(r * rc, rc)
            y = o_ref[rows, :] + jnp.dot(ff_ref[rows, :], w2_ref[...], preferred_element_type=F32)
            x2 = x_ref[rows, :] + gate_ref[...] * y
            var = jnp.mean(x2 * x2, axis=-1, keepdims=True)
            o_ref[rows, :] = x2 * lax.rsqrt(var + EPS) * gf_ref[...]


def _mlp_down(ff, w2, x1, gate, gf, seq):
    m, dff = ff.shape
    d = w2.shape[1]
    tm = _tile(seq, MLP_ROWS, SUBLANES * ROW_CHUNKS)
    tk = _tile(dff // 2, MLP_DOWN_K, LANES)
    per_b = seq // tm
    return pl.pallas_call(
        _mlp_down_kernel,
        grid=(m // tm, dff // tk),
        in_specs=[pl.BlockSpec((tm, tk), lambda i, k: (i, k)),
                  pl.BlockSpec((tk, d), lambda i, k: (k, 0)),
                  pl.BlockSpec((tm, d), lambda i, k: (i, 0)),
                  pl.BlockSpec((None, 1, d), lambda i, k: (i // per_b, 0, 0)),
                  pl.BlockSpec((1, d), lambda i, k: (0, 0))],
        out_specs=pl.BlockSpec((tm, d), lambda i, k: (i, 0)),
        out_shape=jax.ShapeDtypeStruct((m, d), F32),
        compiler_params=_params("parallel", "arbitrary"),
        name="mlp_down",
    )(ff, w2, x1, gate, gf)


def kernel(x, c, w_ada, b_ada, norm1_g, w_in, conv_w, conv_b, lru_wa, lru_ba, lru_wx, lru_bx,
           lru_lambda, w_lru_out, w_attn_out, attn_sinks, rel_bias, w_out, norm2_g, w_ff1, w_ff2,
           final_g):
    bsz, seq, d = x.shape
    depth = w_in.shape[0]
    lru_w = w_lru_out.shape[1]
    attn_w = w_attn_out.shape[1]
    kv_w = (w_in.shape[2] - 2 * lru_w - attn_w - 2 * d) // 2
    n_heads = attn_sinks.shape[1]
    n_kv = kv_w // (attn_w // n_heads)
    q_off = 2 * lru_w
    k_off = q_off + attn_w
    v_off = k_off + kv_w
    ga_off = v_off + kv_w
    gb_off = ga_off + d

    xs = x.reshape(bsz * seq, d)
    for l in range(depth):
        band_bias = _band_bias(rel_bias, attn_sinks[l], n_kv)
        mod = _adaln(c, w_ada[l], b_ada[l])
        shift1, scale1, gate1, shift2, scale2, gate2 = [
            t.reshape(bsz, 1, d) for t in jnp.split(mod, 6, axis=-1)]

        rest = _in_proj(xs, shift1, scale1, norm1_g[l].reshape(1, d), w_in[l].astype(BF16), seq,
                        IN_PROJ_COLS, "in_proj")
        wcat = jnp.concatenate([lru_wa[l], lru_wx[l]], axis=-1).astype(BF16)
        rec = _rglru(rest, wcat, conv_w[l], conv_b[l], lru_ba[l], lru_bx[l], lru_lambda[l],
                     bsz, seq, lru_w)
        x1 = _mix(rest, rec, xs, gate1, band_bias, w_lru_out[l].astype(BF16),
                  w_attn_out[l].astype(BF16), w_out[l].astype(BF16), seq, q_off, k_off, v_off,
                  ga_off, gb_off, kv_w, n_heads)
        if l != depth - 1:
            raise NotImplementedError("the fused final norm assumes a single layer")
        ff = _in_proj(x1, shift2, scale2, norm2_g[l].reshape(1, d), w_ff1[l].astype(BF16), seq,
                      MLP_UP_COLS, "mlp_up", sq_relu=True)
        xs = _mlp_down(ff, w_ff2[l].astype(BF16), x1, gate2, final_g.reshape(1, d), seq)
    return xs.reshape(bsz, seq, d)
```

```python
import functools
import math

import jax
import jax.numpy as jnp
from jax import lax
from jax.experimental import pallas as pl
from jax.experimental.pallas import tpu as pltpu

EPS = 1e-6
LRU_C = 8.0
LOG2E = 1.4426950408889634
ATTN_BLOCK = 128
NEG_INF = -1e30
MAX_DISTANCE = 128
LANES = 128
SUBLANES = 8
VMEM_LIMIT = 56 * 1024 * 1024
IN_PROJ_ROWS, IN_PROJ_COLS = 1024, 1792
RGLRU_STEPS, RGLRU_CHANNELS = 256, 512
MLP_UP_COLS = 2048
MLP_ROWS, MLP_DOWN_K = 1024, 1024
COL_CHUNKS = 4
ADALN_COLS = 1024
ROW_CHUNKS = 2
IN_PROJ_ROW_CHUNKS = 4
QUERY_BLOCKS = 2

F32 = jnp.float32
BF16 = jnp.bfloat16


def _tile(n, target, align):
    best = None
    t = align
    while t <= min(n, target):
        if n % t == 0:
            best = t
        t += align
    if best is None:
        raise ValueError(f"no tile for {n} (target {target}, align {align})")
    return best


def _params(*sem):
    return pltpu.CompilerParams(dimension_semantics=sem, vmem_limit_bytes=VMEM_LIMIT)


def _sigmoid(v):
    return 1.0 / (1.0 + jnp.exp2(v * -LOG2E))


def _rms_mod(x, g, shift, scale):
    var = jnp.mean(x * x, axis=-1, keepdims=True)
    y = x * lax.rsqrt(var + EPS) * g
    return y * (1.0 + scale) + shift


def _adaln_kernel(c_ref, w_ref, b_ref, o_ref):
    c = c_ref[...]
    act = (c * _sigmoid(c)).astype(BF16)
    o_ref[...] = jnp.dot(act, w_ref[...].astype(BF16), preferred_element_type=F32) + b_ref[...]


def _adaln(c, w, b):
    bsz, d = c.shape
    n = w.shape[1]
    tn = _tile(n, ADALN_COLS, LANES)
    return pl.pallas_call(
        _adaln_kernel,
        grid=(n // tn,),
        in_specs=[pl.BlockSpec((bsz, d), lambda j: (0, 0)),
                  pl.BlockSpec((d, tn), lambda j: (0, j)),
                  pl.BlockSpec((1, tn), lambda j: (0, j))],
        out_specs=pl.BlockSpec((bsz, tn), lambda j: (0, j)),
        out_shape=jax.ShapeDtypeStruct((bsz, n), F32),
        compiler_params=_params("parallel"),
        name="adaln_mod",
    )(c, w, b.reshape(1, n))


def _gelu_tanh(v):
    k1 = -2.0 * math.sqrt(2.0 / math.pi) * LOG2E
    return v / (1.0 + jnp.exp2(v * (k1 + (k1 * 0.044715) * (v * v))))


def _rglru_kernel(x_ref, w_ref, cw_ref, cb_ref, ba_ref, bx_ref, lam_ref, o_ref,
                  lx_ref, xtail, hcar, hm_ref, *, conv_k):
    bsz, tt, cw = x_ref.shape
    slabs = cw // LANES
    seq_start = pl.program_id(1) == 0

    @pl.when(seq_start)
    def _():
        xtail[...] = jnp.zeros_like(xtail)
        hcar[...] = jnp.zeros_like(hcar)

    for b in range(bsz):
        xb = x_ref[b].astype(F32)
        for s in range(slabs):
            lx_ref[s, pl.ds(b, tt, stride=SUBLANES), :] = xb[:, s * LANES:(s + 1) * LANES]

    lam = lam_ref[...]
    softplus_neg_lam = jnp.maximum(-lam, 0.0) + jnp.log(1.0 + jnp.exp(-jnp.abs(lam)))
    log2_a_coef = (-LRU_C * LOG2E) * softplus_neg_lam

    def conv_and_gates(s):
        sl = slice(s * LANES, (s + 1) * LANES)
        x3 = lx_ref[s].reshape(tt, SUBLANES, LANES)
        xs = jnp.concatenate([xtail[s], x3], axis=0)
        xtail[s] = xs[tt:]
        xc = cb_ref[:, sl] + cw_ref[conv_k - 1:conv_k, sl] * x3
        for k in range(conv_k - 1):
            xc = xc + cw_ref[k:k + 1, sl] * xs[k:k + tt]
        z = jnp.dot(xc.reshape(tt * SUBLANES, LANES).astype(BF16), w_ref[s],
                    preferred_element_type=F32)
        return xc, z

    ahead = conv_and_gates(0)
    for s in range(slabs):
        sl = slice(s * LANES, (s + 1) * LANES)
        xc, z = ahead
        if s + 1 < slabs:
            ahead = conv_and_gates(s + 1)
        r = _sigmoid(z[:, :LANES] + ba_ref[:, sl]).reshape(tt, SUBLANES, LANES)
        i = _sigmoid(z[:, LANES:] + bx_ref[:, sl]).reshape(tt, SUBLANES, LANES)
        a = jnp.exp2(log2_a_coef[:, sl] * r)
        om = 1.0 - a * a
        mult = jnp.where(om > 0.0, om * lax.rsqrt(om), 0.0)
        mult = jnp.concatenate([jnp.where(seq_start, 1.0, mult[:1]), mult[1:]], axis=0)
        u = mult * (i * xc)
        h = hcar[s]
        for t in range(tt):
            h = a[t] * h + u[t]
            hm_ref[s, t * SUBLANES:(t + 1) * SUBLANES, :] = h
        hcar[s] = h
        for b in range(bsz):
            hb = hm_ref[s, pl.ds(b, tt, stride=SUBLANES), :]
            o_ref[b, :, sl] = hb.astype(o_ref.dtype)


def _rglru(proj, wcat, conv_w, conv_b, ba, bx, lam, bsz, seq, lru_w):
    assert bsz == SUBLANES
    n = proj.shape[1]
    conv_k = conv_w.shape[0]
    tt = _tile(seq, RGLRU_STEPS, SUBLANES)
    cw = _tile(lru_w, RGLRU_CHANNELS, LANES)
    slabs = cw // LANES
    vec = pl.BlockSpec((1, cw), lambda c, t: (0, c))
    rec = pl.pallas_call(
        functools.partial(_rglru_kernel, conv_k=conv_k),
        grid=(lru_w // cw, seq // tt),
        in_specs=[pl.BlockSpec((bsz, tt, cw), lambda c, t: (0, t, c)),
                  pl.BlockSpec((slabs, LANES, 2 * LANES), lambda c, t: (c, 0, 0)),
                  pl.BlockSpec((conv_k, cw), lambda c, t: (0, c)),
                  vec, vec, vec, vec],
        out_specs=pl.BlockSpec((bsz, tt, cw), lambda c, t: (0, t, c)),
        out_shape=jax.ShapeDtypeStruct((bsz, seq, lru_w), BF16),
        scratch_shapes=[pltpu.VMEM((slabs, bsz * tt, LANES), F32),
                        pltpu.VMEM((slabs, conv_k - 1, SUBLANES, LANES), F32),
                        pltpu.VMEM((slabs, SUBLANES, LANES), F32),
                        pltpu.VMEM((slabs, bsz * tt, LANES), F32)],
        compiler_params=_params("parallel", "arbitrary"),
        name="rglru",
    )(proj.reshape(bsz, seq, n), wcat, conv_w, conv_b.reshape(1, -1),
      ba.reshape(1, -1), bx.reshape(1, -1), lam.reshape(1, -1))
    return rec.reshape(bsz * seq, lru_w)


def _in_proj_kernel(x_ref, sh_ref, sc_ref, g_ref, w_ref, o_ref, h_ref, *, sq_relu):
    j = pl.program_id(1)
    rc = x_ref.shape[0] // IN_PROJ_ROW_CHUNKS

    def project(h):
        y = jnp.dot(h, w_ref[...], preferred_element_type=F32)
        if sq_relu:
            y = jnp.square(jnp.maximum(y, 0.0))
        return y.astype(o_ref.dtype)

    @pl.when(j == 0)
    def _():
        for r in range(IN_PROJ_ROW_CHUNKS):
            rows = pl.ds(r * rc, rc)
            h = _rms_mod(x_ref[rows, :], g_ref[...], sh_ref[...], sc_ref[...]).astype(BF16)
            h_ref[rows, :] = h
            o_ref[rows, :] = project(h)

    @pl.when(j > 0)
    def _():
        o_ref[...] = project(h_ref[...])


def _in_proj(x2d, shift, scale, g, w, seq, tn_target, name, sq_relu=False):
    m, d = x2d.shape
    n = w.shape[1]
    tm = _tile(seq, IN_PROJ_ROWS, SUBLANES * IN_PROJ_ROW_CHUNKS)
    tn = _tile(n, tn_target, LANES)
    per_b = seq // tm
    row = lambda i, j: (i // per_b, 0, 0)
    return pl.pallas_call(
        functools.partial(_in_proj_kernel, sq_relu=sq_relu),
        grid=(m // tm, n // tn),
        in_specs=[pl.BlockSpec((tm, d), lambda i, j: (i, 0)),
                  pl.BlockSpec((None, 1, d), row),
                  pl.BlockSpec((None, 1, d), row),
                  pl.BlockSpec((1, d), lambda i, j: (0, 0)),
                  pl.BlockSpec((d, tn), lambda i, j: (0, j))],
        out_specs=pl.BlockSpec((tm, tn), lambda i, j: (i, j)),
        out_shape=jax.ShapeDtypeStruct((m, n), BF16),
        scratch_shapes=[pltpu.VMEM((tm, d), BF16)],
        compiler_params=_params("parallel", "arbitrary"),
        name=name,
    )(x2d, shift, scale, g, w)


def _t5_bucket(rel, n_buckets):
    max_exact = n_buckets // 2
    relf = jnp.maximum(rel, 1).astype(F32)
    large = max_exact + (jnp.log(relf / max_exact) / math.log(MAX_DISTANCE / max_exact)
                         * (n_buckets - max_exact)).astype(jnp.int32)
    large = jnp.minimum(large, n_buckets - 1)
    return jnp.where(rel < max_exact, rel, large)


def _band_bias_kernel(rb_ref, sink_ref, bucket_ref, o_ref, *, pairs, n_buckets):
    kv = pl.program_id(0)
    blk = ATTN_BLOCK
    bucket = bucket_ref[...]
    qi = lax.broadcasted_iota(jnp.int32, (blk, 2 * blk), 0)
    ki = lax.broadcasted_iota(jnp.int32, (blk, 2 * blk), 1)
    rel = qi + blk - ki
    valid = (rel >= 0) & (rel < blk)
    for p in range(pairs):
        for par in range(2):
            h = (kv * pairs + p) * 2 + par
            bias = jnp.zeros((blk, 2 * blk), F32)
            for b in range(n_buckets):
                bias = jnp.where(bucket == b, rb_ref[b, h], bias)
            sink = sink_ref[h]
            rows, cols = slice(p * blk, (p + 1) * blk), slice(par * 2 * blk, (par + 1) * 2 * blk)
            o_ref[rows, cols] = jnp.where(ki == 0, sink, jnp.where(valid, bias, NEG_INF))


def _band_bias(rel_bias, sinks, n_kv):
    n_buckets, n_heads = rel_bias.shape
    pairs = n_heads // n_kv // 2
    blk = ATTN_BLOCK
    qi = jnp.arange(blk)[:, None]
    ki = jnp.arange(2 * blk)[None, :]
    bucket = _t5_bucket(jnp.maximum(qi + blk - ki, 0), n_buckets)
    smem = pl.BlockSpec(memory_space=pltpu.SMEM)
    return pl.pallas_call(
        functools.partial(_band_bias_kernel, pairs=pairs, n_buckets=n_buckets),
        grid=(n_kv,),
        in_specs=[smem, smem, pl.BlockSpec((blk, 2 * blk), lambda kv: (0, 0))],
        out_specs=pl.BlockSpec((None, pairs * blk, 4 * blk), lambda kv: (kv, 0, 0)),
        out_shape=jax.ShapeDtypeStruct((n_kv, pairs * blk, 4 * blk), F32),
        compiler_params=_params("parallel"),
        name="band_bias",
    )(rel_bias, sinks, bucket)


def _swap_lane_halves(v):
    u = pltpu.bitcast(v, jnp.uint32)
    return pltpu.bitcast(pltpu.roll(u, LANES // 2, axis=1), BF16)


def _mix_kernel(*refs, n_kv, pairs, scale, per_b):
    q_ref, kp_ref, kc_ref, vp_ref, vc_ref, bm_ref, h_ref, lg_ref = refs[:8]
    ga_refs, gb_refs = refs[8:8 + n_kv], refs[8 + n_kv:8 + 2 * n_kv]
    (x_ref, gate_ref, wl_ref, wa_ref, wo_ref, o_ref, att_ref, ya_ref, m_ref,
     rec_ref) = refs[8 + 2 * n_kv:]
    blk = ATTN_BLOCK
    d = o_ref.shape[1]
    cc = d // n_kv
    fold = math.log2(scale).is_integer()
    lane = lax.broadcasted_iota(jnp.int32, (blk, LANES), 1)
    key = lax.broadcasted_iota(jnp.int32, (blk, LANES), 0)
    low = jnp.where(lane < LANES // 2, 1.0, 0.0).astype(BF16)
    high = jnp.where(lane < LANES // 2, 0.0, 1.0).astype(BF16)
    not_key0 = jnp.where(key == 0, 0.0, 1.0).astype(BF16)
    ones_blk = jnp.concatenate([low, low, high, high], axis=0)
    key_col = lax.broadcasted_iota(jnp.int32, (1, 4 * blk), 1) % (2 * blk)
    seq_start = pl.program_id(0) % per_b == 0
    start_mask = jnp.where(seq_start & (key_col >= 1) & (key_col < blk), NEG_INF, 0.0)

    def halves(t, kv):
        s = _swap_lane_halves(t)
        lo, hi = (t, s) if kv % 2 == 0 else (s, t)
        return lo * low, hi * high

    def window(prev, cur):
        return jnp.concatenate([prev[0] * not_key0, cur[0], prev[1] * not_key0, cur[1]], axis=0)

    for kv in range(n_kv):
        sl = slice((kv // 2) * LANES, (kv // 2 + 1) * LANES)
        kb = [halves(kp_ref[:, sl], kv)] + [halves(kc_ref[t * blk:(t + 1) * blk, sl], kv)
                                            for t in range(QUERY_BLOCKS)]
        vb = [halves(vp_ref[:, sl], kv)] + [halves(vc_ref[t * blk:(t + 1) * blk, sl], kv)
                                            for t in range(QUERY_BLOCKS)]
        for t in range(QUERY_BLOCKS):
            rows = slice(t * blk, (t + 1) * blk)
            kblk = window(kb[t], kb[t + 1])
            vaug = jnp.concatenate([window(vb[t], vb[t + 1]), ones_blk], axis=1)
            q = jnp.concatenate(
                [q_ref[rows, (kv * pairs + p) * LANES:(kv * pairs + p + 1) * LANES]
                 for p in range(pairs)], axis=0)
            if fold:
                q = q * scale
            s = lax.dot_general(q, kblk, (((1,), (1,)), ((), ())), preferred_element_type=F32)
            if not fold:
                s = s * scale
            s = s + bm_ref[kv]
            if t == 0:
                s = s + start_mask
            es = []
            for par in range(2):
                sp = s[:, par * 2 * blk:(par + 1) * 2 * blk]
                es.append(jnp.exp(sp - jnp.max(sp, axis=-1, keepdims=True)).astype(BF16))
            o = jnp.dot(jnp.concatenate(es, axis=1), vaug, preferred_element_type=F32)
            res = (o[:, :LANES] / o[:, LANES:]).astype(BF16)
            for p in range(pairs):
                att_ref[rows, (kv * pairs + p) * LANES:(kv * pairs + p + 1) * LANES] = (
                    res[p * blk:(p + 1) * blk])
        cols = slice(kv * cc, (kv + 1) * cc)
        if kv == 0:
            kc = rec_ref.shape[1] // n_kv
            ya = jnp.zeros((rec_ref.shape[0], cc), F32)
            for c in range(n_kv):
                kcols = slice(c * kc, (c + 1) * kc)
                rec_ref[:, kcols] = (h_ref[:, kcols].astype(F32)
                                     * _gelu_tanh(lg_ref[:, kcols].astype(F32))).astype(BF16)
                ya = ya + jnp.dot(rec_ref[:, kcols], wl_ref[kcols, cols],
                                  preferred_element_type=F32)
        else:
            ya = jnp.dot(rec_ref[...], wl_ref[:, cols], preferred_element_type=F32)
        ya_ref[:, cols] = _sigmoid(ga_refs[kv][...].astype(F32)) * ya

    for c in range(n_kv):
        cols = slice(c * cc, (c + 1) * cc)
        yb = jnp.dot(att_ref[...], wa_ref[:, cols], preferred_element_type=F32)
        gb = _sigmoid(gb_refs[c][...].astype(F32))
        m_ref[:, cols] = (ya_ref[:, cols] + gb * yb).astype(BF16)
    for c in range(n_kv):
        cols = slice(c * cc, (c + 1) * cc)
        y = jnp.dot(m_ref[...], wo_ref[:, cols], preferred_element_type=F32)
        o_ref[:, cols] = x_ref[:, cols] + gate_ref[:, cols] * y


def _mix(rest, rec, x2d, gate, band_bias, w_lru, w_att, w_out, seq, q_off, k_off, v_off, ga_off,
         gb_off, kv_w, n_heads):
    m, d = x2d.shape
    attn_w = w_att.shape[0]
    lru_w = rec.shape[1]
    hd = attn_w // n_heads
    n_kv = kv_w // hd
    pairs = n_heads // n_kv // 2
    blk = ATTN_BLOCK
    tm = QUERY_BLOCKS * blk
    per_b = seq // tm
    cc = d // n_kv
    assert 2 * hd == LANES and n_heads == 2 * pairs * n_kv and kv_w % LANES == 0
    assert q_off % attn_w == 0 and k_off % kv_w == 0 and v_off % kv_w == 0
    assert seq % tm == 0 and cc % LANES == 0 and ga_off % cc == 0 and gb_off % cc == 0
    kb, vb = k_off // kv_w, v_off // kv_w
    prev = lambda col: (lambda i: (jnp.maximum(QUERY_BLOCKS * i - 1, 0), col))
    col_chunk = lambda off: [pl.BlockSpec((tm, cc), functools.partial(lambda c, i: (i, c),
                                                                      off // cc + c))
                             for c in range(n_kv)]
    resident = lambda a: pl.BlockSpec(a.shape, lambda i: (0,) * a.ndim,
                                      pipeline_mode=pl.Buffered(1))
    kernel = functools.partial(_mix_kernel, n_kv=n_kv, pairs=pairs, scale=hd ** -0.5, per_b=per_b)
    return pl.pallas_call(
        kernel,
        grid=(m // tm,),
        in_specs=[pl.BlockSpec((tm, attn_w), lambda i: (i, q_off // attn_w)),
                  pl.BlockSpec((blk, kv_w), prev(kb)),
                  pl.BlockSpec((tm, kv_w), lambda i: (i, kb)),
                  pl.BlockSpec((blk, kv_w), prev(vb)),
                  pl.BlockSpec((tm, kv_w), lambda i: (i, vb)),
                  resident(band_bias),
                  pl.BlockSpec((tm, lru_w), lambda i: (i, 0)),
                  pl.BlockSpec((tm, lru_w), lambda i: (i, 1))]
                 + col_chunk(ga_off) + col_chunk(gb_off)
                 + [pl.BlockSpec((tm, d), lambda i: (i, 0)),
                    pl.BlockSpec((None, 1, d), lambda i: (i // per_b, 0, 0)),
                    resident(w_lru), resident(w_att), resident(w_out)],
        out_specs=pl.BlockSpec((tm, d), lambda i: (i, 0)),
        out_shape=jax.ShapeDtypeStruct((m, d), F32),
        scratch_shapes=[pltpu.VMEM((tm, attn_w), BF16), pltpu.VMEM((tm, d), F32),
                        pltpu.VMEM((tm, d), BF16), pltpu.VMEM((tm, lru_w), BF16)],
        compiler_params=_params("parallel"),
        name="mix",
    )(rest, rest, rest, rest, rest, band_bias, rec, rest, *([rest] * (2 * n_kv)), x2d, gate,
      w_lru, w_att, w_out)


def _mlp_down_kernel(ff_ref, w2_ref, x_ref, gate_ref, gf_ref, o_ref):
    k = pl.program_id(1)
    last = pl.num_programs(1) - 1
    tm, d = o_ref.shape
    cc = max(d // COL_CHUNKS, LANES)
    rc = tm // ROW_CHUNKS

    @pl.when(k == 0)
    def _():
        for c in range(0, d, cc):
            o_ref[:, c:c + cc] = jnp.dot(ff_ref[...], w2_ref[:, c:c + cc],
                                         preferred_element_type=F32)

    @pl.when((k > 0) & (k < last))
    def _():
        for c in range(0, d, cc):
            o_ref[:, c:c + cc] += jnp.dot(ff_ref[...], w2_ref[:, c:c + cc],
                                          preferred_element_type=F32)

    @pl.when(k == last)
    def _():
        for r in range(ROW_CHUNKS):
            rows = pl.ds(r * rc, rc)
            y = o_ref[rows, :] + jnp.dot(ff_ref[rows, :], w2_ref[...], preferred_element_type=F32)
            x2 = x_ref[rows, :] + gate_ref[...] * y
            var = jnp.mean(x2 * x2, axis=-1, keepdims=True)
            o_ref[rows, :] = x2 * lax.rsqrt(var + EPS) * gf_ref[...]


def _mlp_down(ff, w2, x1, gate, gf, seq):
    m, dff = ff.shape
    d = w2.shape[1]
    tm = _tile(seq, MLP_ROWS, SUBLANES * ROW_CHUNKS)
    tk = _tile(dff // 2, MLP_DOWN_K, LANES)
    per_b = seq // tm
    return pl.pallas_call(
        _mlp_down_kernel,
        grid=(m // tm, dff // tk),
        in_specs=[pl.BlockSpec((tm, tk), lambda i, k: (i, k)),
                  pl.BlockSpec((tk, d), lambda i, k: (k, 0)),
                  pl.BlockSpec((tm, d), lambda i, k: (i, 0)),
                  pl.BlockSpec((None, 1, d), lambda i, k: (i // per_b, 0, 0)),
                  pl.BlockSpec((1, d), lambda i, k: (0, 0))],
        out_specs=pl.BlockSpec((tm, d), lambda i, k: (i, 0)),
        out_shape=jax.ShapeDtypeStruct((m, d), F32),
        compiler_params=_params("parallel", "arbitrary"),
        name="mlp_down",
    )(ff, w2, x1, gate, gf)


def kernel(x, c, w_ada, b_ada, norm1_g, w_in, conv_w, conv_b, lru_wa, lru_ba, lru_wx, lru_bx,
           lru_lambda, w_lru_out, w_attn_out, attn_sinks, rel_bias, w_out, norm2_g, w_ff1, w_ff2,
           final_g):
    bsz, seq, d = x.shape
    depth = w_in.shape[0]
    lru_w = w_lru_out.shape[1]
    attn_w = w_attn_out.shape[1]
    kv_w = (w_in.shape[2] - 2 * lru_w - attn_w - 2 * d) // 2
    n_heads = attn_sinks.shape[1]
    n_kv = kv_w // (attn_w // n_heads)
    q_off = 2 * lru_w
    k_off = q_off + attn_w
    v_off = k_off + kv_w
    ga_off = v_off + kv_w
    gb_off = ga_off + d

    xs = x.reshape(bsz * seq, d)
    for l in range(depth):
        band_bias = _band_bias(rel_bias, attn_sinks[l], n_kv)
        mod = _adaln(c, w_ada[l], b_ada[l])
        shift1, scale1, gate1, shift2, scale2, gate2 = [
            t.reshape(bsz, 1, d) for t in jnp.split(mod, 6, axis=-1)]

        rest = _in_proj(xs, shift1, scale1, norm1_g[l].reshape(1, d), w_in[l].astype(BF16), seq,
                        IN_PROJ_COLS, "in_proj")
        wcat = jnp.concatenate([lru_wa[l], lru_wx[l]], axis=-1).astype(BF16)
        rec = _rglru(rest, wcat, conv_w[l], conv_b[l], lru_ba[l], lru_bx[l], lru_lambda[l],
                     bsz, seq, lru_w)
        x1 = _mix(rest, rec, xs, gate1, band_bias, w_lru_out[l].astype(BF16),
                  w_attn_out[l].astype(BF16), w_out[l].astype(BF16), seq, q_off, k_off, v_off,
                  ga_off, gb_off, kv_w, n_heads)
        if l != depth - 1:
            raise NotImplementedError("the fused final norm assumes a single layer")
        ff = _in_proj(x1, shift2, scale2, norm2_g[l].reshape(1, d), w_ff1[l].astype(BF16), seq,
                      MLP_UP_COLS, "mlp_up", sq_relu=True)
        xs = _mlp_down(ff, w_ff2[l].astype(BF16), x1, gate2, final_g.reshape(1, d), seq)
    return xs.reshape(bsz, seq, d)
```
